```python
import math
import jax
import jax.numpy as jnp
from jax import lax
import numpy as np

D_MODEL = 2048
BATCH = 2
SEQ = 4096
DEPTH = 4

GRID_W = 64
CTX_LEN = 256
N_MIXERS = 3
Q_BLOCK = 128
ROPE_BASE = 10000.0
NORM_EPS = 1e-6

MLA_HEADS = 16
MLA_Q_LORA = 512
MLA_KV_LORA = 512
MLA_NOPE = 128
MLA_ROPE = 64
MLA_V = 128

DIFF_HEADS = 8
DIFF_HEAD_DIM = 128

RET_HEADS = 8
RET_DK = 256
RET_DV = 512
RET_CHUNK = 128

N_EXPERTS = 16
N_GROUPS = 4
EXPERTS_PER_GROUP = N_EXPERTS // N_GROUPS
TOP_K = 2
D_EXPERT = 1024

N_MLA_LAYERS = len(range(0, DEPTH, N_MIXERS))
N_DIFF_LAYERS = len(range(1, DEPTH, N_MIXERS))
N_RET_LAYERS = len(range(2, DEPTH, N_MIXERS))

kernel_name = 'hybrid_mla_diffattn_retnet_groupmoe_dit'


def _rmsnorm(x, g):
    xf = x.astype(jnp.float32)
    y = xf * lax.rsqrt(jnp.mean(xf * xf, axis=-1, keepdims=True) + NORM_EPS)
    return y.astype(x.dtype) * g


def _modulate(h, shift, scale):
    return h * (1.0 + scale) + shift


def _axial_rope_tables(rows, rot_dim):
    n_freq = rot_dim // 4
    inv_freq = jnp.power(ROPE_BASE, -jnp.arange(n_freq, dtype=jnp.float32) / n_freq)
    row = jnp.repeat(jnp.arange(rows, dtype=jnp.float32), GRID_W)
    col = jnp.tile(jnp.arange(GRID_W, dtype=jnp.float32), rows)
    ang = jnp.concatenate([row[:, None] * inv_freq, col[:, None] * inv_freq], axis=-1)
    return jnp.cos(ang), jnp.sin(ang)


def _apply_rope(x, cos, sin):
    half = x.shape[-1] // 2
    extra = [1] * (x.ndim - 3)
    c = cos.reshape(cos.shape[0], *extra, cos.shape[1])
    s = sin.reshape(sin.shape[0], *extra, sin.shape[1])
    x1, x2 = x[..., :half], x[..., half:]
    return jnp.concatenate([x1 * c - x2 * s, x2 * c + x1 * s], axis=-1).astype(x.dtype)


def _heads_first(a):
    return a.transpose(0, 2, 1, 3)


def _heads_last(o):
    b, h, s, d = o.shape
    return o.transpose(0, 2, 1, 3).reshape(b, s, h * d)


def _split_query_blocks(q):
    b, h, s, d = q.shape
    return q.reshape(b, h, s // Q_BLOCK, Q_BLOCK, d).transpose(2, 0, 1, 3, 4)


def _merge_query_blocks(o):
    nb, b, h, qb, d = o.shape
    return o.transpose(1, 2, 0, 3, 4).reshape(b, h, nb * qb, d)


def _softmax_attend(q, k, v, scale):
    kf = k.astype(jnp.float32)
    vf = v.astype(jnp.float32)

    def block(qb):
        s = jnp.einsum('bhqd,bhkd->bhqk', qb.astype(jnp.float32), kf) * scale
        p = jax.nn.softmax(s, axis=-1)
        return jnp.einsum('bhqk,bhkd->bhqd', p, vf)

    return _merge_query_blocks(lax.map(block, _split_query_blocks(q)))


def _diff_attend(q1, q2, k1, k2, v, lam, scale):
    k1f = k1.astype(jnp.float32)
    k2f = k2.astype(jnp.float32)
    vf = v.astype(jnp.float32)

    def block(qs):
        qb1, qb2 = qs
        p1 = jax.nn.softmax(jnp.einsum('bhqd,bhkd->bhqk', qb1.astype(jnp.float32), k1f) * scale, axis=-1)
        p2 = jax.nn.softmax(jnp.einsum('bhqd,bhkd->bhqk', qb2.astype(jnp.float32), k2f) * scale, axis=-1)
        return jnp.einsum('bhqk,bhkd->bhqd', p1 - lam * p2, vf)

    return _merge_query_blocks(lax.map(block, (_split_query_blocks(q1), _split_query_blocks(q2))))


def _mla_project(t, w_in, q_a_g, w_q_b, kv_a_g, w_kv_b, q_g, k_g):
    b, s, _ = t.shape
    z = t @ w_in
    c_q = z[..., :MLA_Q_LORA]
    c_kv = z[..., MLA_Q_LORA:MLA_Q_LORA + MLA_KV_LORA]
    k_rot = z[..., MLA_Q_LORA + MLA_KV_LORA:]
    q = (_rmsnorm(c_q, q_a_g) @ w_q_b).reshape(b, s, MLA_HEADS, MLA_NOPE + MLA_ROPE)
    kv = (_rmsnorm(c_kv, kv_a_g) @ w_kv_b).reshape(b, s, MLA_HEADS, MLA_NOPE + MLA_V)
    k_nope, v = kv[..., :MLA_NOPE], kv[..., MLA_NOPE:]
    k = jnp.concatenate([k_nope, jnp.broadcast_to(k_rot[:, :, None, :], (b, s, MLA_HEADS, MLA_ROPE))], axis=-1)
    return _rmsnorm(q, q_g), _rmsnorm(k, k_g), v


def _rope_tail(x, cos, sin):
    return jnp.concatenate([x[..., :MLA_NOPE], _apply_rope(x[..., MLA_NOPE:], cos, sin)], axis=-1)


def _mla_mixer(h, hc, w_in, q_a_g, w_q_b, kv_a_g, w_kv_b, q_g, k_g, w_o, cos, sin, need_ctx):
    q, k, v = _mla_project(h, w_in, q_a_g, w_q_b, kv_a_g, w_kv_b, q_g, k_g)
    q, k = _rope_tail(q, cos, sin), _rope_tail(k, cos, sin)
    qc, kc, vc = _mla_project(hc, w_in, q_a_g, w_q_b, kv_a_g, w_kv_b, q_g, k_g)
    q, k, v, qc, kc, vc = (_heads_first(a) for a in (q, k, v, qc, kc, vc))
    scale = (MLA_NOPE + MLA_ROPE) ** -0.5
    o = _softmax_attend(q, jnp.concatenate([k, kc], axis=2), jnp.concatenate([v, vc], axis=2), scale)
    out = (_heads_last(o) @ w_o).astype(h.dtype)
    out_c = (_heads_last(_softmax_attend(qc, kc, vc, scale)) @ w_o).astype(hc.dtype) if need_ctx else None
    return out, out_c


def _diff_project(t, w_in, q_g, k_g):
    b, s, _ = t.shape
    hd = DIFF_HEADS * 2 * DIFF_HEAD_DIM
    z = t @ w_in
    q = _rmsnorm(z[..., :hd].reshape(b, s, DIFF_HEADS, 2, DIFF_HEAD_DIM), q_g)
    k = _rmsnorm(z[..., hd:2 * hd].reshape(b, s, DIFF_HEADS, 2, DIFF_HEAD_DIM), k_g)
    v = z[..., 2 * hd:].reshape(b, s, DIFF_HEADS, 2 * DIFF_HEAD_DIM)
    return q, k, v


def _diff_mixer(h, hc, w_in, q_g, k_g, lambdas, subln_g, w_o, cos, sin, layer_idx, need_ctx):
    lam_init = 0.8 - 0.6 * math.exp(-0.3 * layer_idx)
    lf = lambdas.astype(jnp.float32)
    lam = jnp.exp(jnp.sum(lf[0] * lf[1])) - jnp.exp(jnp.sum(lf[2] * lf[3])) + lam_init
    q, k, v = _diff_project(h, w_in, q_g, k_g)
    q, k = _apply_rope(q, cos, sin), _apply_rope(k, cos, sin)
    qc, kc, vc = _diff_project(hc, w_in, q_g, k_g)
    scale = DIFF_HEAD_DIM ** -0.5

    def maps(a):
        return _heads_first(a[..., 0, :]), _heads_first(a[..., 1, :])

    q1, q2 = maps(q)
    k1, k2 = maps(k)
    qc1, qc2 = maps(qc)
    kc1, kc2 = maps(kc)
    v, vc = _heads_first(v), _heads_first(vc)

    def finish(o, t):
        y = _rmsnorm(o.transpose(0, 2, 1, 3), subln_g) * (1.0 - lam_init)
        return (y.reshape(t.shape[0], t.shape[1], -1) @ w_o).astype(t.dtype)

    cat = lambda a, b: jnp.concatenate([a, b], axis=2)
    o = _diff_attend(q1, q2, cat(k1, kc1), cat(k2, kc2), cat(v, vc), lam, scale)
    out = finish(o, h)
    out_c = finish(_diff_attend(qc1, qc2, kc1, kc2, vc, lam, scale), hc) if need_ctx else None
    return out, out_c


def _retention_scan(q, k, v, log_gamma, state0, include_diag):
    b, h, s, dk = q.shape
    dv = v.shape[-1]
    nc = s // RET_CHUNK
    pos = jnp.arange(RET_CHUNK, dtype=jnp.float32)
    dist = pos[:, None] - pos[None, :]
    valid = (dist >= 0) if include_diag else (dist > 0)
    intra = jnp.where(valid, jnp.exp(log_gamma[:, None, None] * jnp.maximum(dist, 0.0)), 0.0)
    q_dec = jnp.exp(log_gamma[:, None] * (pos + 1.0))[:, :, None]
    k_dec = jnp.exp(log_gamma[:, None] * (RET_CHUNK - 1.0 - pos))[:, :, None]
    chunk_dec = jnp.exp(log_gamma * RET_CHUNK)[:, None, None]

    def chunks(a):
        return a.astype(jnp.float32).reshape(b, h, nc, RET_CHUNK, a.shape[-1]).transpose(2, 0, 1, 3, 4)

    def step(state, blk):
        qb, kb, vb = blk
        scores = jnp.einsum('bhid,bhjd->bhij', qb, kb) * intra
        o = jnp.einsum('bhij,bhjv->bhiv', scores, vb) + jnp.einsum('bhid,bhdv->bhiv', qb * q_dec, state)
        state = state * chunk_dec + jnp.einsum('bhjd,bhjv->bhdv', kb * k_dec, vb)
        return state, o

    state, o = lax.scan(step, state0, (chunks(q), chunks(k), chunks(v)))
    return o.transpose(1, 2, 0, 3, 4).reshape(b, h, s, dv), state


def _ret_project(t, w_in):
    b, s, _ = t.shape
    nq = RET_HEADS * RET_DK
    nv = RET_HEADS * RET_DV
    z = t @ w_in
    q = z[..., :nq].reshape(b, s, RET_HEADS, RET_DK)
    k = z[..., nq:2 * nq].reshape(b, s, RET_HEADS, RET_DK) * (RET_DK ** -0.5)
    v = z[..., 2 * nq:2 * nq + nv].reshape(b, s, RET_HEADS, RET_DV)
    g = z[..., 2 * nq + nv:]
    return q, k, v, g


def _retention_mixer(h, hc, w_in, decay_fwd, decay_bwd, norm_g, w_o, cos, sin, need_ctx):
    lg_f = jax.nn.log_sigmoid(decay_fwd.astype(jnp.float32))
    lg_b = jax.nn.log_sigmoid(decay_bwd.astype(jnp.float32))
    q, k, v, g = _ret_project(h, w_in)
    q, k = _apply_rope(q, cos, sin), _apply_rope(k, cos, sin)
    qc, kc, vc, gc = _ret_project(hc, w_in)
    q, k, v, qc, kc, vc = (_heads_first(a) for a in (q, k, v, qc, kc, vc))
    flip = lambda a: jnp.flip(a, axis=2)
    s0 = jnp.zeros((h.shape[0], RET_HEADS, RET_DK, RET_DV), jnp.float32)
    oc_f, sc_f = _retention_scan(qc, kc, vc, lg_f, s0, True)
    oc_b, sc_b = _retention_scan(flip(qc), flip(kc), flip(vc), lg_b, s0, False)
    o_f, _ = _retention_scan(q, k, v, lg_f, sc_f, True)
    o_b, _ = _retention_scan(flip(q), flip(k), flip(v), lg_b, sc_b, False)
    g_heads = norm_g.reshape(RET_HEADS, RET_DV)

    def finish(y, gate, t):
        y = _rmsnorm(y.transpose(0, 2, 1, 3), g_heads).reshape(t.shape[0], t.shape[1], -1)
        return ((jax.nn.silu(gate) * y) @ w_o).astype(t.dtype)

    out = finish(o_f + flip(o_b), g, h)
    out_c = finish(oc_f + flip(oc_b), gc, hc) if need_ctx else None
    return out, out_c


def _moe(t, router_w, router_bias, w_gate, w_up, w_down):
    n = t.shape[0]
    scores = jax.nn.sigmoid(t.astype(jnp.float32) @ router_w.astype(jnp.float32))
    sel = (scores + router_bias).reshape(n, N_GROUPS, EXPERTS_PER_GROUP)
    group_score = lax.top_k(sel, TOP_K)[0].sum(-1)
    grp = jnp.argmax(group_score, axis=-1)
    grp_mask = jax.nn.one_hot(grp, N_GROUPS, dtype=sel.dtype)
    in_group = jnp.einsum('ng,nge->ne', grp_mask, sel)
    _, local = lax.top_k(in_group, TOP_K)
    idx = grp[:, None] * EXPERTS_PER_GROUP + local
    w = jnp.take_along_axis(scores, idx, axis=1)
    w = w / jnp.sum(w, axis=-1, keepdims=True)
    gates = jnp.sum(jax.nn.one_hot(idx, N_EXPERTS, dtype=jnp.float32) * w[..., None], axis=1)
    y = jnp.zeros(t.shape, jnp.float32)
    for e in range(N_EXPERTS):
        a = jax.nn.silu(t @ w_gate[e]) * (t @ w_up[e])
        y = y + gates[:, e:e + 1] * (a @ w_down[e])
    return y.astype(t.dtype)


def setup_inputs(seed: int = 0) -> dict:
    key = jax.random.key(seed)
    keys = iter(jax.random.split(key, 48))
    f32 = jnp.float32

    def normal(shape, scale):
        return scale * jax.random.normal(next(keys), shape, f32)

    def gain(shape):
        return 1.0 + normal(shape, 0.02)

    D = D_MODEL
    nA, nB, nC = N_MLA_LAYERS, N_DIFF_LAYERS, N_RET_LAYERS
    mla_in = MLA_Q_LORA + MLA_KV_LORA + MLA_ROPE
    diff_in = 3 * DIFF_HEADS * 2 * DIFF_HEAD_DIM
    ret_in = 2 * RET_HEADS * RET_DK + 2 * RET_HEADS * RET_DV
    p_min = jnp.power(2.0, -5.0 - jnp.arange(RET_HEADS, dtype=f32))
    decay_logit = jnp.log1p(-p_min) - jnp.log(p_min)
    return {
        'x': normal((BATCH, SEQ, D), 1.0),
        'c': normal((BATCH, D), 1.0),
        'ctx': normal((BATCH, CTX_LEN, D), 1.0),
        'c_ctx': normal((D,), 1.0),
        'ada_w': normal((DEPTH, D, 6 * D), 0.5 * D ** -0.5),
        'ada_b': normal((DEPTH, 6 * D), 0.02),
        'norm_mix_g': gain((DEPTH, D)),
        'norm_ffn_g': gain((DEPTH, D)),
        'mla_w_in': normal((nA, D, mla_in), D ** -0.5),
        'mla_q_a_g': gain((nA, MLA_Q_LORA)),
        'mla_w_q_b': normal((nA, MLA_Q_LORA, MLA_HEADS * (MLA_NOPE + MLA_ROPE)), MLA_Q_LORA ** -0.5),
        'mla_kv_a_g': gain((nA, MLA_KV_LORA)),
        'mla_w_kv_b': normal((nA, MLA_KV_LORA, MLA_HEADS * (MLA_NOPE + MLA_V)), MLA_KV_LORA ** -0.5),
        'mla_q_norm_g': gain((nA, MLA_NOPE + MLA_ROPE)),
        'mla_k_norm_g': gain((nA, MLA_NOPE + MLA_ROPE)),
        'mla_w_o': normal((nA, MLA_HEADS * MLA_V, D), (MLA_HEADS * MLA_V) ** -0.5),
        'diff_w_in': normal((nB, D, diff_in), D ** -0.5),
        'diff_q_norm_g': gain((nB, DIFF_HEAD_DIM)),
        'diff_k_norm_g': gain((nB, DIFF_HEAD_DIM)),
        'diff_lambda': normal((nB, 4, DIFF_HEAD_DIM), 0.1),
        'diff_subln_g': gain((nB, 2 * DIFF_HEAD_DIM)),
        'diff_w_o': normal((nB, DIFF_HEADS * 2 * DIFF_HEAD_DIM, D), (DIFF_HEADS * 2 * DIFF_HEAD_DIM) ** -0.5),
        'ret_w_in': normal((nC, D, ret_in), D ** -0.5),
        'ret_decay_fwd': decay_logit[None, :] + normal((nC, RET_HEADS), 0.05),
        'ret_decay_bwd': decay_logit[None, :] + normal((nC, RET_HEADS), 0.05),
        'ret_norm_g': gain((nC, RET_HEADS * RET_DV)),
        'ret_w_o': normal((nC, RET_HEADS * RET_DV, D), (RET_HEADS * RET_DV) ** -0.5),
        'router_w': normal((D, N_EXPERTS), D ** -0.5),
        'router_bias': normal((N_EXPERTS,), 0.01),
        'moe_w_gate': normal((DEPTH, N_EXPERTS, D, D_EXPERT), D ** -0.5),
        'moe_w_up': normal((DEPTH, N_EXPERTS, D, D_EXPERT), D ** -0.5),
        'moe_w_down': normal((DEPTH, N_EXPERTS, D_EXPERT, D), D_EXPERT ** -0.5),
    }


def reference(x, c, ctx, c_ctx, ada_w, ada_b, norm_mix_g, norm_ffn_g,
              mla_w_in, mla_q_a_g, mla_w_q_b, mla_kv_a_g, mla_w_kv_b, mla_q_norm_g, mla_k_norm_g, mla_w_o,
              diff_w_in, diff_q_norm_g, diff_k_norm_g, diff_lambda, diff_subln_g, diff_w_o,
              ret_w_in, ret_decay_fwd, ret_decay_bwd, ret_norm_g, ret_w_o,
              router_w, router_bias, moe_w_gate, moe_w_up, moe_w_down):
    B, S, D = x.shape
    n_ctx = ctx.shape[1]
    ROWS = S // GRID_W
    cos_mla, sin_mla = _axial_rope_tables(ROWS, MLA_ROPE)
    cos_diff, sin_diff = _axial_rope_tables(ROWS, DIFF_HEAD_DIM)
    cos_ret, sin_ret = _axial_rope_tables(ROWS, RET_DK)
    silu_c = jax.nn.silu(c)
    silu_cc = jax.nn.silu(c_ctx)
    x_lat, x_ctx = x, ctx
    for i in range(DEPTH):
        kind = i % N_MIXERS
        j = i // N_MIXERS
        last = i == DEPTH - 1
        mod = (silu_c @ ada_w[i] + ada_b[i])[:, None, :]
        mod_c = (silu_cc @ ada_w[i] + ada_b[i])[None, None, :]
        sh_m, sc_m, g_m, sh_f, sc_f, g_f = jnp.split(mod, 6, axis=-1)
        csh_m, csc_m, cg_m, csh_f, csc_f, cg_f = jnp.split(mod_c, 6, axis=-1)

        h = _modulate(_rmsnorm(x_lat, norm_mix_g[i]), sh_m, sc_m)
        hc = _modulate(_rmsnorm(x_ctx, norm_mix_g[i]), csh_m, csc_m)
        if kind == 0:
            o, oc = _mla_mixer(h, hc, mla_w_in[j], mla_q_a_g[j], mla_w_q_b[j], mla_kv_a_g[j], mla_w_kv_b[j],
                               mla_q_norm_g[j], mla_k_norm_g[j], mla_w_o[j], cos_mla, sin_mla, not last)
        elif kind == 1:
            o, oc = _diff_mixer(h, hc, diff_w_in[j], diff_q_norm_g[j], diff_k_norm_g[j], diff_lambda[j],
                                diff_subln_g[j], diff_w_o[j], cos_diff, sin_diff, i, not last)
        else:
            o, oc = _retention_mixer(h, hc, ret_w_in[j], ret_decay_fwd[j], ret_decay_bwd[j], ret_norm_g[j],
                                     ret_w_o[j], cos_ret, sin_ret, not last)
        x_lat = x_lat + g_m * o

        if last:
            h = _modulate(_rmsnorm(x_lat, norm_ffn_g[i]), sh_f, sc_f)
            y = _moe(h.reshape(-1, D), router_w, router_bias, moe_w_gate[i], moe_w_up[i], moe_w_down[i])
            x_lat = x_lat + g_f * y.reshape(B, S, D)
        else:
            x_ctx = x_ctx + cg_m * oc
            h = _modulate(_rmsnorm(x_lat, norm_ffn_g[i]), sh_f, sc_f)
            hc = _modulate(_rmsnorm(x_ctx, norm_ffn_g[i]), csh_f, csc_f)
            tokens = jnp.concatenate([h.reshape(-1, D), hc.reshape(-1, D)], axis=0)
            y = _moe(tokens, router_w, router_bias, moe_w_gate[i], moe_w_up[i], moe_w_down[i])
            x_lat = x_lat + g_f * y[:B * S].reshape(B, S, D)
            x_ctx = x_ctx + cg_f * y[B * S:].reshape(B, n_ctx, D)
    return x_lat
```

```python
import functools
import math

import jax
import jax.numpy as jnp
from jax import lax
from jax.experimental import pallas as pl
from jax.experimental.pallas import tpu as pltpu

F32 = jnp.float32
BF16 = jnp.bfloat16

GRID_W = 64
ROPE_BASE = 10000.0
NORM_EPS = 1e-6
N_MIXERS = 3

MLA_HEADS = 16
MLA_Q_LORA = 512
MLA_KV_LORA = 512
MLA_NOPE = 128
MLA_ROPE = 64
MLA_V = 128
MLA_HEAD_PAD = 256

DIFF_HEADS = 8
DIFF_HEAD_DIM = 128

RET_HEADS = 8
RET_DK = 256
RET_DV = 512
RET_CHUNK = 256

N_EXPERTS = 16
N_GROUPS = 4
EXPERTS_PER_GROUP = 4
D_EXPERT = 1024

LANES = 128
SUBLANES = 8
VMEM_LIMIT = 56 * 1024 * 1024

ROW_TILE = 256
MM_ROW_TILE = 512
MOE_ROW_TILE = 256
ADA_COL_TILE = 1024


def _params(sem):
    return pltpu.CompilerParams(dimension_semantics=sem, vmem_limit_bytes=VMEM_LIMIT)


def _rms(x, width):
    return x * lax.rsqrt(jnp.sum(x * x, axis=-1, keepdims=True) / width + NORM_EPS)


def _ada_kernel(ct_ref, w_ref, b_ref, o_ref, sb_ref, *, n_cond):
    d = ct_ref.shape[0]
    tn = o_ref.shape[-1]

    @pl.when((pl.program_id(0) == 0) & (pl.program_id(1) == 0))
    def _():
        ct = ct_ref[...]
        s = ct * jax.nn.sigmoid(ct)
        for r in range(n_cond):
            sb_ref[r] = jnp.broadcast_to(s[:, r:r + 1], (d, LANES))

    def body(kc, accs):
        k0 = pl.multiple_of(kc * SUBLANES, SUBLANES)
        w8 = w_ref[0, pl.ds(k0, SUBLANES), :]
        out = []
        for r in range(n_cond):
            s8 = sb_ref[r, pl.ds(k0, SUBLANES), :]
            out.append(accs[r] + w8 * jnp.concatenate([s8] * (tn // LANES), axis=1))
        return tuple(out)

    accs = lax.fori_loop(0, d // SUBLANES, body,
                         tuple(jnp.zeros((SUBLANES, tn), F32) for _ in range(n_cond)), unroll=4)
    rows = [jnp.sum(a, axis=0, keepdims=True) + b_ref[0] for a in accs]
    rows.append(jnp.zeros((SUBLANES - n_cond, tn), F32))
    o_ref[0] = jnp.concatenate(rows, axis=0)


def _ada_mods(cond_t, ada_w, ada_b, n_cond):
    depth, d, n6 = ada_w.shape
    tn = ADA_COL_TILE
    return pl.pallas_call(
        functools.partial(_ada_kernel, n_cond=n_cond),
        grid=(depth, n6 // tn),
        in_specs=[pl.BlockSpec((d, SUBLANES), lambda l, j: (0, 0)),
                  pl.BlockSpec((1, d, tn), lambda l, j: (l, 0, j)),
                  pl.BlockSpec((1, 1, tn), lambda l, j: (l, 0, j))],
        out_specs=pl.BlockSpec((1, SUBLANES, tn), lambda l, j: (l, 0, j)),
        out_shape=jax.ShapeDtypeStruct((depth, SUBLANES, n6), F32),
        scratch_shapes=[pltpu.VMEM((n_cond, d, LANES), F32)],
        compiler_params=_params(("arbitrary", "arbitrary")),
        name="ada_mods",
    )(cond_t, ada_w, ada_b.reshape(depth, 1, n6))


def _group_spec(d, tm, rows_per_group, n_groups):
    return pl.BlockSpec((1, 1, d), lambda i, *_: (jnp.minimum(i * tm // rows_per_group, n_groups - 1), 0, 0))


def _norm_mod_kernel(x_ref, g_ref, sh_ref, sc_ref, o_ref):
    x = x_ref[...]
    h = _rms(x, x.shape[-1]) * g_ref[...] * (1.0 + sc_ref[0]) + sh_ref[0]
    o_ref[...] = h.astype(o_ref.dtype)


def _norm_mod(x, g, sh, sc, rows_per_group, n_groups):
    n, d = x.shape
    tm = ROW_TILE
    gs = _group_spec(d, tm, rows_per_group, n_groups)
    return pl.pallas_call(
        _norm_mod_kernel,
        grid=(n // tm,),
        in_specs=[pl.BlockSpec((tm, d), lambda i: (i, 0)), pl.BlockSpec((1, d), lambda i: (0, 0)), gs, gs],
        out_specs=pl.BlockSpec((tm, d), lambda i: (i, 0)),
        out_shape=jax.ShapeDtypeStruct((n, d), BF16),
        compiler_params=_params(("parallel",)),
        name="norm_mod",
    )(x, g.reshape(1, d), sh, sc)


def _route_kernel(x_ref, g_ref, sh_ref, sc_ref, rw_ref, rb_ref, h_ref, idx_ref, wt_ref):
    x = x_ref[...]
    tm = x.shape[0]
    h = _rms(x, x.shape[-1]) * g_ref[...] * (1.0 + sc_ref[0]) + sh_ref[0]
    h_ref[...] = h.astype(h_ref.dtype)
    logits = jnp.dot(h, rw_ref[...], precision=lax.Precision.HIGHEST, preferred_element_type=F32)
    scores = jax.nn.sigmoid(logits.T[:N_EXPERTS])
    sel = scores + rb_ref[...]
    sel_r = [sel[e:e + 1, :] for e in range(N_EXPERTS)]
    sc_r = [scores[e:e + 1, :] for e in range(N_EXPERTS)]

    gscore = []
    for g in range(N_GROUPS):
        a, b, c, d = sel_r[4 * g:4 * g + 4]
        hi1, lo1, hi2, lo2 = jnp.maximum(a, b), jnp.minimum(a, b), jnp.maximum(c, d), jnp.minimum(c, d)
        gscore.append(jnp.maximum(hi1, hi2) + jnp.maximum(jnp.minimum(hi1, hi2), jnp.maximum(lo1, lo2)))
    grp = jnp.zeros((1, tm), jnp.int32)
    best = gscore[0]
    for g in range(1, N_GROUPS):
        better = gscore[g] > best
        grp = jnp.where(better, g, grp)
        best = jnp.where(better, gscore[g], best)

    def pick(rows, l):
        out = rows[l]
        for g in range(1, N_GROUPS):
            out = jnp.where(grp == g, rows[4 * g + l], out)
        return out

    v = [pick(sel_r, l) for l in range(EXPERTS_PER_GROUP)]
    s = [pick(sc_r, l) for l in range(EXPERTS_PER_GROUP)]

    def first_max(vals):
        m = jnp.maximum(jnp.maximum(vals[0], vals[1]), jnp.maximum(vals[2], vals[3]))
        l = jnp.where(vals[0] == m, 0, jnp.where(vals[1] == m, 1, jnp.where(vals[2] == m, 2, 3)))
        return l

    def at(vals, l):
        return jnp.where(l == 0, vals[0], jnp.where(l == 1, vals[1], jnp.where(l == 2, vals[2], vals[3])))

    l1 = first_max(v)
    l2 = first_max([jnp.where(l1 == l, -jnp.inf, v[l]) for l in range(EXPERTS_PER_GROUP)])
    s1, s2 = at(s, l1), at(s, l2)
    tot = s1 + s2
    pad_i = jnp.zeros((SUBLANES - 2, tm), jnp.int32)
    pad_f = jnp.zeros((SUBLANES - 2, tm), F32)
    idx_ref[...] = jnp.concatenate([grp * EXPERTS_PER_GROUP + l1, grp * EXPERTS_PER_GROUP + l2, pad_i], axis=0)
    wt_ref[...] = jnp.concatenate([s1 / tot, s2 / tot, pad_f], axis=0)


def _norm_mod_route(x, g, sh, sc, rw_pad, rb_col, rows_per_group, n_groups):
    n, d = x.shape
    tm = ROW_TILE
    gs = _group_spec(d, tm, rows_per_group, n_groups)
    return pl.pallas_call(
        _route_kernel,
        grid=(n // tm,),
        in_specs=[pl.BlockSpec((tm, d), lambda i: (i, 0)), pl.BlockSpec((1, d), lambda i: (0, 0)), gs, gs,
                  pl.BlockSpec((d, LANES), lambda i: (0, 0)), pl.BlockSpec((N_EXPERTS, 1), lambda i: (0, 0))],
        out_specs=[pl.BlockSpec((tm, d), lambda i: (i, 0)),
                   pl.BlockSpec((SUBLANES, tm), lambda i: (0, i)),
                   pl.BlockSpec((SUBLANES, tm), lambda i: (0, i))],
        out_shape=[jax.ShapeDtypeStruct((n, d), BF16),
                   jax.ShapeDtypeStruct((SUBLANES, n), jnp.int32),
                   jax.ShapeDtypeStruct((SUBLANES, n), F32)],
        compiler_params=_params(("parallel",)),
        name="norm_mod_route",
    )(x, g.reshape(1, d), sh, sc, rw_pad, rb_col)


def _half_swap_rope(x, c, s):
    return x * c + pltpu.roll(x, LANES // 2, 1) * s


def _proj_kernel(*refs, mode, scale):
    x_ref, w_ref = refs[0], refs[1]
    o_ref = refs[-1]
    acc = jnp.dot(x_ref[...], w_ref[...], preferred_element_type=F32)
    tn = acc.shape[1]
    if mode == "plain":
        o_ref[...] = acc.astype(o_ref.dtype)
    elif mode == "silu":
        o_ref[...] = (acc * jax.nn.sigmoid(acc)).astype(o_ref.dtype)
    elif mode == "norm_rope":
        g, c, s = refs[2][...], refs[3][...], refs[4][...]
        for j in range(tn // LANES):
            blk = _rms(acc[:, j * LANES:(j + 1) * LANES], LANES) * g
            o_ref[:, j * LANES:(j + 1) * LANES] = (_half_swap_rope(blk, c, s) * scale).astype(o_ref.dtype)
    elif mode == "rope256":
        c, s = refs[2][...], refs[3][...]
        for j in range(tn // (2 * LANES)):
            x1 = acc[:, (2 * j) * LANES:(2 * j + 1) * LANES]
            x2 = acc[:, (2 * j + 1) * LANES:(2 * j + 2) * LANES]
            o_ref[:, (2 * j) * LANES:(2 * j + 1) * LANES] = ((x1 * c - x2 * s) * scale).astype(o_ref.dtype)
            o_ref[:, (2 * j + 1) * LANES:(2 * j + 2) * LANES] = ((x2 * c + x1 * s) * scale).astype(o_ref.dtype)
    else:
        raise ValueError(mode)


def _proj(x, w, col0, n_cols, *, mode="plain", scale=1.0, extras=(), out_dtype=BF16, tn=512):
    n, k = x.shape
    tm = MM_ROW_TILE
    tn = min(tn, n_cols)
    j0 = col0 // tn
    extra_specs = []
    for e in extras:
        if e.shape[0] == 1:
            extra_specs.append(pl.BlockSpec(e.shape, lambda i, j: (0, 0)))
        else:
            extra_specs.append(pl.BlockSpec((tm, e.shape[1]), lambda i, j: (i, 0)))
    return pl.pallas_call(
        functools.partial(_proj_kernel, mode=mode, scale=scale),
        grid=(n // tm, n_cols // tn),
        in_specs=[pl.BlockSpec((tm, k), lambda i, j: (i, 0)),
                  pl.BlockSpec((k, tn), lambda i, j: (0, j0 + j))] + extra_specs,
        out_specs=pl.BlockSpec((tm, tn), lambda i, j: (i, j)),
        out_shape=jax.ShapeDtypeStruct((n, n_cols), out_dtype),
        compiler_params=_params(("parallel", "arbitrary")),
        name="proj_" + mode,
    )(x, w, *extras)


def _out_res_kernel(y_ref, w_ref, x_ref, g_ref, o_ref):
    acc = jnp.dot(y_ref[...], w_ref[...], preferred_element_type=F32)
    o_ref[...] = x_ref[...] + g_ref[0] * acc


def _out_res(y, w, x, gate, rows_per_group, n_groups, tn=512):
    n, k = y.shape
    d = w.shape[1]
    tm = MM_ROW_TILE
    return pl.pallas_call(
        _out_res_kernel,
        grid=(n // tm, d // tn),
        in_specs=[pl.BlockSpec((tm, k), lambda i, j: (i, 0)),
                  pl.BlockSpec((k, tn), lambda i, j: (0, j)),
                  pl.BlockSpec((tm, tn), lambda i, j: (i, j)),
                  pl.BlockSpec((1, 1, tn), lambda i, j: (jnp.minimum(i * tm // rows_per_group, n_groups - 1), 0, j))],
        out_specs=pl.BlockSpec((tm, tn), lambda i, j: (i, j)),
        out_shape=jax.ShapeDtypeStruct((n, d), F32),
        compiler_params=_params(("parallel", "arbitrary")),
        name="out_res",
    )(y, w, x, gate)


def _mla_q_kernel(z_ref, ga_ref, w_ref, g_ref, c_ref, s_ref, o_ref, *, scale):
    cq = z_ref[...]
    cn = (_rms(cq, cq.shape[-1]) * ga_ref[...]).astype(BF16)
    acc = jnp.dot(cn, w_ref[...], preferred_element_type=F32)
    g, c, s = g_ref[...], c_ref[...], s_ref[...]
    width = float(MLA_NOPE + MLA_ROPE)
    for j in range(acc.shape[1] // MLA_HEAD_PAD):
        qh = acc[:, j * MLA_HEAD_PAD:(j + 1) * MLA_HEAD_PAD]
        qn = _rms(qh, width) * g
        o_ref[:, j * MLA_HEAD_PAD:j * MLA_HEAD_PAD + LANES] = (qn[:, :LANES] * scale).astype(o_ref.dtype)
        o_ref[:, j * MLA_HEAD_PAD + LANES:(j + 1) * MLA_HEAD_PAD] = (
            _half_swap_rope(qn[:, LANES:], c, s) * scale).astype(o_ref.dtype)


def _mla_q(z, q_a_g, w_q_b_pad, q_g_pad, cos_t, sin_t, scale, tn=1024):
    n = z.shape[0]
    tm = MM_ROW_TILE
    n_out = w_q_b_pad.shape[1]
    return pl.pallas_call(
        functools.partial(_mla_q_kernel, scale=scale),
        grid=(n // tm, n_out // tn),
        in_specs=[pl.BlockSpec((tm, MLA_Q_LORA), lambda i, j: (i, 0)),
                  pl.BlockSpec((1, MLA_Q_LORA), lambda i, j: (0, 0)),
                  pl.BlockSpec((MLA_Q_LORA, tn), lambda i, j: (0, j)),
                  pl.BlockSpec((1, MLA_HEAD_PAD), lambda i, j: (0, 0)),
                  pl.BlockSpec((tm, LANES), lambda i, j: (i, 0)),
                  pl.BlockSpec((tm, LANES), lambda i, j: (i, 0))],
        out_specs=pl.BlockSpec((tm, tn), lambda i, j: (i, j)),
        out_shape=jax.ShapeDtypeStruct((n, n_out), BF16),
        compiler_params=_params(("parallel", "arbitrary")),
        name="mla_q",
    )(z, q_a_g.reshape(1, -1), w_q_b_pad, q_g_pad, cos_t, sin_t)


def _mla_kv_kernel(z_ref, kr_ref, ga_ref, w_ref, g_ref, c_ref, s_ref, k_ref, v_ref):
    ckv = z_ref[...]
    cn = (_rms(ckv, ckv.shape[-1]) * ga_ref[...]).astype(BF16)
    acc = jnp.dot(cn, w_ref[...], preferred_element_type=F32)
    kr = kr_ref[...]
    kr_ss = jnp.sum(kr * kr, axis=-1, keepdims=True)
    g, c, s = g_ref[...], c_ref[...], s_ref[...]
    width = float(MLA_NOPE + MLA_ROPE)
    for j in range(acc.shape[1] // MLA_HEAD_PAD):
        kn = acc[:, j * MLA_HEAD_PAD:j * MLA_HEAD_PAD + MLA_NOPE]
        vv = acc[:, j * MLA_HEAD_PAD + MLA_NOPE:(j + 1) * MLA_HEAD_PAD]
        f = lax.rsqrt((jnp.sum(kn * kn, axis=-1, keepdims=True) + kr_ss) / width + NORM_EPS)
        k_ref[:, j * MLA_HEAD_PAD:j * MLA_HEAD_PAD + LANES] = (kn * f * g[:, :LANES]).astype(k_ref.dtype)
        k_ref[:, j * MLA_HEAD_PAD + LANES:(j + 1) * MLA_HEAD_PAD] = _half_swap_rope(
            kr * f * g[:, LANES:], c, s).astype(k_ref.dtype)
        v_ref[:, j * MLA_V:(j + 1) * MLA_V] = vv.astype(v_ref.dtype)


def _mla_kv(z, kv_a_g, w_kv_b, k_g_pad, cos_t, sin_t, tn=1024):
    n = z.shape[0]
    tm = MM_ROW_TILE
    n_out = w_kv_b.shape[1]
    kr_block = (MLA_Q_LORA + MLA_KV_LORA) // LANES
    return pl.pallas_call(
        _mla_kv_kernel,
        grid=(n // tm, n_out // tn),
        in_specs=[pl.BlockSpec((tm, MLA_KV_LORA), lambda i, j: (i, 1)),
                  pl.BlockSpec((tm, LANES), lambda i, j: (i, kr_block)),
                  pl.BlockSpec((1, MLA_KV_LORA), lambda i, j: (0, 0)),
                  pl.BlockSpec((MLA_KV_LORA, tn), lambda i, j: (0, j)),
                  pl.BlockSpec((1, MLA_HEAD_PAD), lambda i, j: (0, 0)),
                  pl.BlockSpec((tm, LANES), lambda i, j: (i, 0)),
                  pl.BlockSpec((tm, LANES), lambda i, j: (i, 0))],
        out_specs=[pl.BlockSpec((tm, tn), lambda i, j: (i, j)),
                   pl.BlockSpec((tm, tn // 2), lambda i, j: (i, j))],
        out_shape=[jax.ShapeDtypeStruct((n, n_out), BF16),
                   jax.ShapeDtypeStruct((n, n_out // 2), BF16)],
        compiler_params=_params(("parallel", "arbitrary")),
        name="mla_kv",
    )(z, z, kv_a_g.reshape(1, -1), w_kv_b, k_g_pad, cos_t, sin_t)


def _softmax_pv(q, k_refs, v_refs, c0, dq):
    ss = [lax.dot_general(q, k_ref[:, c0:c0 + dq], (((1,), (1,)), ((), ())), preferred_element_type=F32)
          for k_ref in k_refs]
    mx = functools.reduce(jnp.maximum, [jnp.max(s, axis=-1, keepdims=True) for s in ss])
    ps = [jnp.exp(s - mx) for s in ss]
    den = functools.reduce(lambda a, b: a + b, [jnp.sum(p, axis=-1, keepdims=True) for p in ps])
    num = functools.reduce(lambda a, b: a + b, [jnp.dot(p.astype(BF16), v_ref[...], preferred_element_type=F32)
                                               for p, v_ref in zip(ps, v_refs)])
    return num / den


def _attn_kernel(*refs, n_seg, dq):
    q_ref = refs[0]
    k_refs = refs[1:1 + n_seg]
    v_refs = refs[1 + n_seg:1 + 2 * n_seg]
    o_ref = refs[-1]
    o_ref[...] = _softmax_pv(q_ref[...], k_refs, v_refs, 0, dq).astype(o_ref.dtype)


def _diff_attn_kernel(*refs, n_seg, dq, lam_init):
    q_ref = refs[0]
    k_refs = refs[1:1 + n_seg]
    v_refs = refs[1 + n_seg:1 + 2 * n_seg]
    lam_ref, g_ref, o_ref = refs[-3], refs[-2], refs[-1]
    lf = lam_ref[...]
    lam = (jnp.exp(jnp.sum(lf[0:1] * lf[1:2], axis=-1, keepdims=True))
           - jnp.exp(jnp.sum(lf[2:3] * lf[3:4], axis=-1, keepdims=True)) + lam_init)
    q = q_ref[...]
    o1 = _softmax_pv(q[:, :dq], k_refs, v_refs, 0, dq)
    o2 = _softmax_pv(q[:, dq:], k_refs, v_refs, dq, dq)
    o = o1 - lam * o2
    o_ref[...] = (_rms(o, o.shape[-1]) * g_ref[...] * (1.0 - lam_init)).astype(o_ref.dtype)


def _attention(q, k, v, *, heads, q_width, dv, seq, n_ctx, batch, lat_queries, diff=None):
    lat_rows = batch * seq
    ctx_blk0 = lat_rows // n_ctx
    if lat_queries:
        tq = ROW_TILE
        nq = seq // tq
        q_map = lambda b, h, i: (b * nq + i, h)
        segs = [(seq, lambda b, h, i: (b, h)), (n_ctx, lambda b, h, i: (ctx_blk0 + b, h))]
        out_rows = lat_rows
    else:
        tq = n_ctx
        nq = 1
        q_map = lambda b, h, i: (ctx_blk0 + b, h)
        segs = [(n_ctx, lambda b, h, i: (ctx_blk0 + b, h))]
        out_rows = batch * n_ctx
    n_seg = len(segs)
    in_specs = [pl.BlockSpec((tq, q_width), q_map)]
    in_specs += [pl.BlockSpec((rows, q_width), m) for rows, m in segs]
    in_specs += [pl.BlockSpec((rows, dv), m) for rows, m in segs]
    args = [q] + [k] * n_seg + [v] * n_seg
    if diff is None:
        kern = functools.partial(_attn_kernel, n_seg=n_seg, dq=q_width)
    else:
        lam, subln_g, lam_init = diff
        kern = functools.partial(_diff_attn_kernel, n_seg=n_seg, dq=q_width // 2, lam_init=lam_init)
        in_specs += [pl.BlockSpec(lam.shape, lambda b, h, i: (0, 0)), pl.BlockSpec((1, dv), lambda b, h, i: (0, 0))]
        args += [lam, subln_g.reshape(1, dv)]
    return pl.pallas_call(
        kern,
        grid=(batch, heads, nq),
        in_specs=in_specs,
        out_specs=pl.BlockSpec((tq, dv), lambda b, h, i: (b * nq + i, h)),
        out_shape=jax.ShapeDtypeStruct((out_rows, heads * dv), BF16),
        compiler_params=_params(("parallel", "parallel", "arbitrary")),
        name="attention" if diff is None else "diff_attention",
    )(*args)


def _ret_kernel(dec_ref, q_ref, k_ref, v_ref, o_ref, state_ref, *, backward):
    @pl.when(pl.program_id(2) == 0)
    def _():
        state_ref[...] = jnp.zeros_like(state_ref)

    ch = q_ref.shape[0]
    lg = jax.nn.log_sigmoid(dec_ref[0])[:, :1]
    ii = lax.broadcasted_iota(jnp.int32, (ch, ch), 0).astype(F32)
    jj = lax.broadcasted_iota(jnp.int32, (ch, ch), 1).astype(F32)
    pos = lax.broadcasted_iota(jnp.int32, (ch, 1), 0).astype(F32)
    if backward:
        dist, valid = jj - ii, jj > ii
        q_dec, k_dec = jnp.exp(lg * (ch - pos)), jnp.exp(lg * pos)
    else:
        dist, valid = ii - jj, ii >= jj
        q_dec, k_dec = jnp.exp(lg * (pos + 1.0)), jnp.exp(lg * (ch - 1.0 - pos))
    intra = jnp.where(valid, jnp.exp(lg * jnp.maximum(dist, 0.0)), 0.0)
    q, k, v = q_ref[...], k_ref[...], v_ref[...]
    scores = lax.dot_general(q, k, (((1,), (1,)), ((), ())), preferred_element_type=F32) * intra
    state = state_ref[...]
    o = jnp.dot(scores.astype(BF16), v, preferred_element_type=F32)
    o += jnp.dot((q.astype(F32) * q_dec).astype(BF16), state.astype(BF16), preferred_element_type=F32)
    o_ref[...] = o.astype(o_ref.dtype)
    kd_t = (k.astype(F32) * k_dec).T.astype(BF16)
    state_ref[...] = state * jnp.exp(lg * ch) + jnp.dot(kd_t, v, preferred_element_type=F32)


def _retention(dec, q, k, v, *, backward, seq, n_ctx, batch):
    ch = RET_CHUNK
    n_c, n_s = n_ctx // ch, seq // ch
    ctx_blk0 = batch * seq // ch

    def row_block(b, t):
        if backward:
            return jnp.where(t < n_c, ctx_blk0 + b * n_c + (n_c - 1 - t), b * n_s + (n_s - 1 - (t - n_c)))
        return jnp.where(t < n_c, ctx_blk0 + b * n_c + t, b * n_s + (t - n_c))

    spec = lambda width: pl.BlockSpec((ch, width), lambda b, h, t: (row_block(b, t), h))
    return pl.pallas_call(
        functools.partial(_ret_kernel, backward=backward),
        grid=(batch, RET_HEADS, n_c + n_s),
        in_specs=[pl.BlockSpec((1, 1, LANES), lambda b, h, t: (h, 0, 0)), spec(RET_DK), spec(RET_DK), spec(RET_DV)],
        out_specs=spec(RET_DV),
        out_shape=jax.ShapeDtypeStruct((q.shape[0], RET_HEADS * RET_DV), BF16),
        scratch_shapes=[pltpu.VMEM((RET_DK, RET_DV), F32)],
        compiler_params=_params(("parallel", "parallel", "arbitrary")),
        name="retention_bwd" if backward else "retention_fwd",
    )(dec, q, k, v)


def _ret_finish_kernel(of_ref, ob_ref, gate_ref, g_ref, o_ref):
    for h in range(RET_HEADS):
        cols = slice(h * RET_DV, (h + 1) * RET_DV)
        y = of_ref[:, cols].astype(F32) + ob_ref[:, cols].astype(F32)
        o_ref[:, cols] = (gate_ref[:, cols].astype(F32) * (_rms(y, RET_DV) * g_ref[:, cols])).astype(o_ref.dtype)


def _ret_finish(o_f, o_b, gate, norm_g):
    n, w = o_f.shape
    tm = ROW_TILE
    row = pl.BlockSpec((tm, w), lambda i: (i, 0))
    return pl.pallas_call(
        _ret_finish_kernel,
        grid=(n // tm,),
        in_specs=[row, row, row, pl.BlockSpec((1, w), lambda i: (0, 0))],
        out_specs=row,
        out_shape=jax.ShapeDtypeStruct((n, w), BF16),
        compiler_params=_params(("parallel",)),
        name="ret_finish",
    )(o_f, o_b, gate, norm_g.reshape(1, w))


def _moe_kernel(te_ref, nu_ref, x_ref, wr_ref, wg_ref, wu_ref, wd_ref, o_ref):
    t = pl.program_id(0)

    @pl.when(t < nu_ref[0])
    def _():
        x = x_ref[...]
        a = jnp.dot(x, wg_ref[0], preferred_element_type=F32)
        u = jnp.dot(x, wu_ref[0], preferred_element_type=F32)
        act = (a * jax.nn.sigmoid(a) * u).astype(BF16)
        o_ref[...] = (jnp.dot(act, wd_ref[0], preferred_element_type=F32) * wr_ref[...]).astype(o_ref.dtype)

    @pl.when(t >= nu_ref[0])
    def _():
        o_ref[...] = jnp.zeros_like(o_ref)


def _moe_ffn(tile_expert, n_used, x_sorted, w_row, wg, wu, wd):
    r, d = x_sorted.shape
    f = wg.shape[2]
    tm = MOE_ROW_TILE
    return pl.pallas_call(
        _moe_kernel,
        grid_spec=pltpu.PrefetchScalarGridSpec(
            num_scalar_prefetch=2,
            grid=(r // tm,),
            in_specs=[pl.BlockSpec((tm, d), lambda t, te, nu: (t, 0)),
                      pl.BlockSpec((tm, 1), lambda t, te, nu: (t, 0)),
                      pl.BlockSpec((1, d, f), lambda t, te, nu: (te[t], 0, 0)),
                      pl.BlockSpec((1, d, f), lambda t, te, nu: (te[t], 0, 0)),
                      pl.BlockSpec((1, f, d), lambda t, te, nu: (te[t], 0, 0))],
            out_specs=pl.BlockSpec((tm, d), lambda t, te, nu: (t, 0)),
        ),
        out_shape=jax.ShapeDtypeStruct((r, d), F32),
        compiler_params=_params(("arbitrary",)),
        name="moe_ffn",
    )(tile_expert, n_used, x_sorted, w_row, wg, wu, wd)


def _combine_kernel(x_ref, a_ref, b_ref, g_ref, o_ref):
    o_ref[...] = x_ref[...] + g_ref[0] * (a_ref[...] + b_ref[...])


def _combine(x, a, b, gate, rows_per_group, n_groups):
    n, d = x.shape
    tm = ROW_TILE
    row = pl.BlockSpec((tm, d), lambda i: (i, 0))
    return pl.pallas_call(
        _combine_kernel,
        grid=(n // tm,),
        in_specs=[row, row, row, _group_spec(d, tm, rows_per_group, n_groups)],
        out_specs=row,
        out_shape=jax.ShapeDtypeStruct((n, d), F32),
        compiler_params=_params(("parallel",)),
        name="moe_combine",
    )(x, a, b, gate)


def _moe_layer(x, g, sh, sc, gate, rw_pad, rb_col, wg, wu, wd, rows_per_group, n_groups):
    n, d = x.shape
    tm = MOE_ROW_TILE
    h, idx8, wt8 = _norm_mod_route(x, g, sh, sc, rw_pad, rb_col, rows_per_group, n_groups)
    e_flat = idx8[:2].reshape(-1)
    w_flat = wt8[:2].reshape(-1)
    onehot = (e_flat[:, None] == jnp.arange(N_EXPERTS, dtype=jnp.int32)[None, :]).astype(jnp.int32)
    csum = jnp.cumsum(onehot, axis=0)
    rank = jnp.sum((csum - onehot) * onehot, axis=1)
    counts = csum[-1]
    padded = (counts + tm - 1) // tm * tm
    ends = jnp.cumsum(padded)
    dest = (ends - padded)[e_flat] + rank
    r = (2 * n + N_EXPERTS * (tm - 1)) // tm * tm
    tok = jnp.tile(jnp.arange(n, dtype=jnp.int32), 2)
    src = jnp.zeros((r,), jnp.int32).at[dest].set(tok)
    w_row = jnp.zeros((r,), F32).at[dest].set(w_flat)
    tile_expert = jnp.minimum(
        jnp.searchsorted(ends, jnp.arange(r // tm, dtype=jnp.int32) * tm, side="right"), N_EXPERTS - 1
    ).astype(jnp.int32)
    n_used = (ends[-1:] // tm).astype(jnp.int32)
    x_sorted = jnp.take(h, src, axis=0)
    y_sorted = _moe_ffn(tile_expert, n_used, x_sorted, w_row.reshape(r, 1), wg, wu, wd)
    a = jnp.take(y_sorted, dest[:n], axis=0)
    b = jnp.take(y_sorted, dest[n:], axis=0)
    return _combine(x, a, b, gate, rows_per_group, n_groups)


def _axial_tables(rows, rot_dim):
    n_freq = rot_dim // 4
    inv_freq = jnp.power(ROPE_BASE, -jnp.arange(n_freq, dtype=F32) / n_freq)
    row = jnp.repeat(jnp.arange(rows, dtype=F32), GRID_W)
    col = jnp.tile(jnp.arange(GRID_W, dtype=F32), rows)
    ang = jnp.concatenate([row[:, None] * inv_freq, col[:, None] * inv_freq], axis=-1)
    return jnp.cos(ang), jnp.sin(ang)


def _flat_tables(c_lat, s_lat, batch, n_ctx_rows):
    c = jnp.concatenate([jnp.tile(c_lat, (batch, 1)), jnp.broadcast_to(c_lat[:1], (n_ctx_rows, LANES))])
    s = jnp.concatenate([jnp.tile(s_lat, (batch, 1)), jnp.zeros((n_ctx_rows, LANES), F32)])
    return c, s


def _spread_pairs(a, axis):
    x1, x2 = jnp.split(a, 2, axis=axis)
    z = jnp.zeros_like(x1)
    return jnp.concatenate([x1, z, x2, z], axis=axis)


def kernel(x, c, ctx, c_ctx, ada_w, ada_b, norm_mix_g, norm_ffn_g, mla_w_in, mla_q_a_g, mla_w_q_b, mla_kv_a_g,
           mla_w_kv_b, mla_q_norm_g, mla_k_norm_g, mla_w_o, diff_w_in, diff_q_norm_g, diff_k_norm_g, diff_lambda,
           diff_subln_g, diff_w_o, ret_w_in, ret_decay_fwd, ret_decay_bwd, ret_norm_g, ret_w_o, router_w,
           router_bias, moe_w_gate, moe_w_up, moe_w_down):
    batch, seq, d = x.shape
    n_ctx = ctx.shape[1]
    depth = ada_w.shape[0]
    n_lat, n_cx = batch * seq, batch * n_ctx
    n_groups = batch + 1
    assert n_groups <= SUBLANES and seq % MM_ROW_TILE == 0 and n_cx % MM_ROW_TILE == 0
    assert seq % RET_CHUNK == 0 and n_ctx % RET_CHUNK == 0 and n_lat % n_ctx == 0 and n_ctx == ROW_TILE

    xa = jnp.concatenate([x.reshape(n_lat, d), ctx.reshape(n_cx, d)], axis=0)

    cond = jnp.concatenate([c, c_ctx[None, :], jnp.zeros((SUBLANES - n_groups, d), F32)], axis=0)
    mods = _ada_mods(cond.T, ada_w, ada_b, n_groups)

    rows = seq // GRID_W
    cos64, sin64 = _axial_tables(rows, MLA_ROPE)
    zeros32 = jnp.zeros_like(cos64)
    mla_c, mla_s = _flat_tables(jnp.concatenate([cos64, zeros32, cos64, zeros32], axis=1),
                                jnp.concatenate([-sin64, zeros32, sin64, zeros32], axis=1), batch, n_cx)
    cos128, sin128 = _axial_tables(rows, DIFF_HEAD_DIM)
    diff_c, diff_s = _flat_tables(jnp.concatenate([cos128, cos128], axis=1),
                                  jnp.concatenate([-sin128, sin128], axis=1), batch, n_cx)
    cos256, sin256 = _axial_tables(rows, RET_DK)
    ret_c, ret_s = _flat_tables(cos256, sin256, batch, n_cx)

    rw_pad = jnp.concatenate([router_w, jnp.zeros((d, LANES - N_EXPERTS), F32)], axis=1)
    rb_col = router_bias.reshape(N_EXPERTS, 1)

    for i in range(depth):
        kind, j, last = i % N_MIXERS, i // N_MIXERS, i == depth - 1
        m = mods[i].reshape(SUBLANES, 6, 1, d)
        sh_m, sc_m, g_m, sh_f, sc_f, g_f = (m[:, t] for t in range(6))
        n_all = xa.shape[0]
        h = _norm_mod(xa, norm_mix_g[i], sh_m, sc_m, seq, n_groups)
        q_rows = n_lat if last else n_all

        if kind == 0:
            w_in = jnp.concatenate([mla_w_in[j][:, :MLA_Q_LORA + MLA_KV_LORA],
                                    _spread_pairs(mla_w_in[j][:, MLA_Q_LORA + MLA_KV_LORA:], 1)], axis=1).astype(BF16)
            z = _proj(h, w_in, 0, w_in.shape[1], out_dtype=F32, tn=w_in.shape[1])
            wq = mla_w_q_b[j].reshape(MLA_Q_LORA, MLA_HEADS, MLA_NOPE + MLA_ROPE)
            wq = jnp.concatenate([wq[..., :MLA_NOPE], _spread_pairs(wq[..., MLA_NOPE:], 2)], axis=2)
            wq = wq.reshape(MLA_Q_LORA, MLA_HEADS * MLA_HEAD_PAD).astype(BF16)
            spread_g = lambda g: jnp.concatenate([g[:MLA_NOPE], _spread_pairs(g[MLA_NOPE:], 0)]).reshape(1, -1)
            scale = float(MLA_NOPE + MLA_ROPE) ** -0.5
            q = _mla_q(z[:q_rows], mla_q_a_g[j], wq, spread_g(mla_q_norm_g[j]), mla_c[:q_rows], mla_s[:q_rows], scale)
            k, v = _mla_kv(z, mla_kv_a_g[j], mla_w_kv_b[j].astype(BF16), spread_g(mla_k_norm_g[j]), mla_c, mla_s)
            attn = functools.partial(_attention, q, k, v, heads=MLA_HEADS, q_width=MLA_HEAD_PAD, dv=MLA_V,
                                     seq=seq, n_ctx=n_ctx, batch=batch)
            w_o = mla_w_o[j].astype(BF16)
        elif kind == 1:
            w_in = diff_w_in[j].astype(BF16)
            hd = DIFF_HEADS * 2 * DIFF_HEAD_DIM
            scale = float(DIFF_HEAD_DIM) ** -0.5
            q = _proj(h[:q_rows], w_in, 0, hd, mode="norm_rope", scale=scale,
                      extras=(diff_q_norm_g[j].reshape(1, -1), diff_c[:q_rows], diff_s[:q_rows]))
            k = _proj(h, w_in, hd, hd, mode="norm_rope", extras=(diff_k_norm_g[j].reshape(1, -1), diff_c, diff_s))
            v = _proj(h, w_in, 2 * hd, hd)
            lam_init = 0.8 - 0.6 * math.exp(-0.3 * i)
            attn = functools.partial(_attention, q, k, v, heads=DIFF_HEADS, q_width=2 * DIFF_HEAD_DIM,
                                     dv=2 * DIFF_HEAD_DIM, seq=seq, n_ctx=n_ctx, batch=batch,
                                     diff=(diff_lambda[j], diff_subln_g[j], lam_init))
            w_o = diff_w_o[j].astype(BF16)
        else:
            w_in = ret_w_in[j].astype(BF16)
            nq, nv = RET_HEADS * RET_DK, RET_HEADS * RET_DV
            q = _proj(h, w_in, 0, nq, mode="rope256", extras=(ret_c, ret_s))
            k = _proj(h, w_in, nq, nq, mode="rope256", scale=float(RET_DK) ** -0.5, extras=(ret_c, ret_s))
            v = _proj(h, w_in, 2 * nq, nv)
            gate = _proj(h, w_in, 2 * nq + nv, nv, mode="silu")
            dec = lambda p: jnp.broadcast_to(p.astype(F32)[:, None, None], (RET_HEADS, 1, LANES))
            o_f = _retention(dec(ret_decay_fwd[j]), q, k, v, backward=False, seq=seq, n_ctx=n_ctx, batch=batch)
            o_b = _retention(dec(ret_decay_bwd[j]), q, k, v, backward=True, seq=seq, n_ctx=n_ctx, batch=batch)
            y = _ret_finish(o_f, o_b, gate, ret_norm_g[j])
            attn = None
            w_o = ret_w_o[j].astype(BF16)

        if attn is not None:
            y = attn(lat_queries=True)
            if not last:
                y = jnp.concatenate([y, attn(lat_queries=False)], axis=0)
        if last:
            xa = xa[:n_lat]
            y = y[:n_lat]
        xa = _out_res(y, w_o, xa, g_m, seq, n_groups)
        xa = _moe_layer(xa, norm_ffn_g[i], sh_f, sc_f, g_f, rw_pad, rb_col, moe_w_gate[i].astype(BF16),
                        moe_w_up[i].astype(BF16), moe_w_down[i].astype(BF16), seq, n_groups)
    return xa[:n_lat].reshape(batch, seq, d)
```

```python
import functools
import math

import jax
import jax.numpy as jnp
from jax import lax
from jax.experimental import pallas as pl
from jax.experimental.pallas import tpu as pltpu

F32 = jnp.float32
BF16 = jnp.bfloat16

GRID_W = 64
ROPE_BASE = 10000.0
NORM_EPS = 1e-6
N_MIXERS = 3

MLA_HEADS = 16
MLA_Q_LORA = 512
MLA_KV_LORA = 512
MLA_NOPE = 128
MLA_ROPE = 64
MLA_V = 128
MLA_HEAD_PAD = 256

DIFF_HEADS = 8
DIFF_HEAD_DIM = 128

RET_HEADS = 8
RET_DK = 256
RET_DV = 512
RET_CHUNK = 256

N_EXPERTS = 16
N_GROUPS = 4
EXPERTS_PER_GROUP = 4
D_EXPERT = 1024

LANES = 128
SUBLANES = 8
VMEM_LIMIT = 56 * 1024 * 1024

ROW_TILE = 256
MM_ROW_TILE = 512
MOE_ROW_TILE = 256
KEY_CHUNK = 1024
LOG2E = 1.4426950408889634
ADA_COL_TILE = 1024


def _params(sem):
    return pltpu.CompilerParams(dimension_semantics=sem, vmem_limit_bytes=VMEM_LIMIT)


def _rms(x, width):
    return x * lax.rsqrt(jnp.sum(x * x, axis=-1, keepdims=True) / width + NORM_EPS)


def _ada_kernel(ct_ref, w_ref, b_ref, o_ref, sb_ref, *, n_cond):
    d = ct_ref.shape[0]
    tn = o_ref.shape[-1]

    @pl.when((pl.program_id(0) == 0) & (pl.program_id(1) == 0))
    def _():
        ct = ct_ref[...]
        s = ct * jax.nn.sigmoid(ct)
        for r in range(n_cond):
            sb_ref[r] = jnp.broadcast_to(s[:, r:r + 1], (d, LANES))

    def body(kc, accs):
        k0 = pl.multiple_of(kc * SUBLANES, SUBLANES)
        w8 = w_ref[0, pl.ds(k0, SUBLANES), :]
        out = []
        for r in range(n_cond):
            s8 = sb_ref[r, pl.ds(k0, SUBLANES), :]
            out.append(accs[r] + w8 * jnp.concatenate([s8] * (tn // LANES), axis=1))
        return tuple(out)

    accs = lax.fori_loop(0, d // SUBLANES, body,
                         tuple(jnp.zeros((SUBLANES, tn), F32) for _ in range(n_cond)), unroll=4)
    rows = [jnp.sum(a, axis=0, keepdims=True) + b_ref[0] for a in accs]
    rows.append(jnp.zeros((SUBLANES - n_cond, tn), F32))
    o_ref[0] = jnp.concatenate(rows, axis=0)


def _ada_mods(cond_t, ada_w, ada_b, n_cond):
    depth, d, n6 = ada_w.shape
    tn = ADA_COL_TILE
    return pl.pallas_call(
        functools.partial(_ada_kernel, n_cond=n_cond),
        grid=(depth, n6 // tn),
        in_specs=[pl.BlockSpec((d, SUBLANES), lambda l, j: (0, 0)),
                  pl.BlockSpec((1, d, tn), lambda l, j: (l, 0, j)),
                  pl.BlockSpec((1, 1, tn), lambda l, j: (l, 0, j))],
        out_specs=pl.BlockSpec((1, SUBLANES, tn), lambda l, j: (l, 0, j)),
        out_shape=jax.ShapeDtypeStruct((depth, SUBLANES, n6), F32),
        scratch_shapes=[pltpu.VMEM((n_cond, d, LANES), F32)],
        compiler_params=_params(("arbitrary", "arbitrary")),
        name="ada_mods",
    )(cond_t, ada_w, ada_b.reshape(depth, 1, n6))


def _group_spec(d, tm, rows_per_group, n_groups):
    return pl.BlockSpec((1, 1, d), lambda i, *_: (jnp.minimum(i * tm // rows_per_group, n_groups - 1), 0, 0))


def _norm_mod_kernel(x_ref, g_ref, sh_ref, sc_ref, o_ref):
    x = x_ref[...]
    h = _rms(x, x.shape[-1]) * g_ref[...] * (1.0 + sc_ref[0]) + sh_ref[0]
    o_ref[...] = h.astype(o_ref.dtype)


def _norm_mod(x, g, sh, sc, rows_per_group, n_groups):
    n, d = x.shape
    tm = ROW_TILE
    gs = _group_spec(d, tm, rows_per_group, n_groups)
    return pl.pallas_call(
        _norm_mod_kernel,
        grid=(n // tm,),
        in_specs=[pl.BlockSpec((tm, d), lambda i: (i, 0)), pl.BlockSpec((1, d), lambda i: (0, 0)), gs, gs],
        out_specs=pl.BlockSpec((tm, d), lambda i: (i, 0)),
        out_shape=jax.ShapeDtypeStruct((n, d), BF16),
        compiler_params=_params(("parallel",)),
        name="norm_mod",
    )(x, g.reshape(1, d), sh, sc)


def _route_kernel(x_ref, g_ref, sh_ref, sc_ref, rw_ref, rb_ref, h_ref, idx_ref, wt_ref, cnt_ref, carry_ref):
    @pl.when(pl.program_id(0) == 0)
    def _():
        carry_ref[...] = jnp.zeros_like(carry_ref)

    x = x_ref[...]
    tm = x.shape[0]
    h = _rms(x, x.shape[-1]) * g_ref[...] * (1.0 + sc_ref[0]) + sh_ref[0]
    h_ref[...] = h.astype(h_ref.dtype)
    logits = jnp.dot(h, rw_ref[...], precision=lax.Precision.HIGHEST, preferred_element_type=F32)
    scores = jax.nn.sigmoid(logits.T[:N_EXPERTS])
    sel = scores + rb_ref[...]
    sel_r = [sel[e:e + 1, :] for e in range(N_EXPERTS)]
    sc_r = [scores[e:e + 1, :] for e in range(N_EXPERTS)]

    gscore = []
    for g in range(N_GROUPS):
        a, b, c, d = sel_r[4 * g:4 * g + 4]
        hi1, lo1, hi2, lo2 = jnp.maximum(a, b), jnp.minimum(a, b), jnp.maximum(c, d), jnp.minimum(c, d)
        gscore.append(jnp.maximum(hi1, hi2) + jnp.maximum(jnp.minimum(hi1, hi2), jnp.maximum(lo1, lo2)))
    grp = jnp.zeros((1, tm), jnp.int32)
    best = gscore[0]
    for g in range(1, N_GROUPS):
        better = gscore[g] > best
        grp = jnp.where(better, g, grp)
        best = jnp.where(better, gscore[g], best)

    def pick(rows, l):
        out = rows[l]
        for g in range(1, N_GROUPS):
            out = jnp.where(grp == g, rows[4 * g + l], out)
        return out

    v = [pick(sel_r, l) for l in range(EXPERTS_PER_GROUP)]
    s = [pick(sc_r, l) for l in range(EXPERTS_PER_GROUP)]

    def first_max(vals):
        m = jnp.maximum(jnp.maximum(vals[0], vals[1]), jnp.maximum(vals[2], vals[3]))
        l = jnp.where(vals[0] == m, 0, jnp.where(vals[1] == m, 1, jnp.where(vals[2] == m, 2, 3)))
        return l

    def at(vals, l):
        return jnp.where(l == 0, vals[0], jnp.where(l == 1, vals[1], jnp.where(l == 2, vals[2], vals[3])))

    l1 = first_max(v)
    l2 = first_max([jnp.where(l1 == l, -jnp.inf, v[l]) for l in range(EXPERTS_PER_GROUP)])
    s1, s2 = at(s, l1), at(s, l2)
    tot = s1 + s2
    e1, e2 = grp * EXPERTS_PER_GROUP + l1, grp * EXPERTS_PER_GROUP + l2

    eids = lax.broadcasted_iota(jnp.int32, (N_EXPERTS, tm), 0)
    oh1, oh2 = (eids == e1).astype(F32), (eids == e2).astype(F32)
    before = (lax.broadcasted_iota(jnp.int32, (tm, tm), 0) < lax.broadcasted_iota(jnp.int32, (tm, tm), 1)).astype(BF16)
    carry = carry_ref[:, :1]
    tot1 = jnp.sum(oh1, axis=1, keepdims=True)
    pre1 = carry + jnp.dot(oh1.astype(BF16), before, preferred_element_type=F32)
    pre2 = carry + tot1 + jnp.dot(oh2.astype(BF16), before, preferred_element_type=F32)
    r1 = jnp.sum(oh1 * pre1, axis=0, keepdims=True).astype(jnp.int32)
    r2 = jnp.sum(oh2 * pre2, axis=0, keepdims=True).astype(jnp.int32)
    carry_new = carry + tot1 + jnp.sum(oh2, axis=1, keepdims=True)
    carry_ref[...] = jnp.broadcast_to(carry_new, carry_ref.shape)
    cnt_ref[...] = jnp.broadcast_to(carry_new, cnt_ref.shape).astype(jnp.int32)

    idx_ref[...] = jnp.concatenate([e1, e2, r1, r2, jnp.zeros((SUBLANES - 4, tm), jnp.int32)], axis=0)
    wt_ref[...] = jnp.concatenate([s1 / tot, s2 / tot, jnp.zeros((LANES - 2, tm), F32)], axis=0).T


def _norm_mod_route(x, g, sh, sc, rw_pad, rb_col, rows_per_group, n_groups):
    n, d = x.shape
    tm = ROW_TILE
    gs = _group_spec(d, tm, rows_per_group, n_groups)
    return pl.pallas_call(
        _route_kernel,
        grid=(n // tm,),
        in_specs=[pl.BlockSpec((tm, d), lambda i: (i, 0)), pl.BlockSpec((1, d), lambda i: (0, 0)), gs, gs,
                  pl.BlockSpec((d, LANES), lambda i: (0, 0)), pl.BlockSpec((N_EXPERTS, 1), lambda i: (0, 0))],
        out_specs=[pl.BlockSpec((tm, d), lambda i: (i, 0)),
                   pl.BlockSpec((SUBLANES, tm), lambda i: (0, i)),
                   pl.BlockSpec((tm, LANES), lambda i: (i, 0)),
                   pl.BlockSpec((N_EXPERTS, LANES), lambda i: (0, 0))],
        out_shape=[jax.ShapeDtypeStruct((n, d), F32),
                   jax.ShapeDtypeStruct((SUBLANES, n), jnp.int32),
                   jax.ShapeDtypeStruct((n, LANES), F32),
                   jax.ShapeDtypeStruct((N_EXPERTS, LANES), jnp.int32)],
        scratch_shapes=[pltpu.VMEM((N_EXPERTS, LANES), F32)],
        compiler_params=_params(("arbitrary",)),
        name="norm_mod_route",
    )(x, g.reshape(1, d), sh, sc, rw_pad, rb_col)


def _half_swap_rope(x, c, s):
    return x * c + pltpu.roll(x, LANES // 2, 1) * s


def _proj_kernel(*refs, mode, scale):
    x_ref, w_ref = refs[0], refs[1]
    o_ref = refs[-1]
    acc = jnp.dot(x_ref[...], w_ref[...], preferred_element_type=F32)
    tn = acc.shape[1]
    if mode == "plain":
        o_ref[...] = acc.astype(o_ref.dtype)
    elif mode == "silu":
        o_ref[...] = (acc * jax.nn.sigmoid(acc)).astype(o_ref.dtype)
    elif mode == "norm_rope":
        g, c, s = refs[2][...], refs[3][...], refs[4][...]
        for j in range(tn // LANES):
            blk = _rms(acc[:, j * LANES:(j + 1) * LANES], LANES) * g
            o_ref[:, j * LANES:(j + 1) * LANES] = (_half_swap_rope(blk, c, s) * scale).astype(o_ref.dtype)
    elif mode == "rope256":
        c, s = refs[2][...], refs[3][...]
        for j in range(tn // (2 * LANES)):
            x1 = acc[:, (2 * j) * LANES:(2 * j + 1) * LANES]
            x2 = acc[:, (2 * j + 1) * LANES:(2 * j + 2) * LANES]
            o_ref[:, (2 * j) * LANES:(2 * j + 1) * LANES] = ((x1 * c - x2 * s) * scale).astype(o_ref.dtype)
            o_ref[:, (2 * j + 1) * LANES:(2 * j + 2) * LANES] = ((x2 * c + x1 * s) * scale).astype(o_ref.dtype)
    else:
        raise ValueError(mode)


def _proj(x, w, col0, n_cols, *, mode="plain", scale=1.0, extras=(), out_dtype=BF16, tn=512):
    n, k = x.shape
    tm = MM_ROW_TILE
    tn = min(tn, n_cols)
    j0 = col0 // tn
    extra_specs = []
    for e in extras:
        if e.shape[0] == 1:
            extra_specs.append(pl.BlockSpec(e.shape, lambda i, j: (0, 0)))
        else:
            extra_specs.append(pl.BlockSpec((tm, e.shape[1]), lambda i, j: (i, 0)))
    return pl.pallas_call(
        functools.partial(_proj_kernel, mode=mode, scale=scale),
        grid=(n // tm, n_cols // tn),
        in_specs=[pl.BlockSpec((tm, k), lambda i, j: (i, 0)),
                  pl.BlockSpec((k, tn), lambda i, j: (0, j0 + j))] + extra_specs,
        out_specs=pl.BlockSpec((tm, tn), lambda i, j: (i, j)),
        out_shape=jax.ShapeDtypeStruct((n, n_cols), out_dtype),
        compiler_params=_params(("parallel", "arbitrary")),
        name="proj_" + mode,
    )(x, w, *extras)


def _out_res_kernel(y_ref, w_ref, x_ref, g_ref, o_ref):
    acc = jnp.dot(y_ref[...], w_ref[...], preferred_element_type=F32)
    o_ref[...] = x_ref[...] + g_ref[0] * acc


def _out_res(y, w, x, gate, rows_per_group, n_groups, tn=512):
    n, k = y.shape
    d = w.shape[1]
    tm = MM_ROW_TILE
    return pl.pallas_call(
        _out_res_kernel,
        grid=(n // tm, d // tn),
        in_specs=[pl.BlockSpec((tm, k), lambda i, j: (i, 0)),
                  pl.BlockSpec((k, tn), lambda i, j: (0, j)),
                  pl.BlockSpec((tm, tn), lambda i, j: (i, j)),
                  pl.BlockSpec((1, 1, tn), lambda i, j: (jnp.minimum(i * tm // rows_per_group, n_groups - 1), 0, j))],
        out_specs=pl.BlockSpec((tm, tn), lambda i, j: (i, j)),
        out_shape=jax.ShapeDtypeStruct((n, d), F32),
        compiler_params=_params(("parallel", "arbitrary")),
        name="out_res",
    )(y, w, x, gate)


def _mla_q_kernel(z_ref, ga_ref, w_ref, g_ref, c_ref, s_ref, o_ref, *, scale):
    cq = z_ref[...]
    cn = (_rms(cq, cq.shape[-1]) * ga_ref[...]).astype(BF16)
    acc = jnp.dot(cn, w_ref[...], preferred_element_type=F32)
    g, c, s = g_ref[...], c_ref[...], s_ref[...]
    width = float(MLA_NOPE + MLA_ROPE)
    for j in range(acc.shape[1] // MLA_HEAD_PAD):
        qh = acc[:, j * MLA_HEAD_PAD:(j + 1) * MLA_HEAD_PAD]
        qn = _rms(qh, width) * g
        o_ref[:, j * MLA_HEAD_PAD:j * MLA_HEAD_PAD + LANES] = (qn[:, :LANES] * scale).astype(o_ref.dtype)
        o_ref[:, j * MLA_HEAD_PAD + LANES:(j + 1) * MLA_HEAD_PAD] = (
            _half_swap_rope(qn[:, LANES:], c, s) * scale).astype(o_ref.dtype)


def _mla_q(z, q_a_g, w_q_b_pad, q_g_pad, cos_t, sin_t, scale, tn=1024):
    n = z.shape[0]
    tm = MM_ROW_TILE
    n_out = w_q_b_pad.shape[1]
    return pl.pallas_call(
        functools.partial(_mla_q_kernel, scale=scale),
        grid=(n // tm, n_out // tn),
        in_specs=[pl.BlockSpec((tm, MLA_Q_LORA), lambda i, j: (i, 0)),
                  pl.BlockSpec((1, MLA_Q_LORA), lambda i, j: (0, 0)),
                  pl.BlockSpec((MLA_Q_LORA, tn), lambda i, j: (0, j)),
                  pl.BlockSpec((1, MLA_HEAD_PAD), lambda i, j: (0, 0)),
                  pl.BlockSpec((tm, LANES), lambda i, j: (i, 0)),
                  pl.BlockSpec((tm, LANES), lambda i, j: (i, 0))],
        out_specs=pl.BlockSpec((tm, tn), lambda i, j: (i, j)),
        out_shape=jax.ShapeDtypeStruct((n, n_out), BF16),
        compiler_params=_params(("parallel", "arbitrary")),
        name="mla_q",
    )(z, q_a_g.reshape(1, -1), w_q_b_pad, q_g_pad, cos_t, sin_t)


def _mla_kv_kernel(z_ref, kr_ref, ga_ref, w_ref, g_ref, c_ref, s_ref, k_ref, v_ref):
    ckv = z_ref[...]
    cn = (_rms(ckv, ckv.shape[-1]) * ga_ref[...]).astype(BF16)
    acc = jnp.dot(cn, w_ref[...], preferred_element_type=F32)
    kr = kr_ref[...]
    kr_ss = jnp.sum(kr * kr, axis=-1, keepdims=True)
    g, c, s = g_ref[...], c_ref[...], s_ref[...]
    width = float(MLA_NOPE + MLA_ROPE)
    for j in range(acc.shape[1] // MLA_HEAD_PAD):
        kn = acc[:, j * MLA_HEAD_PAD:j * MLA_HEAD_PAD + MLA_NOPE]
        vv = acc[:, j * MLA_HEAD_PAD + MLA_NOPE:(j + 1) * MLA_HEAD_PAD]
        f = lax.rsqrt((jnp.sum(kn * kn, axis=-1, keepdims=True) + kr_ss) / width + NORM_EPS)
        k_ref[:, j * MLA_HEAD_PAD:j * MLA_HEAD_PAD + LANES] = (kn * f * g[:, :LANES]).astype(k_ref.dtype)
        k_ref[:, j * MLA_HEAD_PAD + LANES:(j + 1) * MLA_HEAD_PAD] = _half_swap_rope(
            kr * f * g[:, LANES:], c, s).astype(k_ref.dtype)
        v_ref[:, j * MLA_HEAD_PAD:j * MLA_HEAD_PAD + MLA_V] = vv.astype(v_ref.dtype)
        v_ref[:, j * MLA_HEAD_PAD + MLA_V:(j + 1) * MLA_HEAD_PAD] = jnp.ones((vv.shape[0], LANES), v_ref.dtype)


def _mla_kv(z, kv_a_g, w_kv_b, k_g_pad, cos_t, sin_t, tn=1024):
    n = z.shape[0]
    tm = MM_ROW_TILE
    n_out = w_kv_b.shape[1]
    kr_block = (MLA_Q_LORA + MLA_KV_LORA) // LANES
    return pl.pallas_call(
        _mla_kv_kernel,
        grid=(n // tm, n_out // tn),
        in_specs=[pl.BlockSpec((tm, MLA_KV_LORA), lambda i, j: (i, 1)),
                  pl.BlockSpec((tm, LANES), lambda i, j: (i, kr_block)),
                  pl.BlockSpec((1, MLA_KV_LORA), lambda i, j: (0, 0)),
                  pl.BlockSpec((MLA_KV_LORA, tn), lambda i, j: (0, j)),
                  pl.BlockSpec((1, MLA_HEAD_PAD), lambda i, j: (0, 0)),
                  pl.BlockSpec((tm, LANES), lambda i, j: (i, 0)),
                  pl.BlockSpec((tm, LANES), lambda i, j: (i, 0))],
        out_specs=[pl.BlockSpec((tm, tn), lambda i, j: (i, j)),
                   pl.BlockSpec((tm, tn), lambda i, j: (i, j))],
        out_shape=[jax.ShapeDtypeStruct((n, n_out), BF16),
                   jax.ShapeDtypeStruct((n, n_out), BF16)],
        compiler_params=_params(("parallel", "arbitrary")),
        name="mla_kv",
    )(z, z, kv_a_g.reshape(1, -1), w_kv_b, k_g_pad, cos_t, sin_t)


def _softmax_pv(q, k_refs, v_refs, c0, dq, *, den_from_v):
    add = lambda a, b: a + b
    chunks = []
    for k_ref, v_ref in zip(k_refs, v_refs):
        step = min(k_ref.shape[0], KEY_CHUNK)
        chunks += [(k_ref, v_ref, r0, step) for r0 in range(0, k_ref.shape[0], step)]
    ss = [lax.dot_general(q, k_ref[r0:r0 + step, c0:c0 + dq], (((1,), (1,)), ((), ())), preferred_element_type=F32)
          for k_ref, _, r0, step in chunks]
    mx = functools.reduce(jnp.maximum, [jnp.max(s, axis=-1, keepdims=True) for s in ss])
    if den_from_v:
        ps = [jnp.exp2((s - mx).astype(BF16)) for s in ss]
    else:
        ps = [jnp.exp2(s - mx) for s in ss]
    acc = functools.reduce(add, [jnp.dot(p.astype(BF16), v_ref[r0:r0 + step, :], preferred_element_type=F32)
                                 for p, (_, v_ref, r0, step) in zip(ps, chunks)])
    if den_from_v:
        dv = acc.shape[1] - LANES
        return acc[:, :dv] / acc[:, dv:dv + 1]
    return acc / functools.reduce(add, [jnp.sum(p, axis=-1, keepdims=True) for p in ps])


def _attn_kernel(*refs, n_seg, dq):
    q_ref = refs[0]
    k_refs = refs[1:1 + n_seg]
    v_refs = refs[1 + n_seg:1 + 2 * n_seg]
    o_ref = refs[-1]
    o_ref[...] = _softmax_pv(q_ref[...], k_refs, v_refs, 0, dq, den_from_v=True).astype(o_ref.dtype)


def _diff_attn_kernel(*refs, n_seg, dq, lam_init):
    q_ref = refs[0]
    k_refs = refs[1:1 + n_seg]
    v_refs = refs[1 + n_seg:1 + 2 * n_seg]
    lam_ref, g_ref, o_ref = refs[-3], refs[-2], refs[-1]
    lf = lam_ref[...]
    lam = (jnp.exp(jnp.sum(lf[0:1] * lf[1:2], axis=-1, keepdims=True))
           - jnp.exp(jnp.sum(lf[2:3] * lf[3:4], axis=-1, keepdims=True)) + lam_init)
    q = q_ref[...]
    o1 = _softmax_pv(q[:, :dq], k_refs, v_refs, 0, dq, den_from_v=False)
    o2 = _softmax_pv(q[:, dq:], k_refs, v_refs, dq, dq, den_from_v=False)
    o = o1 - lam * o2
    o_ref[...] = (_rms(o, o.shape[-1]) * g_ref[...] * (1.0 - lam_init)).astype(o_ref.dtype)


def _attention(q, k, v, *, heads, q_width, v_width, dv, seq, n_ctx, batch, lat_queries, diff=None):
    lat_rows = batch * seq
    ctx_blk0 = lat_rows // n_ctx
    if lat_queries:
        tq = ROW_TILE
        nq = seq // tq
        q_map = lambda b, h, i: (b * nq + i, h)
        segs = [(seq, lambda b, h, i: (b, h)), (n_ctx, lambda b, h, i: (ctx_blk0 + b, h))]
        out_rows = lat_rows
    else:
        tq = n_ctx
        nq = 1
        q_map = lambda b, h, i: (ctx_blk0 + b, h)
        segs = [(n_ctx, lambda b, h, i: (ctx_blk0 + b, h))]
        out_rows = batch * n_ctx
    n_seg = len(segs)
    in_specs = [pl.BlockSpec((tq, q_width), q_map)]
    in_specs += [pl.BlockSpec((rows, q_width), m) for rows, m in segs]
    in_specs += [pl.BlockSpec((rows, v_width), m) for rows, m in segs]
    args = [q] + [k] * n_seg + [v] * n_seg
    if diff is None:
        kern = functools.partial(_attn_kernel, n_seg=n_seg, dq=q_width)
    else:
        lam, subln_g, lam_init = diff
        kern = functools.partial(_diff_attn_kernel, n_seg=n_seg, dq=q_width // 2, lam_init=lam_init)
        in_specs += [pl.BlockSpec(lam.shape, lambda b, h, i: (0, 0)), pl.BlockSpec((1, dv), lambda b, h, i: (0, 0))]
        args += [lam, subln_g.reshape(1, dv)]
    return pl.pallas_call(
        kern,
        grid=(batch, heads, nq),
        in_specs=in_specs,
        out_specs=pl.BlockSpec((tq, dv), lambda b, h, i: (b * nq + i, h)),
        out_shape=jax.ShapeDtypeStruct((out_rows, heads * dv), BF16),
        compiler_params=_params(("parallel", "parallel", "arbitrary")),
        name="attention" if diff is None else "diff_attention",
    )(*args)


def _ret_kernel(dec_ref, q_ref, k_ref, v_ref, o_ref, state_ref, *, backward):
    @pl.when(pl.program_id(2) == 0)
    def _():
        state_ref[...] = jnp.zeros_like(state_ref)

    ch = q_ref.shape[0]
    lg = jax.nn.log_sigmoid(dec_ref[0])[:, :1]
    ii = lax.broadcasted_iota(jnp.int32, (ch, ch), 0).astype(F32)
    jj = lax.broadcasted_iota(jnp.int32, (ch, ch), 1).astype(F32)
    pos = lax.broadcasted_iota(jnp.int32, (ch, 1), 0).astype(F32)
    if backward:
        dist, valid = jj - ii, jj > ii
        q_dec, k_dec = jnp.exp(lg * (ch - pos)), jnp.exp(lg * pos)
    else:
        dist, valid = ii - jj, ii >= jj
        q_dec, k_dec = jnp.exp(lg * (pos + 1.0)), jnp.exp(lg * (ch - 1.0 - pos))
    intra = jnp.where(valid, jnp.exp(lg * jnp.maximum(dist, 0.0)), 0.0)
    q, k, v = q_ref[...], k_ref[...], v_ref[...]
    scores = lax.dot_general(q, k, (((1,), (1,)), ((), ())), preferred_element_type=F32) * intra
    state = state_ref[...]
    o = jnp.dot(scores.astype(BF16), v, preferred_element_type=F32)
    o += jnp.dot((q.astype(F32) * q_dec).astype(BF16), state.astype(BF16), preferred_element_type=F32)
    o_ref[...] = o.astype(o_ref.dtype)
    kd_t = (k.astype(F32) * k_dec).T.astype(BF16)
    state_ref[...] = state * jnp.exp(lg * ch) + jnp.dot(kd_t, v, preferred_element_type=F32)


def _retention(dec, q, k, v, *, backward, seq, n_ctx, batch):
    ch = RET_CHUNK
    n_c, n_s = n_ctx // ch, seq // ch
    ctx_blk0 = batch * seq // ch

    def row_block(b, t):
        if backward:
            return jnp.where(t < n_c, ctx_blk0 + b * n_c + (n_c - 1 - t), b * n_s + (n_s - 1 - (t - n_c)))
        return jnp.where(t < n_c, ctx_blk0 + b * n_c + t, b * n_s + (t - n_c))

    spec = lambda width: pl.BlockSpec((ch, width), lambda b, h, t: (row_block(b, t), h))
    return pl.pallas_call(
        functools.partial(_ret_kernel, backward=backward),
        grid=(batch, RET_HEADS, n_c + n_s),
        in_specs=[pl.BlockSpec((1, 1, LANES), lambda b, h, t: (h, 0, 0)), spec(RET_DK), spec(RET_DK), spec(RET_DV)],
        out_specs=spec(RET_DV),
        out_shape=jax.ShapeDtypeStruct((q.shape[0], RET_HEADS * RET_DV), BF16),
        scratch_shapes=[pltpu.VMEM((RET_DK, RET_DV), F32)],
        compiler_params=_params(("parallel", "parallel", "arbitrary")),
        name="retention_bwd" if backward else "retention_fwd",
    )(dec, q, k, v)


def _ret_finish_kernel(of_ref, ob_ref, gate_ref, g_ref, o_ref):
    for h in range(RET_HEADS):
        cols = slice(h * RET_DV, (h + 1) * RET_DV)
        y = of_ref[:, cols].astype(F32) + ob_ref[:, cols].astype(F32)
        o_ref[:, cols] = (gate_ref[:, cols].astype(F32) * (_rms(y, RET_DV) * g_ref[:, cols])).astype(o_ref.dtype)


def _ret_finish(o_f, o_b, gate, norm_g):
    n, w = o_f.shape
    tm = ROW_TILE
    row = pl.BlockSpec((tm, w), lambda i: (i, 0))
    return pl.pallas_call(
        _ret_finish_kernel,
        grid=(n // tm,),
        in_specs=[row, row, row, pl.BlockSpec((1, w), lambda i: (0, 0))],
        out_specs=row,
        out_shape=jax.ShapeDtypeStruct((n, w), BF16),
        compiler_params=_params(("parallel",)),
        name="ret_finish",
    )(o_f, o_b, gate, norm_g.reshape(1, w))


def _moe_kernel(te_ref, nu_ref, x_ref, wg_ref, wu_ref, wd_ref, o_ref):
    t = pl.program_id(0)

    @pl.when(t < nu_ref[0])
    def _():
        x = x_ref[...].astype(BF16)
        a = jnp.dot(x, wg_ref[0], preferred_element_type=F32)
        u = jnp.dot(x, wu_ref[0], preferred_element_type=F32)
        act = (a * jax.nn.sigmoid(a) * u).astype(BF16)
        o_ref[...] = jnp.dot(act, wd_ref[0], preferred_element_type=F32).astype(o_ref.dtype)

    @pl.when(t >= nu_ref[0])
    def _():
        o_ref[...] = jnp.zeros_like(o_ref)


def _moe_ffn(tile_expert, n_used, x_sorted, wg, wu, wd):
    r, d = x_sorted.shape
    f = wg.shape[2]
    tm = MOE_ROW_TILE
    return pl.pallas_call(
        _moe_kernel,
        grid_spec=pltpu.PrefetchScalarGridSpec(
            num_scalar_prefetch=2,
            grid=(r // tm,),
            in_specs=[pl.BlockSpec((tm, d), lambda t, te, nu: (t, 0)),
                      pl.BlockSpec((1, d, f), lambda t, te, nu: (te[t], 0, 0)),
                      pl.BlockSpec((1, d, f), lambda t, te, nu: (te[t], 0, 0)),
                      pl.BlockSpec((1, f, d), lambda t, te, nu: (te[t], 0, 0))],
            out_specs=pl.BlockSpec((tm, d), lambda t, te, nu: (t, 0)),
        ),
        out_shape=jax.ShapeDtypeStruct((r, d), F32),
        compiler_params=_params(("arbitrary",)),
        name="moe_ffn",
    )(tile_expert, n_used, x_sorted, wg, wu, wd)


def _combine_kernel(x_ref, a_ref, b_ref, w_ref, g_ref, o_ref):
    w = w_ref[...]
    o_ref[...] = x_ref[...] + g_ref[0] * (w[:, 0:1] * a_ref[...] + w[:, 1:2] * b_ref[...])


def _combine(x, a, b, w_col, gate, rows_per_group, n_groups):
    n, d = x.shape
    tm = ROW_TILE
    row = pl.BlockSpec((tm, d), lambda i: (i, 0))
    return pl.pallas_call(
        _combine_kernel,
        grid=(n // tm,),
        in_specs=[row, row, row, pl.BlockSpec((tm, LANES), lambda i: (i, 0)),
                  _group_spec(d, tm, rows_per_group, n_groups)],
        out_specs=row,
        out_shape=jax.ShapeDtypeStruct((n, d), F32),
        compiler_params=_params(("parallel",)),
        name="moe_combine",
    )(x, a, b, w_col, gate)


def _moe_layer(x, g, sh, sc, gate, rw_pad, rb_col, wg, wu, wd, rows_per_group, n_groups):
    n, d = x.shape
    tm = MOE_ROW_TILE
    h, idx8, w_col, cnt = _norm_mod_route(x, g, sh, sc, rw_pad, rb_col, rows_per_group, n_groups)
    experts = jnp.arange(N_EXPERTS, dtype=jnp.int32)
    padded = (cnt[:, 0] + tm - 1) // tm * tm
    ends = jnp.cumsum(padded)
    starts = ends - padded
    e2, rank2 = idx8[0:2], idx8[2:4]
    dest = rank2 + jnp.sum(jnp.where(e2[..., None] == experts, starts, 0), axis=-1)
    r = (2 * n + N_EXPERTS * (tm - 1)) // tm * tm
    tok = jnp.tile(jnp.arange(n, dtype=jnp.int32), 2)
    src = jnp.zeros((r,), jnp.int32).at[dest.reshape(-1)].set(tok, unique_indices=True, mode="promise_in_bounds")
    tile_start = jnp.arange(r // tm, dtype=jnp.int32) * tm
    tile_expert = jnp.minimum(jnp.sum((ends[None, :] <= tile_start[:, None]).astype(jnp.int32), axis=1),
                              N_EXPERTS - 1)
    n_used = ends[-1:] // tm
    x_sorted = h.at[src].get(mode="promise_in_bounds")
    y_sorted = _moe_ffn(tile_expert, n_used, x_sorted, wg, wu, wd)
    a = y_sorted.at[dest[0]].get(mode="promise_in_bounds")
    b = y_sorted.at[dest[1]].get(mode="promise_in_bounds")
    return _combine(x, a, b, w_col, gate, rows_per_group, n_groups)


def _axial_tables(rows, rot_dim):
    n_freq = rot_dim // 4
    inv_freq = jnp.power(ROPE_BASE, -jnp.arange(n_freq, dtype=F32) / n_freq)
    row = jnp.repeat(jnp.arange(rows, dtype=F32), GRID_W)
    col = jnp.tile(jnp.arange(GRID_W, dtype=F32), rows)
    ang = jnp.concatenate([row[:, None] * inv_freq, col[:, None] * inv_freq], axis=-1)
    return jnp.cos(ang), jnp.sin(ang)


def _flat_tables(c_lat, s_lat, batch, n_ctx_rows):
    c = jnp.concatenate([jnp.tile(c_lat, (batch, 1)), jnp.broadcast_to(c_lat[:1], (n_ctx_rows, LANES))])
    s = jnp.concatenate([jnp.tile(s_lat, (batch, 1)), jnp.zeros((n_ctx_rows, LANES), F32)])
    return c, s


def _spread_pairs(a, axis):
    x1, x2 = jnp.split(a, 2, axis=axis)
    z = jnp.zeros_like(x1)
    return jnp.concatenate([x1, z, x2, z], axis=axis)


def kernel(x, c, ctx, c_ctx, ada_w, ada_b, norm_mix_g, norm_ffn_g, mla_w_in, mla_q_a_g, mla_w_q_b, mla_kv_a_g,
           mla_w_kv_b, mla_q_norm_g, mla_k_norm_g, mla_w_o, diff_w_in, diff_q_norm_g, diff_k_norm_g, diff_lambda,
           diff_subln_g, diff_w_o, ret_w_in, ret_decay_fwd, ret_decay_bwd, ret_norm_g, ret_w_o, router_w,
           router_bias, moe_w_gate, moe_w_up, moe_w_down):
    batch, seq, d = x.shape
    n_ctx = ctx.shape[1]
    depth = ada_w.shape[0]
    n_lat, n_cx = batch * seq, batch * n_ctx
    n_groups = batch + 1
    assert n_groups <= SUBLANES and seq % MM_ROW_TILE == 0 and n_cx % MM_ROW_TILE == 0
    assert seq % RET_CHUNK == 0 and n_ctx % RET_CHUNK == 0 and n_lat % n_ctx == 0 and n_ctx == ROW_TILE

    xa = jnp.concatenate([x.reshape(n_lat, d), ctx.reshape(n_cx, d)], axis=0)

    cond = jnp.concatenate([c, c_ctx[None, :], jnp.zeros((SUBLANES - n_groups, d), F32)], axis=0)
    mods = _ada_mods(cond.T, ada_w, ada_b, n_groups)

    rows = seq // GRID_W
    cos64, sin64 = _axial_tables(rows, MLA_ROPE)
    zeros32 = jnp.zeros_like(cos64)
    mla_c, mla_s = _flat_tables(jnp.concatenate([cos64, zeros32, cos64, zeros32], axis=1),
                                jnp.concatenate([-sin64, zeros32, sin64, zeros32], axis=1), batch, n_cx)
    cos128, sin128 = _axial_tables(rows, DIFF_HEAD_DIM)
    diff_c, diff_s = _flat_tables(jnp.concatenate([cos128, cos128], axis=1),
                                  jnp.concatenate([-sin128, sin128], axis=1), batch, n_cx)
    cos256, sin256 = _axial_tables(rows, RET_DK)
    ret_c, ret_s = _flat_tables(cos256, sin256, batch, n_cx)

    rw_pad = jnp.concatenate([router_w, jnp.zeros((d, LANES - N_EXPERTS), F32)], axis=1)
    rb_col = router_bias.reshape(N_EXPERTS, 1)

    for i in range(depth):
        kind, j, last = i % N_MIXERS, i // N_MIXERS, i == depth - 1
        m = mods[i].reshape(SUBLANES, 6, 1, d)
        sh_m, sc_m, g_m, sh_f, sc_f, g_f = (m[:, t] for t in range(6))
        n_all = xa.shape[0]
        h = _norm_mod(xa, norm_mix_g[i], sh_m, sc_m, seq, n_groups)
        q_rows = n_lat if last else n_all

        if kind == 0:
            w_in = jnp.concatenate([mla_w_in[j][:, :MLA_Q_LORA + MLA_KV_LORA],
                                    _spread_pairs(mla_w_in[j][:, MLA_Q_LORA + MLA_KV_LORA:], 1)], axis=1).astype(BF16)
            z = _proj(h, w_in, 0, w_in.shape[1], out_dtype=F32, tn=w_in.shape[1])
            wq = mla_w_q_b[j].reshape(MLA_Q_LORA, MLA_HEADS, MLA_NOPE + MLA_ROPE)
            wq = jnp.concatenate([wq[..., :MLA_NOPE], _spread_pairs(wq[..., MLA_NOPE:], 2)], axis=2)
            wq = wq.reshape(MLA_Q_LORA, MLA_HEADS * MLA_HEAD_PAD).astype(BF16)
            spread_g = lambda g: jnp.concatenate([g[:MLA_NOPE], _spread_pairs(g[MLA_NOPE:], 0)]).reshape(1, -1)
            scale = float(MLA_NOPE + MLA_ROPE) ** -0.5 * LOG2E
            q = _mla_q(z[:q_rows], mla_q_a_g[j], wq, spread_g(mla_q_norm_g[j]), mla_c[:q_rows], mla_s[:q_rows], scale)
            k, v = _mla_kv(z, mla_kv_a_g[j], mla_w_kv_b[j].astype(BF16), spread_g(mla_k_norm_g[j]), mla_c, mla_s)
            attn = functools.partial(_attention, q, k, v, heads=MLA_HEADS, q_width=MLA_HEAD_PAD,
                                     v_width=MLA_V + LANES, dv=MLA_V, seq=seq, n_ctx=n_ctx, batch=batch)
            w_o = mla_w_o[j].astype(BF16)
        elif kind == 1:
            w_in = diff_w_in[j].astype(BF16)
            hd = DIFF_HEADS * 2 * DIFF_HEAD_DIM
            scale = float(DIFF_HEAD_DIM) ** -0.5 * LOG2E
            q = _proj(h[:q_rows], w_in, 0, hd, mode="norm_rope", scale=scale,
                      extras=(diff_q_norm_g[j].reshape(1, -1), diff_c[:q_rows], diff_s[:q_rows]))
            k = _proj(h, w_in, hd, hd, mode="norm_rope", extras=(diff_k_norm_g[j].reshape(1, -1), diff_c, diff_s))
            v = _proj(h, w_in, 2 * hd, hd)
            lam_init = 0.8 - 0.6 * math.exp(-0.3 * i)
            attn = functools.partial(_attention, q, k, v, heads=DIFF_HEADS, q_width=2 * DIFF_HEAD_DIM,
                                     v_width=2 * DIFF_HEAD_DIM, dv=2 * DIFF_HEAD_DIM, seq=seq, n_ctx=n_ctx,
                                     batch=batch,
                                     diff=(diff_lambda[j], diff_subln_g[j], lam_init))
            w_o = diff_w_o[j].astype(BF16)
        else:
            w_in = ret_w_in[j].astype(BF16)
            nq, nv = RET_HEADS * RET_DK, RET_HEADS * RET_DV
            q = _proj(h, w_in, 0, nq, mode="rope256", extras=(ret_c, ret_s))
            k = _proj(h, w_in, nq, nq, mode="rope256", scale=float(RET_DK) ** -0.5, extras=(ret_c, ret_s))
            v = _proj(h, w_in, 2 * nq, nv)
            gate = _proj(h, w_in, 2 * nq + nv, nv, mode="silu")
            dec = lambda p: jnp.broadcast_to(p.astype(F32)[:, None, None], (RET_HEADS, 1, LANES))
            o_f = _retention(dec(ret_decay_fwd[j]), q, k, v, backward=False, seq=seq, n_ctx=n_ctx, batch=batch)
            o_b = _retention(dec(ret_decay_bwd[j]), q, k, v, backward=True, seq=seq, n_ctx=n_ctx, batch=batch)
            y = _ret_finish(o_f, o_b, gate, ret_norm_g[j])
            attn = None
            w_o = ret_w_o[j].astype(BF16)

        if attn is not None:
            y = attn(lat_queries=True)
            if not last:
                y = jnp.concatenate([y, attn(lat_queries=False)], axis=0)
        if last:
            xa = xa[:n_lat]
            y = y[:n_lat]
        xa = _out_res(y, w_o, xa, g_m, seq, n_groups)
        xa = _moe_layer(xa, norm_ffn_g[i], sh_f, sc_f, g_f, rw_pad, rb_col, moe_w_gate[i].astype(BF16),
                        moe_w_up[i].astype(BF16), moe_w_down[i].astype(BF16), seq, n_groups)
    return xa[:n_lat].reshape(batch, seq, d)
```

```python
import functools
import math

import jax
import jax.numpy as jnp
from jax import lax
from jax.experimental import pallas as pl
from jax.experimental.pallas import tpu as pltpu

F32 = jnp.float32
BF16 = jnp.bfloat16

GRID_W = 64
ROPE_BASE = 10000.0
NORM_EPS = 1e-6
N_MIXERS = 3

MLA_HEADS = 16
MLA_Q_LORA = 512
MLA_KV_LORA = 512
MLA_NOPE = 128
MLA_ROPE = 64
MLA_V = 128
MLA_HEAD_PAD = 256

DIFF_HEADS = 8
DIFF_HEAD_DIM = 128

RET_HEADS = 8
RET_DK = 256
RET_DV = 512
RET_CHUNK = 256

N_EXPERTS = 16
N_GROUPS = 4
EXPERTS_PER_GROUP = 4
D_EXPERT = 1024

LANES = 128
SUBLANES = 8
VMEM_LIMIT = 56 * 1024 * 1024

ROW_TILE = 256
MM_ROW_TILE = 512
MOE_ROW_TILE = 256
KEY_CHUNK = 512
ATTN_Q_TILE = 512
LOG2E = 1.4426950408889634
ADA_COL_TILE = 1024


def _params(sem):
    return pltpu.CompilerParams(dimension_semantics=sem, vmem_limit_bytes=VMEM_LIMIT)


def _rms(x, width):
    return x * lax.rsqrt(jnp.sum(x * x, axis=-1, keepdims=True) / width + NORM_EPS)


def _ada_kernel(ct_ref, w_ref, b_ref, o_ref, sb_ref, *, n_cond):
    d = ct_ref.shape[0]
    tn = o_ref.shape[-1]

    @pl.when((pl.program_id(0) == 0) & (pl.program_id(1) == 0))
    def _():
        ct = ct_ref[...]
        s = ct * jax.nn.sigmoid(ct)
        for r in range(n_cond):
            sb_ref[r] = jnp.broadcast_to(s[:, r:r + 1], (d, LANES))

    def body(kc, accs):
        k0 = pl.multiple_of(kc * SUBLANES, SUBLANES)
        w8 = w_ref[0, pl.ds(k0, SUBLANES), :]
        out = []
        for r in range(n_cond):
            s8 = sb_ref[r, pl.ds(k0, SUBLANES), :]
            out.append(accs[r] + w8 * jnp.concatenate([s8] * (tn // LANES), axis=1))
        return tuple(out)

    accs = lax.fori_loop(0, d // SUBLANES, body,
                         tuple(jnp.zeros((SUBLANES, tn), F32) for _ in range(n_cond)), unroll=4)
    rows = [jnp.sum(a, axis=0, keepdims=True) + b_ref[0] for a in accs]
    rows.append(jnp.zeros((SUBLANES - n_cond, tn), F32))
    o_ref[0] = jnp.concatenate(rows, axis=0)


def _ada_mods(cond_t, ada_w, ada_b, n_cond):
    depth, d, n6 = ada_w.shape
    tn = ADA_COL_TILE
    return pl.pallas_call(
        functools.partial(_ada_kernel, n_cond=n_cond),
        grid=(depth, n6 // tn),
        in_specs=[pl.BlockSpec((d, SUBLANES), lambda l, j: (0, 0)),
                  pl.BlockSpec((1, d, tn), lambda l, j: (l, 0, j)),
                  pl.BlockSpec((1, 1, tn), lambda l, j: (l, 0, j))],
        out_specs=pl.BlockSpec((1, SUBLANES, tn), lambda l, j: (l, 0, j)),
        out_shape=jax.ShapeDtypeStruct((depth, SUBLANES, n6), F32),
        scratch_shapes=[pltpu.VMEM((n_cond, d, LANES), F32)],
        compiler_params=_params(("arbitrary", "arbitrary")),
        name="ada_mods",
    )(cond_t, ada_w, ada_b.reshape(depth, 1, n6))


def _group_spec(d, tm, rows_per_group, n_groups):
    return pl.BlockSpec((1, 1, d), lambda i, *_: (jnp.minimum(i * tm // rows_per_group, n_groups - 1), 0, 0))


def _norm_mod_kernel(x_ref, g_ref, sh_ref, sc_ref, o_ref):
    x = x_ref[...]
    h = _rms(x, x.shape[-1]) * g_ref[...] * (1.0 + sc_ref[0]) + sh_ref[0]
    o_ref[...] = h.astype(o_ref.dtype)


def _norm_mod(x, g, sh, sc, rows_per_group, n_groups):
    n, d = x.shape
    tm = ROW_TILE
    gs = _group_spec(d, tm, rows_per_group, n_groups)
    return pl.pallas_call(
        _norm_mod_kernel,
        grid=(n // tm,),
        in_specs=[pl.BlockSpec((tm, d), lambda i: (i, 0)), pl.BlockSpec((1, d), lambda i: (0, 0)), gs, gs],
        out_specs=pl.BlockSpec((tm, d), lambda i: (i, 0)),
        out_shape=jax.ShapeDtypeStruct((n, d), BF16),
        compiler_params=_params(("parallel",)),
        name="norm_mod",
    )(x, g.reshape(1, d), sh, sc)


def _route_kernel(x_ref, g_ref, sh_ref, sc_ref, rw_ref, rb_ref, h_ref, idx_ref, wt_ref, cnt_ref, carry_ref):
    @pl.when(pl.program_id(0) == 0)
    def _():
        carry_ref[...] = jnp.zeros_like(carry_ref)

    x = x_ref[...]
    tm = x.shape[0]
    h = _rms(x, x.shape[-1]) * g_ref[...] * (1.0 + sc_ref[0]) + sh_ref[0]
    h_ref[...] = h.astype(h_ref.dtype)
    logits = jnp.dot(h, rw_ref[...], precision=lax.Precision.HIGHEST, preferred_element_type=F32)
    scores = jax.nn.sigmoid(logits.T[:N_EXPERTS])
    sel = scores + rb_ref[...]
    sel_r = [sel[e:e + 1, :] for e in range(N_EXPERTS)]
    sc_r = [scores[e:e + 1, :] for e in range(N_EXPERTS)]

    gscore = []
    for g in range(N_GROUPS):
        a, b, c, d = sel_r[4 * g:4 * g + 4]
        hi1, lo1, hi2, lo2 = jnp.maximum(a, b), jnp.minimum(a, b), jnp.maximum(c, d), jnp.minimum(c, d)
        gscore.append(jnp.maximum(hi1, hi2) + jnp.maximum(jnp.minimum(hi1, hi2), jnp.maximum(lo1, lo2)))
    grp = jnp.zeros((1, tm), jnp.int32)
    best = gscore[0]
    for g in range(1, N_GROUPS):
        better = gscore[g] > best
        grp = jnp.where(better, g, grp)
        best = jnp.where(better, gscore[g], best)

    def pick(rows, l):
        out = rows[l]
        for g in range(1, N_GROUPS):
            out = jnp.where(grp == g, rows[4 * g + l], out)
        return out

    v = [pick(sel_r, l) for l in range(EXPERTS_PER_GROUP)]
    s = [pick(sc_r, l) for l in range(EXPERTS_PER_GROUP)]

    def first_max(vals):
        m = jnp.maximum(jnp.maximum(vals[0], vals[1]), jnp.maximum(vals[2], vals[3]))
        l = jnp.where(vals[0] == m, 0, jnp.where(vals[1] == m, 1, jnp.where(vals[2] == m, 2, 3)))
        return l

    def at(vals, l):
        return jnp.where(l == 0, vals[0], jnp.where(l == 1, vals[1], jnp.where(l == 2, vals[2], vals[3])))

    l1 = first_max(v)
    l2 = first_max([jnp.where(l1 == l, -jnp.inf, v[l]) for l in range(EXPERTS_PER_GROUP)])
    s1, s2 = at(s, l1), at(s, l2)
    tot = s1 + s2
    e1, e2 = grp * EXPERTS_PER_GROUP + l1, grp * EXPERTS_PER_GROUP + l2

    eids = lax.broadcasted_iota(jnp.int32, (N_EXPERTS, tm), 0)
    oh1, oh2 = (eids == e1).astype(F32), (eids == e2).astype(F32)
    before = (lax.broadcasted_iota(jnp.int32, (tm, tm), 0) < lax.broadcasted_iota(jnp.int32, (tm, tm), 1)).astype(BF16)
    carry = carry_ref[:, :1]
    tot1 = jnp.sum(oh1, axis=1, keepdims=True)
    pre1 = carry + jnp.dot(oh1.astype(BF16), before, preferred_element_type=F32)
    pre2 = carry + tot1 + jnp.dot(oh2.astype(BF16), before, preferred_element_type=F32)
    r1 = jnp.sum(oh1 * pre1, axis=0, keepdims=True).astype(jnp.int32)
    r2 = jnp.sum(oh2 * pre2, axis=0, keepdims=True).astype(jnp.int32)
    carry_new = carry + tot1 + jnp.sum(oh2, axis=1, keepdims=True)
    carry_ref[...] = jnp.broadcast_to(carry_new, carry_ref.shape)
    cnt_ref[...] = jnp.broadcast_to(carry_new, cnt_ref.shape).astype(jnp.int32)

    idx_ref[...] = jnp.concatenate([e1, e2, r1, r2, jnp.zeros((SUBLANES - 4, tm), jnp.int32)], axis=0)
    wt_ref[...] = jnp.concatenate([s1 / tot, s2 / tot, jnp.zeros((LANES - 2, tm), F32)], axis=0).T


def _norm_mod_route(x, g, sh, sc, rw_pad, rb_col, rows_per_group, n_groups):
    n, d = x.shape
    tm = ROW_TILE
    gs = _group_spec(d, tm, rows_per_group, n_groups)
    return pl.pallas_call(
        _route_kernel,
        grid=(n // tm,),
        in_specs=[pl.BlockSpec((tm, d), lambda i: (i, 0)), pl.BlockSpec((1, d), lambda i: (0, 0)), gs, gs,
                  pl.BlockSpec((d, LANES), lambda i: (0, 0)), pl.BlockSpec((N_EXPERTS, 1), lambda i: (0, 0))],
        out_specs=[pl.BlockSpec((tm, d), lambda i: (i, 0)),
                   pl.BlockSpec((SUBLANES, tm), lambda i: (0, i)),
                   pl.BlockSpec((tm, LANES), lambda i: (i, 0)),
                   pl.BlockSpec((N_EXPERTS, LANES), lambda i: (0, 0))],
        out_shape=[jax.ShapeDtypeStruct((n, d), F32),
                   jax.ShapeDtypeStruct((SUBLANES, n), jnp.int32),
                   jax.ShapeDtypeStruct((n, LANES), F32),
                   jax.ShapeDtypeStruct((N_EXPERTS, LANES), jnp.int32)],
        scratch_shapes=[pltpu.VMEM((N_EXPERTS, LANES), F32)],
        compiler_params=_params(("arbitrary",)),
        name="norm_mod_route",
    )(x, g.reshape(1, d), sh, sc, rw_pad, rb_col)


def _half_swap_rope(x, c, s):
    return x * c + pltpu.roll(x, LANES // 2, 1) * s


def _proj_kernel(*refs, mode, scale):
    x_ref, w_ref = refs[0], refs[1]
    o_ref, wb_ref = refs[-2], refs[-1]

    @pl.when(pl.program_id(1) == 0)
    def _():
        wb_ref[...] = w_ref[...].astype(BF16)

    acc = jnp.dot(x_ref[...], wb_ref[...], preferred_element_type=F32)
    tn = acc.shape[1]
    if mode == "plain":
        o_ref[...] = acc.astype(o_ref.dtype)
    elif mode == "silu":
        o_ref[...] = (acc * jax.nn.sigmoid(acc)).astype(o_ref.dtype)
    elif mode == "norm_rope":
        g, c, s = refs[2][...], refs[3][...], refs[4][...]
        for j in range(tn // LANES):
            blk = _rms(acc[:, j * LANES:(j + 1) * LANES], LANES) * g
            o_ref[:, j * LANES:(j + 1) * LANES] = (_half_swap_rope(blk, c, s) * scale).astype(o_ref.dtype)
    elif mode == "rope256":
        c, s = refs[2][...], refs[3][...]
        for j in range(tn // (2 * LANES)):
            x1 = acc[:, (2 * j) * LANES:(2 * j + 1) * LANES]
            x2 = acc[:, (2 * j + 1) * LANES:(2 * j + 2) * LANES]
            o_ref[:, (2 * j) * LANES:(2 * j + 1) * LANES] = ((x1 * c - x2 * s) * scale).astype(o_ref.dtype)
            o_ref[:, (2 * j + 1) * LANES:(2 * j + 2) * LANES] = ((x2 * c + x1 * s) * scale).astype(o_ref.dtype)
    else:
        raise ValueError(mode)


def _proj(x, w, col0, n_cols, *, mode="plain", scale=1.0, extras=(), out_dtype=BF16, tn=512):
    n, k = x.shape
    tm = MM_ROW_TILE
    tn = min(tn, n_cols)
    j0 = col0 // tn
    extra_specs = []
    for e in extras:
        if e.shape[0] == 1:
            extra_specs.append(pl.BlockSpec(e.shape, lambda j, i: (0, 0)))
        else:
            extra_specs.append(pl.BlockSpec((tm, e.shape[1]), lambda j, i: (i, 0)))
    return pl.pallas_call(
        functools.partial(_proj_kernel, mode=mode, scale=scale),
        grid=(n_cols // tn, n // tm),
        in_specs=[pl.BlockSpec((tm, k), lambda j, i: (i, 0)),
                  pl.BlockSpec((k, tn), lambda j, i: (0, j0 + j))] + extra_specs,
        out_specs=pl.BlockSpec((tm, tn), lambda j, i: (i, j)),
        out_shape=jax.ShapeDtypeStruct((n, n_cols), out_dtype),
        scratch_shapes=[pltpu.VMEM((k, tn), BF16)],
        compiler_params=_params(("arbitrary", "arbitrary")),
        name="proj_" + mode,
    )(x, w, *extras)


def _out_res_kernel(y_ref, w_ref, x_ref, g_ref, o_ref, wb_ref):
    @pl.when(pl.program_id(1) == 0)
    def _():
        wb_ref[...] = w_ref[...].astype(BF16)

    acc = jnp.dot(y_ref[...], wb_ref[...], preferred_element_type=F32)
    o_ref[...] = x_ref[...] + g_ref[0] * acc


def _out_res(y, w, x, gate, rows_per_group, n_groups, tn=512):
    n, k = y.shape
    d = w.shape[1]
    tm = MM_ROW_TILE
    return pl.pallas_call(
        _out_res_kernel,
        grid=(d // tn, n // tm),
        in_specs=[pl.BlockSpec((tm, k), lambda j, i: (i, 0)),
                  pl.BlockSpec((k, tn), lambda j, i: (0, j)),
                  pl.BlockSpec((tm, tn), lambda j, i: (i, j)),
                  pl.BlockSpec((1, 1, tn), lambda j, i: (jnp.minimum(i * tm // rows_per_group, n_groups - 1), 0, j))],
        out_specs=pl.BlockSpec((tm, tn), lambda j, i: (i, j)),
        out_shape=jax.ShapeDtypeStruct((n, d), F32),
        scratch_shapes=[pltpu.VMEM((k, tn), BF16)],
        compiler_params=_params(("arbitrary", "arbitrary")),
        name="out_res",
    )(y, w, x, gate)


def _mla_q_kernel(z_ref, ga_ref, w_ref, g_ref, c_ref, s_ref, o_ref, *, scale):
    cq = z_ref[...]
    cn = (_rms(cq, cq.shape[-1]) * ga_ref[...]).astype(BF16)
    acc = jnp.dot(cn, w_ref[...], preferred_element_type=F32)
    g, c, s = g_ref[...], c_ref[...], s_ref[...]
    width = float(MLA_NOPE + MLA_ROPE)
    for j in range(acc.shape[1] // MLA_HEAD_PAD):
        qh = acc[:, j * MLA_HEAD_PAD:(j + 1) * MLA_HEAD_PAD]
        qn = _rms(qh, width) * g
        o_ref[:, j * MLA_HEAD_PAD:j * MLA_HEAD_PAD + LANES] = (qn[:, :LANES] * scale).astype(o_ref.dtype)
        o_ref[:, j * MLA_HEAD_PAD + LANES:(j + 1) * MLA_HEAD_PAD] = (
            _half_swap_rope(qn[:, LANES:], c, s) * scale).astype(o_ref.dtype)


def _mla_q(z, q_a_g, w_q_b_pad, q_g_pad, cos_t, sin_t, scale, tn=1024):
    n = z.shape[0]
    tm = MM_ROW_TILE
    n_out = w_q_b_pad.shape[1]
    return pl.pallas_call(
        functools.partial(_mla_q_kernel, scale=scale),
        grid=(n // tm, n_out // tn),
        in_specs=[pl.BlockSpec((tm, MLA_Q_LORA), lambda i, j: (i, 0)),
                  pl.BlockSpec((1, MLA_Q_LORA), lambda i, j: (0, 0)),
                  pl.BlockSpec((MLA_Q_LORA, tn), lambda i, j: (0, j)),
                  pl.BlockSpec((1, MLA_HEAD_PAD), lambda i, j: (0, 0)),
                  pl.BlockSpec((tm, LANES), lambda i, j: (i, 0)),
                  pl.BlockSpec((tm, LANES), lambda i, j: (i, 0))],
        out_specs=pl.BlockSpec((tm, tn), lambda i, j: (i, j)),
        out_shape=jax.ShapeDtypeStruct((n, n_out), BF16),
        compiler_params=_params(("parallel", "arbitrary")),
        name="mla_q",
    )(z, q_a_g.reshape(1, -1), w_q_b_pad, q_g_pad, cos_t, sin_t)


def _mla_kv_kernel(z_ref, kr_ref, ga_ref, w_ref, g_ref, c_ref, s_ref, k_ref, v_ref):
    ckv = z_ref[...]
    cn = (_rms(ckv, ckv.shape[-1]) * ga_ref[...]).astype(BF16)
    acc = jnp.dot(cn, w_ref[...], preferred_element_type=F32)
    kr = kr_ref[...]
    kr_ss = jnp.sum(kr * kr, axis=-1, keepdims=True)
    g, c, s = g_ref[...], c_ref[...], s_ref[...]
    width = float(MLA_NOPE + MLA_ROPE)
    for j in range(acc.shape[1] // MLA_HEAD_PAD):
        kn = acc[:, j * MLA_HEAD_PAD:j * MLA_HEAD_PAD + MLA_NOPE]
        vv = acc[:, j * MLA_HEAD_PAD + MLA_NOPE:(j + 1) * MLA_HEAD_PAD]
        f = lax.rsqrt((jnp.sum(kn * kn, axis=-1, keepdims=True) + kr_ss) / width + NORM_EPS)
        k_ref[:, j * MLA_HEAD_PAD:j * MLA_HEAD_PAD + LANES] = (kn * f * g[:, :LANES]).astype(k_ref.dtype)
        k_ref[:, j * MLA_HEAD_PAD + LANES:(j + 1) * MLA_HEAD_PAD] = _half_swap_rope(
            kr * f * g[:, LANES:], c, s).astype(k_ref.dtype)
        v_ref[:, j * MLA_HEAD_PAD:j * MLA_HEAD_PAD + MLA_V] = vv.astype(v_ref.dtype)
        v_ref[:, j * MLA_HEAD_PAD + MLA_V:(j + 1) * MLA_HEAD_PAD] = jnp.ones((vv.shape[0], LANES), v_ref.dtype)


def _mla_kv(z, kv_a_g, w_kv_b, k_g_pad, cos_t, sin_t, tn=1024):
    n = z.shape[0]
    tm = MM_ROW_TILE
    n_out = w_kv_b.shape[1]
    kr_block = (MLA_Q_LORA + MLA_KV_LORA) // LANES
    return pl.pallas_call(
        _mla_kv_kernel,
        grid=(n // tm, n_out // tn),
        in_specs=[pl.BlockSpec((tm, MLA_KV_LORA), lambda i, j: (i, 1)),
                  pl.BlockSpec((tm, LANES), lambda i, j: (i, kr_block)),
                  pl.BlockSpec((1, MLA_KV_LORA), lambda i, j: (0, 0)),
                  pl.BlockSpec((MLA_KV_LORA, tn), lambda i, j: (0, j)),
                  pl.BlockSpec((1, MLA_HEAD_PAD), lambda i, j: (0, 0)),
                  pl.BlockSpec((tm, LANES), lambda i, j: (i, 0)),
                  pl.BlockSpec((tm, LANES), lambda i, j: (i, 0))],
        out_specs=[pl.BlockSpec((tm, tn), lambda i, j: (i, j)),
                   pl.BlockSpec((tm, tn), lambda i, j: (i, j))],
        out_shape=[jax.ShapeDtypeStruct((n, n_out), BF16),
                   jax.ShapeDtypeStruct((n, n_out), BF16)],
        compiler_params=_params(("parallel", "arbitrary")),
        name="mla_kv",
    )(z, z, kv_a_g.reshape(1, -1), w_kv_b, k_g_pad, cos_t, sin_t)


def _softmax_pv(q, k_refs, v_refs, c0, dq, *, den_from_v):
    chunks = []
    for k_ref, v_ref in zip(k_refs, v_refs):
        step = min(k_ref.shape[0], KEY_CHUNK)
        chunks += [(k_ref, v_ref, r0, step) for r0 in range(0, k_ref.shape[0], step)]
    m = acc = den = None
    for k_ref, v_ref, r0, step in chunks:
        s = lax.dot_general(q, k_ref[r0:r0 + step, c0:c0 + dq], (((1,), (1,)), ((), ())),
                            preferred_element_type=F32)
        mc = jnp.max(s, axis=-1, keepdims=True)
        m_new = mc if m is None else jnp.maximum(m, mc)
        p = jnp.exp2((s - m_new).astype(BF16)) if den_from_v else jnp.exp2(s - m_new)
        pv = jnp.dot(p.astype(BF16), v_ref[r0:r0 + step, :], preferred_element_type=F32)
        if m is None:
            acc = pv
            if not den_from_v:
                den = jnp.sum(p, axis=-1, keepdims=True)
        else:
            alpha = jnp.exp2(m - m_new)
            acc = acc * alpha + pv
            if not den_from_v:
                den = den * alpha + jnp.sum(p, axis=-1, keepdims=True)
        m = m_new
    if den_from_v:
        dv = acc.shape[1] - LANES
        return acc[:, :dv] / acc[:, dv:dv + 1]
    return acc / den


def _attn_kernel(*refs, n_seg, dq):
    q_ref = refs[0]
    k_refs = refs[1:1 + n_seg]
    v_refs = refs[1 + n_seg:1 + 2 * n_seg]
    o_ref = refs[-1]
    o_ref[...] = _softmax_pv(q_ref[...], k_refs, v_refs, 0, dq, den_from_v=True).astype(o_ref.dtype)


def _diff_attn_kernel(*refs, n_seg, dq, lam_init):
    q_ref = refs[0]
    k_refs = refs[1:1 + n_seg]
    v_refs = refs[1 + n_seg:1 + 2 * n_seg]
    lam_ref, g_ref, o_ref = refs[-3], refs[-2], refs[-1]
    lf = lam_ref[...]
    lam = (jnp.exp(jnp.sum(lf[0:1] * lf[1:2], axis=-1, keepdims=True))
           - jnp.exp(jnp.sum(lf[2:3] * lf[3:4], axis=-1, keepdims=True)) + lam_init)
    q = q_ref[...]
    o1 = _softmax_pv(q[:, :dq], k_refs, v_refs, 0, dq, den_from_v=False)
    o2 = _softmax_pv(q[:, dq:], k_refs, v_refs, dq, dq, den_from_v=False)
    o = o1 - lam * o2
    o_ref[...] = (_rms(o, o.shape[-1]) * g_ref[...] * (1.0 - lam_init)).astype(o_ref.dtype)


def _attention(q, k, v, *, heads, q_width, v_width, dv, seq, n_ctx, batch, lat_queries, diff=None):
    lat_rows = batch * seq
    ctx_blk0 = lat_rows // n_ctx
    if lat_queries:
        tq = ATTN_Q_TILE
        nq = seq // tq
        q_map = lambda b, h, i: (b * nq + i, h)
        segs = [(seq, lambda b, h, i: (b, h)), (n_ctx, lambda b, h, i: (ctx_blk0 + b, h))]
        out_rows = lat_rows
    else:
        tq = n_ctx
        nq = 1
        q_map = lambda b, h, i: (ctx_blk0 + b, h)
        segs = [(n_ctx, lambda b, h, i: (ctx_blk0 + b, h))]
        out_rows = batch * n_ctx
    n_seg = len(segs)
    in_specs = [pl.BlockSpec((tq, q_width), q_map)]
    in_specs += [pl.BlockSpec((rows, q_width), m) for rows, m in segs]
    in_specs += [pl.BlockSpec((rows, v_width), m) for rows, m in segs]
    args = [q] + [k] * n_seg + [v] * n_seg
    if diff is None:
        kern = functools.partial(_attn_kernel, n_seg=n_seg, dq=q_width)
    else:
        lam, subln_g, lam_init = diff
        kern = functools.partial(_diff_attn_kernel, n_seg=n_seg, dq=q_width // 2, lam_init=lam_init)
        in_specs += [pl.BlockSpec(lam.shape, lambda b, h, i: (0, 0)), pl.BlockSpec((1, dv), lambda b, h, i: (0, 0))]
        args += [lam, subln_g.reshape(1, dv)]
    return pl.pallas_call(
        kern,
        grid=(batch, heads, nq),
        in_specs=in_specs,
        out_specs=pl.BlockSpec((tq, dv), lambda b, h, i: (b * nq + i, h)),
        out_shape=jax.ShapeDtypeStruct((out_rows, heads * dv), BF16),
        compiler_params=_params(("parallel", "parallel", "arbitrary")),
        name="attention" if diff is None else "diff_attention",
    )(*args)


def _ret_kernel(dec_ref, q_ref, k_ref, v_ref, o_ref, state_ref, *, backward):
    @pl.when(pl.program_id(2) == 0)
    def _():
        state_ref[...] = jnp.zeros_like(state_ref)

    ch = q_ref.shape[0]
    lg = jax.nn.log_sigmoid(dec_ref[0])[:, :1]
    ii = lax.broadcasted_iota(jnp.int32, (ch, ch), 0).astype(F32)
    jj = lax.broadcasted_iota(jnp.int32, (ch, ch), 1).astype(F32)
    pos = lax.broadcasted_iota(jnp.int32, (ch, 1), 0).astype(F32)
    if backward:
        dist, valid = jj - ii, jj > ii
        q_dec, k_dec = jnp.exp(lg * (ch - pos)), jnp.exp(lg * pos)
    else:
        dist, valid = ii - jj, ii >= jj
        q_dec, k_dec = jnp.exp(lg * (pos + 1.0)), jnp.exp(lg * (ch - 1.0 - pos))
    intra = jnp.where(valid, jnp.exp(lg * jnp.maximum(dist, 0.0)), 0.0)
    q, k, v = q_ref[...], k_ref[...], v_ref[...]
    scores = lax.dot_general(q, k, (((1,), (1,)), ((), ())), preferred_element_type=F32) * intra
    state = state_ref[...]
    o = jnp.dot(scores.astype(BF16), v, preferred_element_type=F32)
    o += jnp.dot((q.astype(F32) * q_dec).astype(BF16), state.astype(BF16), preferred_element_type=F32)
    o_ref[...] = o.astype(o_ref.dtype)
    kd_t = (k.astype(F32) * k_dec).T.astype(BF16)
    state_ref[...] = state * jnp.exp(lg * ch) + jnp.dot(kd_t, v, preferred_element_type=F32)


def _retention(dec, q, k, v, *, backward, seq, n_ctx, batch):
    ch = RET_CHUNK
    n_c, n_s = n_ctx // ch, seq // ch
    ctx_blk0 = batch * seq // ch

    def row_block(b, t):
        if backward:
            return jnp.where(t < n_c, ctx_blk0 + b * n_c + (n_c - 1 - t), b * n_s + (n_s - 1 - (t - n_c)))
        return jnp.where(t < n_c, ctx_blk0 + b * n_c + t, b * n_s + (t - n_c))

    spec = lambda width: pl.BlockSpec((ch, width), lambda b, h, t: (row_block(b, t), h))
    return pl.pallas_call(
        functools.partial(_ret_kernel, backward=backward),
        grid=(batch, RET_HEADS, n_c + n_s),
        in_specs=[pl.BlockSpec((1, 1, LANES), lambda b, h, t: (h, 0, 0)), spec(RET_DK), spec(RET_DK), spec(RET_DV)],
        out_specs=spec(RET_DV),
        out_shape=jax.ShapeDtypeStruct((q.shape[0], RET_HEADS * RET_DV), BF16),
        scratch_shapes=[pltpu.VMEM((RET_DK, RET_DV), F32)],
        compiler_params=_params(("parallel", "parallel", "arbitrary")),
        name="retention_bwd" if backward else "retention_fwd",
    )(dec, q, k, v)


def _ret_finish_kernel(of_ref, ob_ref, gate_ref, g_ref, o_ref):
    for h in range(RET_HEADS):
        cols = slice(h * RET_DV, (h + 1) * RET_DV)
        y = of_ref[:, cols].astype(F32) + ob_ref[:, cols].astype(F32)
        o_ref[:, cols] = (gate_ref[:, cols].astype(F32) * (_rms(y, RET_DV) * g_ref[:, cols])).astype(o_ref.dtype)


def _ret_finish(o_f, o_b, gate, norm_g):
    n, w = o_f.shape
    tm = ROW_TILE
    row = pl.BlockSpec((tm, w), lambda i: (i, 0))
    return pl.pallas_call(
        _ret_finish_kernel,
        grid=(n // tm,),
        in_specs=[row, row, row, pl.BlockSpec((1, w), lambda i: (0, 0))],
        out_specs=row,
        out_shape=jax.ShapeDtypeStruct((n, w), BF16),
        compiler_params=_params(("parallel",)),
        name="ret_finish",
    )(o_f, o_b, gate, norm_g.reshape(1, w))


def _moe_kernel(te_ref, nu_ref, x_ref, wg_ref, wu_ref, wd_ref, o_ref):
    t = pl.program_id(0)

    @pl.when(t < nu_ref[0])
    def _():
        x = x_ref[...].astype(BF16)
        a = jnp.dot(x, wg_ref[0, 0], preferred_element_type=F32)
        u = jnp.dot(x, wu_ref[0, 0], preferred_element_type=F32)
        act = (a * jax.nn.sigmoid(a) * u).astype(BF16)
        o_ref[...] = jnp.dot(act, wd_ref[0, 0], preferred_element_type=F32).astype(o_ref.dtype)

    @pl.when(t >= nu_ref[0])
    def _():
        o_ref[...] = jnp.zeros_like(o_ref)


def _moe_ffn(tile_expert, n_used, x_sorted, wg, wu, wd, layer):
    r, d = x_sorted.shape
    f = wg.shape[3]
    tm = MOE_ROW_TILE
    return pl.pallas_call(
        _moe_kernel,
        grid_spec=pltpu.PrefetchScalarGridSpec(
            num_scalar_prefetch=2,
            grid=(r // tm,),
            in_specs=[pl.BlockSpec((tm, d), lambda t, te, nu: (t, 0)),
                      pl.BlockSpec((1, 1, d, f), lambda t, te, nu: (layer, te[t], 0, 0)),
                      pl.BlockSpec((1, 1, d, f), lambda t, te, nu: (layer, te[t], 0, 0)),
                      pl.BlockSpec((1, 1, f, d), lambda t, te, nu: (layer, te[t], 0, 0))],
            out_specs=pl.BlockSpec((tm, d), lambda t, te, nu: (t, 0)),
        ),
        out_shape=jax.ShapeDtypeStruct((r, d), F32),
        compiler_params=_params(("arbitrary",)),
        name="moe_ffn",
    )(tile_expert, n_used, x_sorted, wg, wu, wd)


def _combine_kernel(x_ref, a_ref, b_ref, w_ref, g_ref, o_ref):
    w = w_ref[...]
    o_ref[...] = x_ref[...] + g_ref[0] * (w[:, 0:1] * a_ref[...] + w[:, 1:2] * b_ref[...])


def _combine(x, a, b, w_col, gate, rows_per_group, n_groups):
    n, d = x.shape
    tm = ROW_TILE
    row = pl.BlockSpec((tm, d), lambda i: (i, 0))
    return pl.pallas_call(
        _combine_kernel,
        grid=(n // tm,),
        in_specs=[row, row, row, pl.BlockSpec((tm, LANES), lambda i: (i, 0)),
                  _group_spec(d, tm, rows_per_group, n_groups)],
        out_specs=row,
        out_shape=jax.ShapeDtypeStruct((n, d), F32),
        compiler_params=_params(("parallel",)),
        name="moe_combine",
    )(x, a, b, w_col, gate)


def _moe_layer(x, g, sh, sc, gate, rw_pad, rb_col, wg, wu, wd, layer, rows_per_group, n_groups):
    n, d = x.shape
    tm = MOE_ROW_TILE
    h, idx8, w_col, cnt = _norm_mod_route(x, g, sh, sc, rw_pad, rb_col, rows_per_group, n_groups)
    experts = jnp.arange(N_EXPERTS, dtype=jnp.int32)
    padded = (cnt[:, 0] + tm - 1) // tm * tm
    ends = jnp.cumsum(padded)
    starts = ends - padded
    e2, rank2 = idx8[0:2], idx8[2:4]
    dest = rank2 + jnp.sum(jnp.where(e2[..., None] == experts, starts, 0), axis=-1)
    r = (2 * n + N_EXPERTS * (tm - 1)) // tm * tm
    tok = jnp.tile(jnp.arange(n, dtype=jnp.int32), 2)
    src = (jnp.arange(r, dtype=jnp.int32) % n).at[dest.reshape(-1)].set(tok, unique_indices=True,
                                                                         mode="promise_in_bounds")
    tile_start = jnp.arange(r // tm, dtype=jnp.int32) * tm
    tile_expert = jnp.minimum(jnp.sum((ends[None, :] <= tile_start[:, None]).astype(jnp.int32), axis=1),
                              N_EXPERTS - 1)
    n_used = ends[-1:] // tm
    x_sorted = h.at[src].get(mode="promise_in_bounds")
    y_sorted = _moe_ffn(tile_expert, n_used, x_sorted, wg, wu, wd, layer)
    a = y_sorted.at[dest[0]].get(mode="promise_in_bounds")
    b = y_sorted.at[dest[1]].get(mode="promise_in_bounds")
    return _combine(x, a, b, w_col, gate, rows_per_group, n_groups)


def _axial_tables(rows, rot_dim):
    n_freq = rot_dim // 4
    inv_freq = jnp.power(ROPE_BASE, -jnp.arange(n_freq, dtype=F32) / n_freq)
    row = jnp.repeat(jnp.arange(rows, dtype=F32), GRID_W)
    col = jnp.tile(jnp.arange(GRID_W, dtype=F32), rows)
    ang = jnp.concatenate([row[:, None] * inv_freq, col[:, None] * inv_freq], axis=-1)
    return jnp.cos(ang), jnp.sin(ang)


def _flat_tables(c_lat, s_lat, batch, n_ctx_rows):
    c = jnp.concatenate([jnp.tile(c_lat, (batch, 1)), jnp.broadcast_to(c_lat[:1], (n_ctx_rows, LANES))])
    s = jnp.concatenate([jnp.tile(s_lat, (batch, 1)), jnp.zeros((n_ctx_rows, LANES), F32)])
    return c, s


def _spread_pairs(a, axis):
    x1, x2 = jnp.split(a, 2, axis=axis)
    z = jnp.zeros_like(x1)
    return jnp.concatenate([x1, z, x2, z], axis=axis)


def kernel(x, c, ctx, c_ctx, ada_w, ada_b, norm_mix_g, norm_ffn_g, mla_w_in, mla_q_a_g, mla_w_q_b, mla_kv_a_g,
           mla_w_kv_b, mla_q_norm_g, mla_k_norm_g, mla_w_o, diff_w_in, diff_q_norm_g, diff_k_norm_g, diff_lambda,
           diff_subln_g, diff_w_o, ret_w_in, ret_decay_fwd, ret_decay_bwd, ret_norm_g, ret_w_o, router_w,
           router_bias, moe_w_gate, moe_w_up, moe_w_down):
    batch, seq, d = x.shape
    n_ctx = ctx.shape[1]
    depth = ada_w.shape[0]
    n_lat, n_cx = batch * seq, batch * n_ctx
    n_groups = batch + 1
    assert n_groups <= SUBLANES and seq % MM_ROW_TILE == 0 and n_cx % MM_ROW_TILE == 0
    assert seq % RET_CHUNK == 0 and n_ctx % RET_CHUNK == 0 and n_lat % n_ctx == 0 and n_ctx == ROW_TILE

    xa = jnp.concatenate([x.reshape(n_lat, d), ctx.reshape(n_cx, d)], axis=0)

    cond = jnp.concatenate([c, c_ctx[None, :], jnp.zeros((SUBLANES - n_groups, d), F32)], axis=0)
    mods = _ada_mods(cond.T, ada_w, ada_b, n_groups)

    rows = seq // GRID_W
    cos64, sin64 = _axial_tables(rows, MLA_ROPE)
    zeros32 = jnp.zeros_like(cos64)
    mla_c, mla_s = _flat_tables(jnp.concatenate([cos64, zeros32, cos64, zeros32], axis=1),
                                jnp.concatenate([-sin64, zeros32, sin64, zeros32], axis=1), batch, n_cx)
    cos128, sin128 = _axial_tables(rows, DIFF_HEAD_DIM)
    diff_c, diff_s = _flat_tables(jnp.concatenate([cos128, cos128], axis=1),
                                  jnp.concatenate([-sin128, sin128], axis=1), batch, n_cx)
    cos256, sin256 = _axial_tables(rows, RET_DK)
    ret_c, ret_s = _flat_tables(cos256, sin256, batch, n_cx)

    rw_pad = jnp.concatenate([router_w, jnp.zeros((d, LANES - N_EXPERTS), F32)], axis=1)
    rb_col = router_bias.reshape(N_EXPERTS, 1)

    wg_all, wu_all, wd_all = moe_w_gate.astype(BF16), moe_w_up.astype(BF16), moe_w_down.astype(BF16)

    for i in range(depth):
        kind, j, last = i % N_MIXERS, i // N_MIXERS, i == depth - 1
        m = mods[i].reshape(SUBLANES, 6, 1, d)
        sh_m, sc_m, g_m, sh_f, sc_f, g_f = (m[:, t] for t in range(6))
        n_all = xa.shape[0]
        h = _norm_mod(xa, norm_mix_g[i], sh_m, sc_m, seq, n_groups)
        q_rows = n_lat if last else n_all

        if kind == 0:
            w_in = jnp.concatenate([mla_w_in[j][:, :MLA_Q_LORA + MLA_KV_LORA],
                                    _spread_pairs(mla_w_in[j][:, MLA_Q_LORA + MLA_KV_LORA:], 1)], axis=1)
            z = _proj(h, w_in, 0, w_in.shape[1], out_dtype=F32, tn=w_in.shape[1])
            wq = mla_w_q_b[j].reshape(MLA_Q_LORA, MLA_HEADS, MLA_NOPE + MLA_ROPE)
            wq = jnp.concatenate([wq[..., :MLA_NOPE], _spread_pairs(wq[..., MLA_NOPE:], 2)], axis=2)
            wq = wq.reshape(MLA_Q_LORA, MLA_HEADS * MLA_HEAD_PAD).astype(BF16)
            spread_g = lambda g: jnp.concatenate([g[:MLA_NOPE], _spread_pairs(g[MLA_NOPE:], 0)]).reshape(1, -1)
            scale = float(MLA_NOPE + MLA_ROPE) ** -0.5 * LOG2E
            q = _mla_q(z[:q_rows], mla_q_a_g[j], wq, spread_g(mla_q_norm_g[j]), mla_c[:q_rows], mla_s[:q_rows], scale)
            k, v = _mla_kv(z, mla_kv_a_g[j], mla_w_kv_b[j].astype(BF16), spread_g(mla_k_norm_g[j]), mla_c, mla_s)
            attn = functools.partial(_attention, q, k, v, heads=MLA_HEADS, q_width=MLA_HEAD_PAD,
                                     v_width=MLA_V + LANES, dv=MLA_V, seq=seq, n_ctx=n_ctx, batch=batch)
            w_o = mla_w_o[j]
        elif kind == 1:
            w_in = diff_w_in[j]
            hd = DIFF_HEADS * 2 * DIFF_HEAD_DIM
            scale = float(DIFF_HEAD_DIM) ** -0.5 * LOG2E
            q = _proj(h[:q_rows], w_in, 0, hd, mode="norm_rope", scale=scale,
                      extras=(diff_q_norm_g[j].reshape(1, -1), diff_c[:q_rows], diff_s[:q_rows]))
            k = _proj(h, w_in, hd, hd, mode="norm_rope", extras=(diff_k_norm_g[j].reshape(1, -1), diff_c, diff_s))
            v = _proj(h, w_in, 2 * hd, hd)
            lam_init = 0.8 - 0.6 * math.exp(-0.3 * i)
            attn = functools.partial(_attention, q, k, v, heads=DIFF_HEADS, q_width=2 * DIFF_HEAD_DIM,
                                     v_width=2 * DIFF_HEAD_DIM, dv=2 * DIFF_HEAD_DIM, seq=seq, n_ctx=n_ctx,
                                     batch=batch,
                                     diff=(diff_lambda[j], diff_subln_g[j], lam_init))
            w_o = diff_w_o[j]
        else:
            w_in = ret_w_in[j]
            nq, nv = RET_HEADS * RET_DK, RET_HEADS * RET_DV
            q = _proj(h, w_in, 0, nq, mode="rope256", extras=(ret_c, ret_s))
            k = _proj(h, w_in, nq, nq, mode="rope256", scale=float(RET_DK) ** -0.5, extras=(ret_c, ret_s))
            v = _proj(h, w_in, 2 * nq, nv)
            gate = _proj(h, w_in, 2 * nq + nv, nv, mode="silu")
            dec = lambda p: jnp.broadcast_to(p.astype(F32)[:, None, None], (RET_HEADS, 1, LANES))
            o_f = _retention(dec(ret_decay_fwd[j]), q, k, v, backward=False, seq=seq, n_ctx=n_ctx, batch=batch)
            o_b = _retention(dec(ret_decay_bwd[j]), q, k, v, backward=True, seq=seq, n_ctx=n_ctx, batch=batch)
            y = _ret_finish(o_f, o_b, gate, ret_norm_g[j])
            attn = None
            w_o = ret_w_o[j]

        if attn is not None:
            y = attn(lat_queries=True)
            if not last:
                y = jnp.concatenate([y, attn(lat_queries=False)], axis=0)
        if last:
            xa = xa[:n_lat]
            y = y[:n_lat]
        xa = _out_res(y, w_o, xa, g_m, seq, n_groups)
        xa = _moe_layer(xa, norm_ffn_g[i], sh_f, sc_f, g_f, rw_pad, rb_col, wg_all, wu_all, wd_all, i, seq, n_groups)
    return xa[:n_lat].reshape(batch, seq, d)
```

```python
import functools
import math

import jax
import jax.numpy as jnp
from jax import lax
from jax.experimental import pallas as pl
from jax.experimental.pallas import tpu as pltpu

F32 = jnp.float32
BF16 = jnp.bfloat16

GRID_W = 64
ROPE_BASE = 10000.0
NORM_EPS = 1e-6
N_MIXERS = 3

MLA_HEADS = 16
MLA_Q_LORA = 512
MLA_KV_LORA = 512
MLA_NOPE = 128
MLA_ROPE = 64
MLA_V = 128
MLA_HEAD_PAD = 256

DIFF_HEADS = 8
DIFF_HEAD_DIM = 128

RET_HEADS = 8
RET_DK = 256
RET_DV = 512
RET_CHUNK = 256

N_EXPERTS = 16
N_GROUPS = 4
EXPERTS_PER_GROUP = 4
D_EXPERT = 1024

LANES = 128
SUBLANES = 8
VMEM_LIMIT = 56 * 1024 * 1024

ROW_TILE = 256
MM_ROW_TILE = 512
MM_COL_TILE = 1024
MM_COL_TILE_MAX_K = 2048
RET_HEAD_BLOCK = 4
WEIGHT_CAST_CHUNKS = 8
MOE_ROW_TILE = 256
KEY_CHUNK = 512
ATTN_Q_TILE = 512
LOG2E = 1.4426950408889634
ADA_COL_TILE = 1024


def _params(sem):
    return pltpu.CompilerParams(dimension_semantics=sem, vmem_limit_bytes=VMEM_LIMIT)


def _rms(x, width):
    return x * lax.rsqrt(jnp.sum(x * x, axis=-1, keepdims=True) / width + NORM_EPS)


def _ada_kernel(ct_ref, w_ref, b_ref, o_ref, sb_ref, *, n_cond):
    d = ct_ref.shape[0]
    tn = o_ref.shape[-1]

    @pl.when((pl.program_id(0) == 0) & (pl.program_id(1) == 0))
    def _():
        ct = ct_ref[...]
        s = ct * jax.nn.sigmoid(ct)
        for r in range(n_cond):
            sb_ref[r] = jnp.broadcast_to(s[:, r:r + 1], (d, LANES))

    def body(kc, accs):
        k0 = pl.multiple_of(kc * SUBLANES, SUBLANES)
        w8 = w_ref[0, pl.ds(k0, SUBLANES), :]
        out = []
        for r in range(n_cond):
            s8 = sb_ref[r, pl.ds(k0, SUBLANES), :]
            out.append(accs[r] + w8 * jnp.concatenate([s8] * (tn // LANES), axis=1))
        return tuple(out)

    accs = lax.fori_loop(0, d // SUBLANES, body,
                         tuple(jnp.zeros((SUBLANES, tn), F32) for _ in range(n_cond)), unroll=4)
    rows = [jnp.sum(a, axis=0, keepdims=True) + b_ref[0] for a in accs]
    rows.append(jnp.zeros((SUBLANES - n_cond, tn), F32))
    o_ref[0] = jnp.concatenate(rows, axis=0)


def _ada_mods(cond_t, ada_w, ada_b, n_cond):
    depth, d, n6 = ada_w.shape
    tn = ADA_COL_TILE
    return pl.pallas_call(
        functools.partial(_ada_kernel, n_cond=n_cond),
        grid=(depth, n6 // tn),
        in_specs=[pl.BlockSpec((d, SUBLANES), lambda l, j: (0, 0)),
                  pl.BlockSpec((1, d, tn), lambda l, j: (l, 0, j)),
                  pl.BlockSpec((1, 1, tn), lambda l, j: (l, 0, j))],
        out_specs=pl.BlockSpec((1, SUBLANES, tn), lambda l, j: (l, 0, j)),
        out_shape=jax.ShapeDtypeStruct((depth, SUBLANES, n6), F32),
        scratch_shapes=[pltpu.VMEM((n_cond, d, LANES), F32)],
        compiler_params=_params(("arbitrary", "arbitrary")),
        name="ada_mods",
    )(cond_t, ada_w, ada_b.reshape(depth, 1, n6))


def _group_spec(d, tm, rows_per_group, n_groups):
    return pl.BlockSpec((1, 1, d), lambda i, *_: (jnp.minimum(i * tm // rows_per_group, n_groups - 1), 0, 0))


def _norm_mod_kernel(x_ref, g_ref, sh_ref, sc_ref, o_ref):
    x = x_ref[...]
    h = _rms(x, x.shape[-1]) * g_ref[...] * (1.0 + sc_ref[0]) + sh_ref[0]
    o_ref[...] = h.astype(o_ref.dtype)


def _norm_mod(x, g, sh, sc, rows_per_group, n_groups):
    n, d = x.shape
    tm = ROW_TILE
    gs = _group_spec(d, tm, rows_per_group, n_groups)
    return pl.pallas_call(
        _norm_mod_kernel,
        grid=(n // tm,),
        in_specs=[pl.BlockSpec((tm, d), lambda i: (i, 0)), pl.BlockSpec((1, d), lambda i: (0, 0)), gs, gs],
        out_specs=pl.BlockSpec((tm, d), lambda i: (i, 0)),
        out_shape=jax.ShapeDtypeStruct((n, d), BF16),
        compiler_params=_params(("parallel",)),
        name="norm_mod",
    )(x, g.reshape(1, d), sh, sc)


def _route_kernel(x_ref, g_ref, sh_ref, sc_ref, rw_ref, rb_ref, h_ref, idx_ref, wt_ref, cnt_ref, carry_ref):
    @pl.when(pl.program_id(0) == 0)
    def _():
        carry_ref[...] = jnp.zeros_like(carry_ref)

    x = x_ref[...]
    tm = x.shape[0]
    h = _rms(x, x.shape[-1]) * g_ref[...] * (1.0 + sc_ref[0]) + sh_ref[0]
    h_ref[...] = h.astype(h_ref.dtype)
    logits = jnp.dot(h, rw_ref[...], precision=lax.Precision.HIGHEST, preferred_element_type=F32)
    scores = jax.nn.sigmoid(logits.T[:N_EXPERTS])
    sel = scores + rb_ref[...]
    sel_r = [sel[e:e + 1, :] for e in range(N_EXPERTS)]
    sc_r = [scores[e:e + 1, :] for e in range(N_EXPERTS)]

    gscore = []
    for g in range(N_GROUPS):
        a, b, c, d = sel_r[4 * g:4 * g + 4]
        hi1, lo1, hi2, lo2 = jnp.maximum(a, b), jnp.minimum(a, b), jnp.maximum(c, d), jnp.minimum(c, d)
        gscore.append(jnp.maximum(hi1, hi2) + jnp.maximum(jnp.minimum(hi1, hi2), jnp.maximum(lo1, lo2)))
    grp = jnp.zeros((1, tm), jnp.int32)
    best = gscore[0]
    for g in range(1, N_GROUPS):
        better = gscore[g] > best
        grp = jnp.where(better, g, grp)
        best = jnp.where(better, gscore[g], best)

    def pick(rows, l):
        out = rows[l]
        for g in range(1, N_GROUPS):
            out = jnp.where(grp == g, rows[4 * g + l], out)
        return out

    v = [pick(sel_r, l) for l in range(EXPERTS_PER_GROUP)]
    s = [pick(sc_r, l) for l in range(EXPERTS_PER_GROUP)]

    def first_max(vals):
        m = jnp.maximum(jnp.maximum(vals[0], vals[1]), jnp.maximum(vals[2], vals[3]))
        l = jnp.where(vals[0] == m, 0, jnp.where(vals[1] == m, 1, jnp.where(vals[2] == m, 2, 3)))
        return l

    def at(vals, l):
        return jnp.where(l == 0, vals[0], jnp.where(l == 1, vals[1], jnp.where(l == 2, vals[2], vals[3])))

    l1 = first_max(v)
    l2 = first_max([jnp.where(l1 == l, -jnp.inf, v[l]) for l in range(EXPERTS_PER_GROUP)])
    s1, s2 = at(s, l1), at(s, l2)
    tot = s1 + s2
    e1, e2 = grp * EXPERTS_PER_GROUP + l1, grp * EXPERTS_PER_GROUP + l2

    eids = lax.broadcasted_iota(jnp.int32, (N_EXPERTS, tm), 0)
    oh1, oh2 = (eids == e1).astype(F32), (eids == e2).astype(F32)
    before = (lax.broadcasted_iota(jnp.int32, (tm, tm), 0) < lax.broadcasted_iota(jnp.int32, (tm, tm), 1)).astype(BF16)
    carry = carry_ref[:, :1]
    tot1 = jnp.sum(oh1, axis=1, keepdims=True)
    pre1 = carry + jnp.dot(oh1.astype(BF16), before, preferred_element_type=F32)
    pre2 = carry + tot1 + jnp.dot(oh2.astype(BF16), before, preferred_element_type=F32)
    r1 = jnp.sum(oh1 * pre1, axis=0, keepdims=True).astype(jnp.int32)
    r2 = jnp.sum(oh2 * pre2, axis=0, keepdims=True).astype(jnp.int32)
    carry_new = carry + tot1 + jnp.sum(oh2, axis=1, keepdims=True)
    carry_ref[...] = jnp.broadcast_to(carry_new, carry_ref.shape)
    cnt_ref[...] = jnp.broadcast_to(carry_new, cnt_ref.shape).astype(jnp.int32)

    idx_ref[...] = jnp.concatenate([e1, e2, r1, r2, jnp.zeros((SUBLANES - 4, tm), jnp.int32)], axis=0)
    wt_ref[...] = jnp.concatenate([s1 / tot, s2 / tot, jnp.zeros((LANES - 2, tm), F32)], axis=0).T


def _norm_mod_route(x, g, sh, sc, rw_pad, rb_col, rows_per_group, n_groups):
    n, d = x.shape
    tm = ROW_TILE
    gs = _group_spec(d, tm, rows_per_group, n_groups)
    return pl.pallas_call(
        _route_kernel,
        grid=(n // tm,),
        in_specs=[pl.BlockSpec((tm, d), lambda i: (i, 0)), pl.BlockSpec((1, d), lambda i: (0, 0)), gs, gs,
                  pl.BlockSpec((d, LANES), lambda i: (0, 0)), pl.BlockSpec((N_EXPERTS, 1), lambda i: (0, 0))],
        out_specs=[pl.BlockSpec((tm, d), lambda i: (i, 0)),
                   pl.BlockSpec((SUBLANES, tm), lambda i: (0, i)),
                   pl.BlockSpec((tm, LANES), lambda i: (i, 0)),
                   pl.BlockSpec((N_EXPERTS, LANES), lambda i: (0, 0))],
        out_shape=[jax.ShapeDtypeStruct((n, d), F32),
                   jax.ShapeDtypeStruct((SUBLANES, n), jnp.int32),
                   jax.ShapeDtypeStruct((n, LANES), F32),
                   jax.ShapeDtypeStruct((N_EXPERTS, LANES), jnp.int32)],
        scratch_shapes=[pltpu.VMEM((N_EXPERTS, LANES), F32)],
        compiler_params=_params(("arbitrary",)),
        name="norm_mod_route",
    )(x, g.reshape(1, d), sh, sc, rw_pad, rb_col)


def _half_swap_rope(x, c, s):
    return x * c + pltpu.roll(x, LANES // 2, 1) * s


def _proj_kernel(*refs, mode, scale):
    x_ref, w_ref = refs[0], refs[1]
    o_ref, wb_ref = refs[-2], refs[-1]

    @pl.when(pl.program_id(1) == 0)
    def _():
        wb_ref[...] = w_ref[...].astype(BF16)

    acc = jnp.dot(x_ref[...], wb_ref[...], preferred_element_type=F32)
    tn = acc.shape[1]
    if mode == "plain":
        o_ref[...] = acc.astype(o_ref.dtype)
    elif mode == "silu":
        o_ref[...] = (acc * jax.nn.sigmoid(acc)).astype(o_ref.dtype)
    elif mode == "norm_rope":
        g, c, s = refs[2][...], refs[3][...], refs[4][...]
        for j in range(tn // LANES):
            blk = _rms(acc[:, j * LANES:(j + 1) * LANES], LANES) * g
            o_ref[:, j * LANES:(j + 1) * LANES] = (_half_swap_rope(blk, c, s) * scale).astype(o_ref.dtype)
    elif mode == "rope256":
        c, s = refs[2][...], refs[3][...]
        for j in range(tn // (2 * LANES)):
            x1 = acc[:, (2 * j) * LANES:(2 * j + 1) * LANES]
            x2 = acc[:, (2 * j + 1) * LANES:(2 * j + 2) * LANES]
            o_ref[:, (2 * j) * LANES:(2 * j + 1) * LANES] = ((x1 * c - x2 * s) * scale).astype(o_ref.dtype)
            o_ref[:, (2 * j + 1) * LANES:(2 * j + 2) * LANES] = ((x2 * c + x1 * s) * scale).astype(o_ref.dtype)
    else:
        raise ValueError(mode)


def _col_tile(k):
    return MM_COL_TILE if k <= MM_COL_TILE_MAX_K else MM_COL_TILE // 2


def _proj(x, w, col0, n_cols, *, mode="plain", scale=1.0, extras=(), out_dtype=BF16, tn=None):
    n, k = x.shape
    tm = MM_ROW_TILE
    tn = min(_col_tile(k) if tn is None else tn, n_cols)
    j0 = col0 // tn
    extra_specs = []
    for e in extras:
        if e.shape[0] == 1:
            extra_specs.append(pl.BlockSpec(e.shape, lambda j, i: (0, 0)))
        else:
            extra_specs.append(pl.BlockSpec((tm, e.shape[1]), lambda j, i: (i, 0)))
    return pl.pallas_call(
        functools.partial(_proj_kernel, mode=mode, scale=scale),
        grid=(n_cols // tn, n // tm),
        in_specs=[pl.BlockSpec((tm, k), lambda j, i: (i, 0)),
                  pl.BlockSpec((k, tn), lambda j, i: (0, j0 + j))] + extra_specs,
        out_specs=pl.BlockSpec((tm, tn), lambda j, i: (i, j)),
        out_shape=jax.ShapeDtypeStruct((n, n_cols), out_dtype),
        scratch_shapes=[pltpu.VMEM((k, tn), BF16)],
        compiler_params=_params(("arbitrary", "arbitrary")),
        name="proj_" + mode,
    )(x, w, *extras)


def _out_res_kernel(y_ref, w_ref, x_ref, g_ref, o_ref, wb_ref):
    @pl.when(pl.program_id(1) == 0)
    def _():
        wb_ref[...] = w_ref[...].astype(BF16)

    acc = jnp.dot(y_ref[...], wb_ref[...], preferred_element_type=F32)
    o_ref[...] = x_ref[...] + g_ref[0] * acc


def _out_res(y, w, x, gate, rows_per_group, n_groups):
    n, k = y.shape
    d = w.shape[1]
    tm = MM_ROW_TILE
    tn = _col_tile(k)
    return pl.pallas_call(
        _out_res_kernel,
        grid=(d // tn, n // tm),
        in_specs=[pl.BlockSpec((tm, k), lambda j, i: (i, 0)),
                  pl.BlockSpec((k, tn), lambda j, i: (0, j)),
                  pl.BlockSpec((tm, tn), lambda j, i: (i, j)),
                  pl.BlockSpec((1, 1, tn), lambda j, i: (jnp.minimum(i * tm // rows_per_group, n_groups - 1), 0, j))],
        out_specs=pl.BlockSpec((tm, tn), lambda j, i: (i, j)),
        out_shape=jax.ShapeDtypeStruct((n, d), F32),
        scratch_shapes=[pltpu.VMEM((k, tn), BF16)],
        compiler_params=_params(("arbitrary", "arbitrary")),
        name="out_res",
    )(y, w, x, gate)


def _mla_q_kernel(z_ref, ga_ref, w_ref, g_ref, c_ref, s_ref, o_ref, *, scale):
    cq = z_ref[...]
    cn = (_rms(cq, cq.shape[-1]) * ga_ref[...]).astype(BF16)
    acc = jnp.dot(cn, w_ref[...], preferred_element_type=F32)
    g, c, s = g_ref[...], c_ref[...], s_ref[...]
    width = float(MLA_NOPE + MLA_ROPE)
    for j in range(acc.shape[1] // MLA_HEAD_PAD):
        qh = acc[:, j * MLA_HEAD_PAD:(j + 1) * MLA_HEAD_PAD]
        qn = _rms(qh, width) * g
        o_ref[:, j * MLA_HEAD_PAD:j * MLA_HEAD_PAD + LANES] = (qn[:, :LANES] * scale).astype(o_ref.dtype)
        o_ref[:, j * MLA_HEAD_PAD + LANES:(j + 1) * MLA_HEAD_PAD] = (
            _half_swap_rope(qn[:, LANES:], c, s) * scale).astype(o_ref.dtype)


def _mla_q(z, q_a_g, w_q_b_pad, q_g_pad, cos_t, sin_t, scale, tn=1024):
    n = z.shape[0]
    tm = MM_ROW_TILE
    n_out = w_q_b_pad.shape[1]
    return pl.pallas_call(
        functools.partial(_mla_q_kernel, scale=scale),
        grid=(n // tm, n_out // tn),
        in_specs=[pl.BlockSpec((tm, MLA_Q_LORA), lambda i, j: (i, 0)),
                  pl.BlockSpec((1, MLA_Q_LORA), lambda i, j: (0, 0)),
                  pl.BlockSpec((MLA_Q_LORA, tn), lambda i, j: (0, j)),
                  pl.BlockSpec((1, MLA_HEAD_PAD), lambda i, j: (0, 0)),
                  pl.BlockSpec((tm, LANES), lambda i, j: (i, 0)),
                  pl.BlockSpec((tm, LANES), lambda i, j: (i, 0))],
        out_specs=pl.BlockSpec((tm, tn), lambda i, j: (i, j)),
        out_shape=jax.ShapeDtypeStruct((n, n_out), BF16),
        compiler_params=_params(("parallel", "arbitrary")),
        name="mla_q",
    )(z, q_a_g.reshape(1, -1), w_q_b_pad, q_g_pad, cos_t, sin_t)


def _mla_kv_kernel(z_ref, kr_ref, ga_ref, w_ref, g_ref, c_ref, s_ref, k_ref, v_ref):
    ckv = z_ref[...]
    cn = (_rms(ckv, ckv.shape[-1]) * ga_ref[...]).astype(BF16)
    acc = jnp.dot(cn, w_ref[...], preferred_element_type=F32)
    kr = kr_ref[...]
    kr_ss = jnp.sum(kr * kr, axis=-1, keepdims=True)
    g, c, s = g_ref[...], c_ref[...], s_ref[...]
    width = float(MLA_NOPE + MLA_ROPE)
    for j in range(acc.shape[1] // MLA_HEAD_PAD):
        kn = acc[:, j * MLA_HEAD_PAD:j * MLA_HEAD_PAD + MLA_NOPE]
        vv = acc[:, j * MLA_HEAD_PAD + MLA_NOPE:(j + 1) * MLA_HEAD_PAD]
        f = lax.rsqrt((jnp.sum(kn * kn, axis=-1, keepdims=True) + kr_ss) / width + NORM_EPS)
        k_ref[:, j * MLA_HEAD_PAD:j * MLA_HEAD_PAD + LANES] = (kn * f * g[:, :LANES]).astype(k_ref.dtype)
        k_ref[:, j * MLA_HEAD_PAD + LANES:(j + 1) * MLA_HEAD_PAD] = _half_swap_rope(
            kr * f * g[:, LANES:], c, s).astype(k_ref.dtype)
        v_ref[:, j * MLA_HEAD_PAD:j * MLA_HEAD_PAD + MLA_V] = vv.astype(v_ref.dtype)
        v_ref[:, j * MLA_HEAD_PAD + MLA_V:(j + 1) * MLA_HEAD_PAD] = jnp.ones((vv.shape[0], LANES), v_ref.dtype)


def _mla_kv(z, kv_a_g, w_kv_b, k_g_pad, cos_t, sin_t, tn=1024):
    n = z.shape[0]
    tm = MM_ROW_TILE
    n_out = w_kv_b.shape[1]
    kr_block = (MLA_Q_LORA + MLA_KV_LORA) // LANES
    return pl.pallas_call(
        _mla_kv_kernel,
        grid=(n // tm, n_out // tn),
        in_specs=[pl.BlockSpec((tm, MLA_KV_LORA), lambda i, j: (i, 1)),
                  pl.BlockSpec((tm, LANES), lambda i, j: (i, kr_block)),
                  pl.BlockSpec((1, MLA_KV_LORA), lambda i, j: (0, 0)),
                  pl.BlockSpec((MLA_KV_LORA, tn), lambda i, j: (0, j)),
                  pl.BlockSpec((1, MLA_HEAD_PAD), lambda i, j: (0, 0)),
                  pl.BlockSpec((tm, LANES), lambda i, j: (i, 0)),
                  pl.BlockSpec((tm, LANES), lambda i, j: (i, 0))],
        out_specs=[pl.BlockSpec((tm, tn), lambda i, j: (i, j)),
                   pl.BlockSpec((tm, tn), lambda i, j: (i, j))],
        out_shape=[jax.ShapeDtypeStruct((n, n_out), BF16),
                   jax.ShapeDtypeStruct((n, n_out), BF16)],
        compiler_params=_params(("parallel", "arbitrary")),
        name="mla_kv",
    )(z, z, kv_a_g.reshape(1, -1), w_kv_b, k_g_pad, cos_t, sin_t)


def _softmax_pv(q, k_refs, v_refs, c0, dq, *, den_from_v):
    chunks = []
    for k_ref, v_ref in zip(k_refs, v_refs):
        step = min(k_ref.shape[0], KEY_CHUNK)
        chunks += [(k_ref, v_ref, r0, step) for r0 in range(0, k_ref.shape[0], step)]
    m = acc = den = None
    for k_ref, v_ref, r0, step in chunks:
        s = lax.dot_general(q, k_ref[r0:r0 + step, c0:c0 + dq], (((1,), (1,)), ((), ())),
                            preferred_element_type=F32)
        mc = jnp.max(s, axis=-1, keepdims=True)
        m_new = mc if m is None else jnp.maximum(m, mc)
        p = jnp.exp2((s - m_new).astype(BF16)) if den_from_v else jnp.exp2(s - m_new)
        pv = jnp.dot(p.astype(BF16), v_ref[r0:r0 + step, :], preferred_element_type=F32)
        if m is None:
            acc = pv
            if not den_from_v:
                den = jnp.sum(p, axis=-1, keepdims=True)
        else:
            alpha = jnp.exp2(m - m_new)
            acc = acc * alpha + pv
            if not den_from_v:
                den = den * alpha + jnp.sum(p, axis=-1, keepdims=True)
        m = m_new
    if den_from_v:
        dv = acc.shape[1] - LANES
        return acc[:, :dv] / acc[:, dv:dv + 1]
    return acc / den


def _attn_kernel(*refs, n_seg, dq):
    q_ref = refs[0]
    k_refs = refs[1:1 + n_seg]
    v_refs = refs[1 + n_seg:1 + 2 * n_seg]
    o_ref = refs[-1]
    o_ref[...] = _softmax_pv(q_ref[...], k_refs, v_refs, 0, dq, den_from_v=True).astype(o_ref.dtype)


def _diff_attn_kernel(*refs, n_seg, dq, lam_init):
    q_ref = refs[0]
    k_refs = refs[1:1 + n_seg]
    v_refs = refs[1 + n_seg:1 + 2 * n_seg]
    lam_ref, g_ref, o_ref = refs[-3], refs[-2], refs[-1]
    lf = lam_ref[...]
    lam = (jnp.exp(jnp.sum(lf[0:1] * lf[1:2], axis=-1, keepdims=True))
           - jnp.exp(jnp.sum(lf[2:3] * lf[3:4], axis=-1, keepdims=True)) + lam_init)
    q = q_ref[...]
    o1 = _softmax_pv(q[:, :dq], k_refs, v_refs, 0, dq, den_from_v=False)
    o2 = _softmax_pv(q[:, dq:], k_refs, v_refs, dq, dq, den_from_v=False)
    o = o1 - lam * o2
    o_ref[...] = (_rms(o, o.shape[-1]) * g_ref[...] * (1.0 - lam_init)).astype(o_ref.dtype)


def _attention(q, k, v, *, heads, q_width, v_width, dv, seq, n_ctx, batch, lat_queries, diff=None):
    lat_rows = batch * seq
    ctx_blk0 = lat_rows // n_ctx
    if lat_queries:
        tq = ATTN_Q_TILE
        nq = seq // tq
        q_map = lambda b, h, i: (b * nq + i, h)
        segs = [(seq, lambda b, h, i: (b, h)), (n_ctx, lambda b, h, i: (ctx_blk0 + b, h))]
        out_rows = lat_rows
    else:
        tq = n_ctx
        nq = 1
        q_map = lambda b, h, i: (ctx_blk0 + b, h)
        segs = [(n_ctx, lambda b, h, i: (ctx_blk0 + b, h))]
        out_rows = batch * n_ctx
    n_seg = len(segs)
    in_specs = [pl.BlockSpec((tq, q_width), q_map)]
    in_specs += [pl.BlockSpec((rows, q_width), m) for rows, m in segs]
    in_specs += [pl.BlockSpec((rows, v_width), m) for rows, m in segs]
    args = [q] + [k] * n_seg + [v] * n_seg
    if diff is None:
        kern = functools.partial(_attn_kernel, n_seg=n_seg, dq=q_width)
    else:
        lam, subln_g, lam_init = diff
        kern = functools.partial(_diff_attn_kernel, n_seg=n_seg, dq=q_width // 2, lam_init=lam_init)
        in_specs += [pl.BlockSpec(lam.shape, lambda b, h, i: (0, 0)), pl.BlockSpec((1, dv), lambda b, h, i: (0, 0))]
        args += [lam, subln_g.reshape(1, dv)]
    return pl.pallas_call(
        kern,
        grid=(batch, heads, nq),
        in_specs=in_specs,
        out_specs=pl.BlockSpec((tq, dv), lambda b, h, i: (b * nq + i, h)),
        out_shape=jax.ShapeDtypeStruct((out_rows, heads * dv), BF16),
        compiler_params=_params(("parallel", "parallel", "arbitrary")),
        name="attention" if diff is None else "diff_attention",
    )(*args)


def _ret_kernel(dec_ref, q_ref, k_ref, v_ref, o_ref, state_ref, *, backward):
    @pl.when(pl.program_id(2) == 0)
    def _():
        state_ref[...] = jnp.zeros_like(state_ref)

    ch = q_ref.shape[0]
    ii = lax.broadcasted_iota(jnp.int32, (ch, ch), 0).astype(F32)
    jj = lax.broadcasted_iota(jnp.int32, (ch, ch), 1).astype(F32)
    pos = lax.broadcasted_iota(jnp.int32, (ch, 1), 0).astype(F32)
    if backward:
        dist, valid = jnp.maximum(jj - ii, 0.0), jj > ii
        q_pow, k_pow = ch - pos, pos
    else:
        dist, valid = jnp.maximum(ii - jj, 0.0), ii >= jj
        q_pow, k_pow = pos + 1.0, ch - 1.0 - pos
    for h in range(state_ref.shape[0]):
        lg = jax.nn.log_sigmoid(dec_ref[h])[:, :1]
        intra = jnp.where(valid, jnp.exp(lg * dist), 0.0)
        q = q_ref[:, h * RET_DK:(h + 1) * RET_DK]
        k = k_ref[:, h * RET_DK:(h + 1) * RET_DK]
        v = v_ref[:, h * RET_DV:(h + 1) * RET_DV]
        scores = lax.dot_general(q, k, (((1,), (1,)), ((), ())), preferred_element_type=F32) * intra
        state = state_ref[h]
        o = jnp.dot(scores.astype(BF16), v, preferred_element_type=F32)
        o += jnp.dot((q.astype(F32) * jnp.exp(lg * q_pow)).astype(BF16), state.astype(BF16),
                     preferred_element_type=F32)
        o_ref[:, h * RET_DV:(h + 1) * RET_DV] = o.astype(o_ref.dtype)
        kd_t = (k.astype(F32) * jnp.exp(lg * k_pow)).T.astype(BF16)
        state_ref[h] = state * jnp.exp(lg * ch) + jnp.dot(kd_t, v, preferred_element_type=F32)


def _retention(dec, q, k, v, *, backward, seq, n_ctx, batch):
    ch = RET_CHUNK
    n_c, n_s = n_ctx // ch, seq // ch
    ctx_blk0 = batch * seq // ch

    def row_block(b, t):
        if backward:
            return jnp.where(t < n_c, ctx_blk0 + b * n_c + (n_c - 1 - t), b * n_s + (n_s - 1 - (t - n_c)))
        return jnp.where(t < n_c, ctx_blk0 + b * n_c + t, b * n_s + (t - n_c))

    hb = RET_HEAD_BLOCK
    spec = lambda width: pl.BlockSpec((ch, hb * width), lambda b, h, t: (row_block(b, t), h))
    return pl.pallas_call(
        functools.partial(_ret_kernel, backward=backward),
        grid=(batch, RET_HEADS // hb, n_c + n_s),
        in_specs=[pl.BlockSpec((hb, 1, LANES), lambda b, h, t: (h, 0, 0)), spec(RET_DK), spec(RET_DK), spec(RET_DV)],
        out_specs=spec(RET_DV),
        out_shape=jax.ShapeDtypeStruct((q.shape[0], RET_HEADS * RET_DV), BF16),
        scratch_shapes=[pltpu.VMEM((hb, RET_DK, RET_DV), F32)],
        compiler_params=_params(("parallel", "parallel", "arbitrary")),
        name="retention_bwd" if backward else "retention_fwd",
    )(dec, q, k, v)


def _ret_finish_kernel(of_ref, ob_ref, gate_ref, g_ref, o_ref):
    for h in range(RET_HEADS):
        cols = slice(h * RET_DV, (h + 1) * RET_DV)
        y = of_ref[:, cols].astype(F32) + ob_ref[:, cols].astype(F32)
        o_ref[:, cols] = (gate_ref[:, cols].astype(F32) * (_rms(y, RET_DV) * g_ref[:, cols])).astype(o_ref.dtype)


def _ret_finish(o_f, o_b, gate, norm_g):
    n, w = o_f.shape
    tm = ROW_TILE
    row = pl.BlockSpec((tm, w), lambda i: (i, 0))
    return pl.pallas_call(
        _ret_finish_kernel,
        grid=(n // tm,),
        in_specs=[row, row, row, pl.BlockSpec((1, w), lambda i: (0, 0))],
        out_specs=row,
        out_shape=jax.ShapeDtypeStruct((n, w), BF16),
        compiler_params=_params(("parallel",)),
        name="ret_finish",
    )(o_f, o_b, gate, norm_g.reshape(1, w))


def _moe_kernel(te_ref, nu_ref, first_ref, nxt_ref, x_ref, wg_hbm, wu_hbm, wd_hbm, o_ref,
                stage_g, stage_u, stage_d, cur_g, cur_u, cur_d, sem, *, layer):
    t = pl.program_id(0)

    def weight_copies(e):
        return (pltpu.make_async_copy(wg_hbm.at[layer, e], stage_g, sem.at[0]),
                pltpu.make_async_copy(wu_hbm.at[layer, e], stage_u, sem.at[1]),
                pltpu.make_async_copy(wd_hbm.at[layer, e], stage_d, sem.at[2]))

    @pl.when(t == 0)
    def _():
        for cp in weight_copies(te_ref[0]):
            cp.start()

    @pl.when((t < nu_ref[0]) & (first_ref[t] == 1))
    def _():
        for cp in weight_copies(te_ref[t]):
            cp.wait()
        for stage, cur in ((stage_g, cur_g), (stage_u, cur_u), (stage_d, cur_d)):
            rows = stage.shape[0] // WEIGHT_CAST_CHUNKS

            def cast(c, carry, stage=stage, cur=cur, rows=rows):
                r0 = pl.multiple_of(c * rows, rows)
                cur[pl.ds(r0, rows), :] = stage[pl.ds(r0, rows), :].astype(BF16)
                return carry

            lax.fori_loop(0, WEIGHT_CAST_CHUNKS, cast, 0)

        @pl.when(nxt_ref[t] >= 0)
        def _():
            for cp in weight_copies(nxt_ref[t]):
                cp.start()

    @pl.when(t < nu_ref[0])
    def _():
        x = x_ref[...].astype(BF16)
        a = jnp.dot(x, cur_g[...], preferred_element_type=F32)
        u = jnp.dot(x, cur_u[...], preferred_element_type=F32)
        act = (a * jax.nn.sigmoid(a) * u).astype(BF16)
        o_ref[...] = jnp.dot(act, cur_d[...], preferred_element_type=F32).astype(o_ref.dtype)

    @pl.when(t >= nu_ref[0])
    def _():
        o_ref[...] = jnp.zeros_like(o_ref)


def _moe_ffn(tile_expert, n_used, first, nxt, x_sorted, wg, wu, wd, layer):
    r, d = x_sorted.shape
    f = wg.shape[3]
    tm = MOE_ROW_TILE
    any_spec = pl.BlockSpec(memory_space=pl.ANY)
    return pl.pallas_call(
        functools.partial(_moe_kernel, layer=layer),
        grid_spec=pltpu.PrefetchScalarGridSpec(
            num_scalar_prefetch=4,
            grid=(r // tm,),
            in_specs=[pl.BlockSpec((tm, d), lambda t, *_: (t, 0)), any_spec, any_spec, any_spec],
            out_specs=pl.BlockSpec((tm, d), lambda t, *_: (t, 0)),
            scratch_shapes=[pltpu.VMEM((d, f), F32), pltpu.VMEM((d, f), F32), pltpu.VMEM((f, d), F32),
                            pltpu.VMEM((d, f), BF16), pltpu.VMEM((d, f), BF16), pltpu.VMEM((f, d), BF16),
                            pltpu.SemaphoreType.DMA((3,))],
        ),
        out_shape=jax.ShapeDtypeStruct((r, d), F32),
        compiler_params=_params(("arbitrary",)),
        name="moe_ffn",
    )(tile_expert, n_used, first, nxt, x_sorted, wg, wu, wd)


def _combine_kernel(x_ref, a_ref, b_ref, w_ref, g_ref, o_ref):
    w = w_ref[...]
    o_ref[...] = x_ref[...] + g_ref[0] * (w[:, 0:1] * a_ref[...] + w[:, 1:2] * b_ref[...])


def _combine(x, ab, w_col, gate, rows_per_group, n_groups):
    n, d = x.shape
    tm = ROW_TILE
    row = pl.BlockSpec((tm, d), lambda i: (i, 0))
    return pl.pallas_call(
        _combine_kernel,
        grid=(n // tm,),
        in_specs=[row, row, pl.BlockSpec((tm, d), lambda i: (i + n // tm, 0)),
                  pl.BlockSpec((tm, LANES), lambda i: (i, 0)), _group_spec(d, tm, rows_per_group, n_groups)],
        out_specs=row,
        out_shape=jax.ShapeDtypeStruct((n, d), F32),
        compiler_params=_params(("parallel",)),
        name="moe_combine",
    )(x, ab, ab, w_col, gate)


def _moe_layer(x, g, sh, sc, gate, rw_pad, rb_col, wg, wu, wd, layer, rows_per_group, n_groups):
    n, d = x.shape
    tm = MOE_ROW_TILE
    h, idx8, w_col, cnt = _norm_mod_route(x, g, sh, sc, rw_pad, rb_col, rows_per_group, n_groups)
    experts = jnp.arange(N_EXPERTS, dtype=jnp.int32)
    padded = (cnt[:, 0] + tm - 1) // tm * tm
    ends = jnp.cumsum(padded)
    starts = ends - padded
    e2, rank2 = idx8[0:2], idx8[2:4]
    dest = rank2 + jnp.sum(jnp.where(e2[..., None] == experts, starts, 0), axis=-1)
    r = (2 * n + N_EXPERTS * (tm - 1)) // tm * tm
    tok = jnp.tile(jnp.arange(n, dtype=jnp.int32), 2)
    src = (jnp.arange(r, dtype=jnp.int32) % n).at[dest.reshape(-1)].set(tok, unique_indices=True,
                                                                         mode="promise_in_bounds")
    tile_start = jnp.arange(r // tm, dtype=jnp.int32) * tm
    tile_expert = jnp.minimum(jnp.sum((ends[None, :] <= tile_start[:, None]).astype(jnp.int32), axis=1),
                              N_EXPERTS - 1)
    n_used = ends[-1:] // tm
    first = jnp.concatenate([jnp.ones((1,), jnp.int32), (tile_expert[1:] != tile_expert[:-1]).astype(jnp.int32)])
    later = (padded[None, :] > 0) & (experts[None, :] > experts[:, None])
    next_of = jnp.min(jnp.where(later, experts[None, :], N_EXPERTS), axis=1)
    next_of = jnp.where(next_of < N_EXPERTS, next_of, -1)
    nxt = jnp.sum(jnp.where(tile_expert[:, None] == experts, next_of, 0), axis=1)
    x_sorted = h.at[src].get(mode="promise_in_bounds")
    y_sorted = _moe_ffn(tile_expert, n_used, first, nxt, x_sorted, wg, wu, wd, layer)
    ab = y_sorted.at[dest.reshape(-1)].get(mode="promise_in_bounds")
    return _combine(x, ab, w_col, gate, rows_per_group, n_groups)


def _axial_tables(rows, rot_dim):
    n_freq = rot_dim // 4
    inv_freq = jnp.power(ROPE_BASE, -jnp.arange(n_freq, dtype=F32) / n_freq)
    row = jnp.repeat(jnp.arange(rows, dtype=F32), GRID_W)
    col = jnp.tile(jnp.arange(GRID_W, dtype=F32), rows)
    ang = jnp.concatenate([row[:, None] * inv_freq, col[:, None] * inv_freq], axis=-1)
    return jnp.cos(ang), jnp.sin(ang)


def _flat_tables(c_lat, s_lat, batch, n_ctx_rows):
    c = jnp.concatenate([jnp.tile(c_lat, (batch, 1)), jnp.broadcast_to(c_lat[:1], (n_ctx_rows, LANES))])
    s = jnp.concatenate([jnp.tile(s_lat, (batch, 1)), jnp.zeros((n_ctx_rows, LANES), F32)])
    return c, s


def _spread_pairs(a, axis):
    x1, x2 = jnp.split(a, 2, axis=axis)
    z = jnp.zeros_like(x1)
    return jnp.concatenate([x1, z, x2, z], axis=axis)


def kernel(x, c, ctx, c_ctx, ada_w, ada_b, norm_mix_g, norm_ffn_g, mla_w_in, mla_q_a_g, mla_w_q_b, mla_kv_a_g,
           mla_w_kv_b, mla_q_norm_g, mla_k_norm_g, mla_w_o, diff_w_in, diff_q_norm_g, diff_k_norm_g, diff_lambda,
           diff_subln_g, diff_w_o, ret_w_in, ret_decay_fwd, ret_decay_bwd, ret_norm_g, ret_w_o, router_w,
           router_bias, moe_w_gate, moe_w_up, moe_w_down):
    batch, seq, d = x.shape
    n_ctx = ctx.shape[1]
    depth = ada_w.shape[0]
    n_lat, n_cx = batch * seq, batch * n_ctx
    n_groups = batch + 1
    assert n_groups <= SUBLANES and seq % MM_ROW_TILE == 0 and n_cx % MM_ROW_TILE == 0
    assert seq % RET_CHUNK == 0 and n_ctx % RET_CHUNK == 0 and n_lat % n_ctx == 0 and n_ctx == ROW_TILE

    xa = jnp.concatenate([x.reshape(n_lat, d), ctx.reshape(n_cx, d)], axis=0)

    cond = jnp.concatenate([c, c_ctx[None, :], jnp.zeros((SUBLANES - n_groups, d), F32)], axis=0)
    mods = _ada_mods(cond.T, ada_w, ada_b, n_groups)

    rows = seq // GRID_W
    cos64, sin64 = _axial_tables(rows, MLA_ROPE)
    zeros32 = jnp.zeros_like(cos64)
    mla_c, mla_s = _flat_tables(jnp.concatenate([cos64, zeros32, cos64, zeros32], axis=1),
                                jnp.concatenate([-sin64, zeros32, sin64, zeros32], axis=1), batch, n_cx)
    cos128, sin128 = _axial_tables(rows, DIFF_HEAD_DIM)
    diff_c, diff_s = _flat_tables(jnp.concatenate([cos128, cos128], axis=1),
                                  jnp.concatenate([-sin128, sin128], axis=1), batch, n_cx)
    cos256, sin256 = _axial_tables(rows, RET_DK)
    ret_c, ret_s = _flat_tables(cos256, sin256, batch, n_cx)

    rw_pad = jnp.concatenate([router_w, jnp.zeros((d, LANES - N_EXPERTS), F32)], axis=1)
    rb_col = router_bias.reshape(N_EXPERTS, 1)

    for i in range(depth):
        kind, j, last = i % N_MIXERS, i // N_MIXERS, i == depth - 1
        m = mods[i].reshape(SUBLANES, 6, 1, d)
        sh_m, sc_m, g_m, sh_f, sc_f, g_f = (m[:, t] for t in range(6))
        n_all = xa.shape[0]
        h = _norm_mod(xa, norm_mix_g[i], sh_m, sc_m, seq, n_groups)
        q_rows = n_lat if last else n_all

        if kind == 0:
            w_in = jnp.concatenate([mla_w_in[j][:, :MLA_Q_LORA + MLA_KV_LORA],
                                    _spread_pairs(mla_w_in[j][:, MLA_Q_LORA + MLA_KV_LORA:], 1)], axis=1)
            z = _proj(h, w_in, 0, w_in.shape[1], out_dtype=F32, tn=w_in.shape[1])
            wq = mla_w_q_b[j].reshape(MLA_Q_LORA, MLA_HEADS, MLA_NOPE + MLA_ROPE)
            wq = jnp.concatenate([wq[..., :MLA_NOPE], _spread_pairs(wq[..., MLA_NOPE:], 2)], axis=2)
            wq = wq.reshape(MLA_Q_LORA, MLA_HEADS * MLA_HEAD_PAD).astype(BF16)
            spread_g = lambda g: jnp.concatenate([g[:MLA_NOPE], _spread_pairs(g[MLA_NOPE:], 0)]).reshape(1, -1)
            scale = float(MLA_NOPE + MLA_ROPE) ** -0.5 * LOG2E
            q = _mla_q(z[:q_rows], mla_q_a_g[j], wq, spread_g(mla_q_norm_g[j]), mla_c[:q_rows], mla_s[:q_rows], scale)
            k, v = _mla_kv(z, mla_kv_a_g[j], mla_w_kv_b[j].astype(BF16), spread_g(mla_k_norm_g[j]), mla_c, mla_s)
            attn = functools.partial(_attention, q, k, v, heads=MLA_HEADS, q_width=MLA_HEAD_PAD,
                                     v_width=MLA_V + LANES, dv=MLA_V, seq=seq, n_ctx=n_ctx, batch=batch)
            w_o = mla_w_o[j]
        elif kind == 1:
            w_in = diff_w_in[j]
            hd = DIFF_HEADS * 2 * DIFF_HEAD_DIM
            scale = float(DIFF_HEAD_DIM) ** -0.5 * LOG2E
            q = _proj(h[:q_rows], w_in, 0, hd, mode="norm_rope", scale=scale,
                      extras=(diff_q_norm_g[j].reshape(1, -1), diff_c[:q_rows], diff_s[:q_rows]))
            k = _proj(h, w_in, hd, hd, mode="norm_rope", extras=(diff_k_norm_g[j].reshape(1, -1), diff_c, diff_s))
            v = _proj(h, w_in, 2 * hd, hd)
            lam_init = 0.8 - 0.6 * math.exp(-0.3 * i)
            attn = functools.partial(_attention, q, k, v, heads=DIFF_HEADS, q_width=2 * DIFF_HEAD_DIM,
                                     v_width=2 * DIFF_HEAD_DIM, dv=2 * DIFF_HEAD_DIM, seq=seq, n_ctx=n_ctx,
                                     batch=batch,
                                     diff=(diff_lambda[j], diff_subln_g[j], lam_init))
            w_o = diff_w_o[j]
        else:
            w_in = ret_w_in[j]
            nq, nv = RET_HEADS * RET_DK, RET_HEADS * RET_DV
            q = _proj(h, w_in, 0, nq, mode="rope256", extras=(ret_c, ret_s))
            k = _proj(h, w_in, nq, nq, mode="rope256", scale=float(RET_DK) ** -0.5, extras=(ret_c, ret_s))
            v = _proj(h, w_in, 2 * nq, nv)
            gate = _proj(h, w_in, 2 * nq + nv, nv, mode="silu")
            dec = lambda p: jnp.broadcast_to(p.astype(F32)[:, None, None], (RET_HEADS, 1, LANES))
            o_f = _retention(dec(ret_decay_fwd[j]), q, k, v, backward=False, seq=seq, n_ctx=n_ctx, batch=batch)
            o_b = _retention(dec(ret_decay_bwd[j]), q, k, v, backward=True, seq=seq, n_ctx=n_ctx, batch=batch)
            y = _ret_finish(o_f, o_b, gate, ret_norm_g[j])
            attn = None
            w_o = ret_w_o[j]

        if attn is not None:
            y = attn(lat_queries=True)
            if not last:
                y = jnp.concatenate([y, attn(lat_queries=False)], axis=0)
        if last:
            xa = xa[:n_lat]
            y = y[:n_lat]
        xa = _out_res(y, w_o, xa, g_m, seq, n_groups)
        xa = _moe_layer(xa, norm_ffn_g[i], sh_f, sc_f, g_f, rw_pad, rb_col, moe_w_gate, moe_w_up, moe_w_down, i,
                        seq, n_groups)
    return xa[:n_lat].reshape(batch, seq, d)
```

```python
import functools
import math

import jax
import jax.numpy as jnp
from jax import lax
from jax.experimental import pallas as pl
from jax.experimental.pallas import tpu as pltpu

F32 = jnp.float32
BF16 = jnp.bfloat16

GRID_W = 64
ROPE_BASE = 10000.0
NORM_EPS = 1e-6
N_MIXERS = 3

MLA_HEADS = 16
MLA_Q_LORA = 512
MLA_KV_LORA = 512
MLA_NOPE = 128
MLA_ROPE = 64
MLA_V = 128
MLA_HEAD_PAD = 256

DIFF_HEADS = 8
DIFF_HEAD_DIM = 128

RET_HEADS = 8
RET_DK = 256
RET_DV = 512
RET_CHUNK = 256

N_EXPERTS = 16
N_GROUPS = 4
EXPERTS_PER_GROUP = 4
D_EXPERT = 1024

LANES = 128
SUBLANES = 8
VMEM_LIMIT = 56 * 1024 * 1024

ROW_TILE = 256
MM_ROW_TILE = 512
MM_COL_TILE = 1024
MM_COL_TILE_MAX_K = 2048
RET_HEAD_BLOCK = 4
WEIGHT_CAST_CHUNKS = 8
MOE_ROW_TILE = 256
KEY_CHUNK = 512
ATTN_Q_TILE = 512
LOG2E = 1.4426950408889634
ADA_COL_TILE = 1024


def _params(sem):
    return pltpu.CompilerParams(dimension_semantics=sem, vmem_limit_bytes=VMEM_LIMIT)


def _rms(x, width):
    return x * lax.rsqrt(jnp.sum(x * x, axis=-1, keepdims=True) / width + NORM_EPS)


def _ada_kernel(ct_ref, w_ref, b_ref, o_ref, sb_ref, *, n_cond):
    d = ct_ref.shape[0]
    tn = o_ref.shape[-1]

    @pl.when((pl.program_id(0) == 0) & (pl.program_id(1) == 0))
    def _():
        ct = ct_ref[...]
        s = ct * jax.nn.sigmoid(ct)
        for r in range(n_cond):
            sb_ref[r] = jnp.broadcast_to(s[:, r:r + 1], (d, LANES))

    def body(kc, accs):
        k0 = pl.multiple_of(kc * SUBLANES, SUBLANES)
        w8 = w_ref[0, pl.ds(k0, SUBLANES), :]
        out = []
        for r in range(n_cond):
            s8 = sb_ref[r, pl.ds(k0, SUBLANES), :]
            out.append(accs[r] + w8 * jnp.concatenate([s8] * (tn // LANES), axis=1))
        return tuple(out)

    accs = lax.fori_loop(0, d // SUBLANES, body,
                         tuple(jnp.zeros((SUBLANES, tn), F32) for _ in range(n_cond)), unroll=4)
    rows = [jnp.sum(a, axis=0, keepdims=True) + b_ref[0] for a in accs]
    rows.append(jnp.zeros((SUBLANES - n_cond, tn), F32))
    o_ref[0] = jnp.concatenate(rows, axis=0)


def _ada_mods(cond_t, ada_w, ada_b, n_cond):
    depth, d, n6 = ada_w.shape
    tn = ADA_COL_TILE
    return pl.pallas_call(
        functools.partial(_ada_kernel, n_cond=n_cond),
        grid=(depth, n6 // tn),
        in_specs=[pl.BlockSpec((d, SUBLANES), lambda l, j: (0, 0)),
                  pl.BlockSpec((1, d, tn), lambda l, j: (l, 0, j)),
                  pl.BlockSpec((1, 1, tn), lambda l, j: (l, 0, j))],
        out_specs=pl.BlockSpec((1, SUBLANES, tn), lambda l, j: (l, 0, j)),
        out_shape=jax.ShapeDtypeStruct((depth, SUBLANES, n6), F32),
        scratch_shapes=[pltpu.VMEM((n_cond, d, LANES), F32)],
        compiler_params=_params(("arbitrary", "arbitrary")),
        name="ada_mods",
    )(cond_t, ada_w, ada_b.reshape(depth, 1, n6))


def _group_spec(d, tm, rows_per_group, n_groups):
    return pl.BlockSpec((1, 1, d), lambda i, *_: (jnp.minimum(i * tm // rows_per_group, n_groups - 1), 0, 0))


def _norm_mod_kernel(x_ref, g_ref, sh_ref, sc_ref, o_ref):
    x = x_ref[...]
    h = _rms(x, x.shape[-1]) * g_ref[...] * (1.0 + sc_ref[0]) + sh_ref[0]
    o_ref[...] = h.astype(o_ref.dtype)


def _norm_mod(x, g, sh, sc, rows_per_group, n_groups):
    n, d = x.shape
    tm = ROW_TILE
    gs = _group_spec(d, tm, rows_per_group, n_groups)
    return pl.pallas_call(
        _norm_mod_kernel,
        grid=(n // tm,),
        in_specs=[pl.BlockSpec((tm, d), lambda i: (i, 0)), pl.BlockSpec((1, d), lambda i: (0, 0)), gs, gs],
        out_specs=pl.BlockSpec((tm, d), lambda i: (i, 0)),
        out_shape=jax.ShapeDtypeStruct((n, d), BF16),
        compiler_params=_params(("parallel",)),
        name="norm_mod",
    )(x, g.reshape(1, d), sh, sc)


def _route_kernel(x_ref, g_ref, sh_ref, sc_ref, rw_ref, rb_ref, h_ref, idx_ref, wt_ref, cnt_ref, carry_ref,
                  rwp_ref):
    @pl.when(pl.program_id(0) == 0)
    def _():
        carry_ref[...] = jnp.zeros_like(carry_ref)
        w = rw_ref[...]
        hi = w.astype(BF16)
        lo = (w - hi.astype(F32)).astype(BF16)
        rwp_ref[...] = jnp.where(lax.broadcasted_iota(jnp.int32, w.shape, 1) < N_EXPERTS, hi, lo)

    x = x_ref[...]
    tm = x.shape[0]
    h = _rms(x, x.shape[-1]) * g_ref[...] * (1.0 + sc_ref[0]) + sh_ref[0]
    h_ref[...] = h.astype(h_ref.dtype)
    h_hi = h.astype(BF16)
    h_lo = (h - h_hi.astype(F32)).astype(BF16)
    parts = (jnp.dot(h_hi, rwp_ref[...], preferred_element_type=F32)
             + jnp.dot(h_lo, rwp_ref[...], preferred_element_type=F32)).T
    logits = parts[:N_EXPERTS] + parts[N_EXPERTS:2 * N_EXPERTS]
    scores = jax.nn.sigmoid(logits)
    sel = scores + rb_ref[...]
    sel_r = [sel[e:e + 1, :] for e in range(N_EXPERTS)]
    sc_r = [scores[e:e + 1, :] for e in range(N_EXPERTS)]

    gscore = []
    for g in range(N_GROUPS):
        a, b, c, d = sel_r[4 * g:4 * g + 4]
        hi1, lo1, hi2, lo2 = jnp.maximum(a, b), jnp.minimum(a, b), jnp.maximum(c, d), jnp.minimum(c, d)
        gscore.append(jnp.maximum(hi1, hi2) + jnp.maximum(jnp.minimum(hi1, hi2), jnp.maximum(lo1, lo2)))
    grp = jnp.zeros((1, tm), jnp.int32)
    best = gscore[0]
    for g in range(1, N_GROUPS):
        better = gscore[g] > best
        grp = jnp.where(better, g, grp)
        best = jnp.where(better, gscore[g], best)

    def pick(rows, l):
        out = rows[l]
        for g in range(1, N_GROUPS):
            out = jnp.where(grp == g, rows[4 * g + l], out)
        return out

    v = [pick(sel_r, l) for l in range(EXPERTS_PER_GROUP)]
    s = [pick(sc_r, l) for l in range(EXPERTS_PER_GROUP)]

    def first_max(vals):
        m = jnp.maximum(jnp.maximum(vals[0], vals[1]), jnp.maximum(vals[2], vals[3]))
        l = jnp.where(vals[0] == m, 0, jnp.where(vals[1] == m, 1, jnp.where(vals[2] == m, 2, 3)))
        return l

    def at(vals, l):
        return jnp.where(l == 0, vals[0], jnp.where(l == 1, vals[1], jnp.where(l == 2, vals[2], vals[3])))

    l1 = first_max(v)
    l2 = first_max([jnp.where(l1 == l, -jnp.inf, v[l]) for l in range(EXPERTS_PER_GROUP)])
    s1, s2 = at(s, l1), at(s, l2)
    tot = s1 + s2
    e1, e2 = grp * EXPERTS_PER_GROUP + l1, grp * EXPERTS_PER_GROUP + l2

    eids = lax.broadcasted_iota(jnp.int32, (N_EXPERTS, tm), 0)
    oh1, oh2 = (eids == e1).astype(F32), (eids == e2).astype(F32)
    before = (lax.broadcasted_iota(jnp.int32, (tm, tm), 0) < lax.broadcasted_iota(jnp.int32, (tm, tm), 1)).astype(BF16)
    carry = carry_ref[:, :1]
    tot1 = jnp.sum(oh1, axis=1, keepdims=True)
    pre1 = carry + jnp.dot(oh1.astype(BF16), before, preferred_element_type=F32)
    pre2 = carry + tot1 + jnp.dot(oh2.astype(BF16), before, preferred_element_type=F32)
    r1 = jnp.sum(oh1 * pre1, axis=0, keepdims=True).astype(jnp.int32)
    r2 = jnp.sum(oh2 * pre2, axis=0, keepdims=True).astype(jnp.int32)
    carry_new = carry + tot1 + jnp.sum(oh2, axis=1, keepdims=True)
    carry_ref[...] = jnp.broadcast_to(carry_new, carry_ref.shape)
    cnt_ref[...] = jnp.broadcast_to(carry_new, cnt_ref.shape).astype(jnp.int32)

    idx_ref[...] = jnp.concatenate([e1, e2, r1, r2, jnp.zeros((SUBLANES - 4, tm), jnp.int32)], axis=0)
    wt_ref[...] = jnp.concatenate([s1 / tot, s2 / tot, jnp.zeros((LANES - 2, tm), F32)], axis=0).T


def _norm_mod_route(x, g, sh, sc, rw_pad, rb_col, rows_per_group, n_groups):
    n, d = x.shape
    tm = ROW_TILE
    gs = _group_spec(d, tm, rows_per_group, n_groups)
    return pl.pallas_call(
        _route_kernel,
        grid=(n // tm,),
        in_specs=[pl.BlockSpec((tm, d), lambda i: (i, 0)), pl.BlockSpec((1, d), lambda i: (0, 0)), gs, gs,
                  pl.BlockSpec((d, LANES), lambda i: (0, 0)), pl.BlockSpec((N_EXPERTS, 1), lambda i: (0, 0))],
        out_specs=[pl.BlockSpec((tm, d), lambda i: (i, 0)),
                   pl.BlockSpec((SUBLANES, tm), lambda i: (0, i)),
                   pl.BlockSpec((tm, LANES), lambda i: (i, 0)),
                   pl.BlockSpec((N_EXPERTS, LANES), lambda i: (0, 0))],
        out_shape=[jax.ShapeDtypeStruct((n, d), F32),
                   jax.ShapeDtypeStruct((SUBLANES, n), jnp.int32),
                   jax.ShapeDtypeStruct((n, LANES), F32),
                   jax.ShapeDtypeStruct((N_EXPERTS, LANES), jnp.int32)],
        scratch_shapes=[pltpu.VMEM((N_EXPERTS, LANES), F32), pltpu.VMEM((d, LANES), BF16)],
        compiler_params=_params(("arbitrary",)),
        name="norm_mod_route",
    )(x, g.reshape(1, d), sh, sc, rw_pad, rb_col)


def _half_swap_rope(x, c, s):
    return x * c + pltpu.roll(x, LANES // 2, 1) * s


def _fold_rope(c_t, s_t, g, scale):
    return c_t * (g * scale)[None, :], s_t * (jnp.roll(g, LANES // 2) * scale)[None, :]


def _proj_kernel(*refs, mode):
    x_ref, w_ref = refs[0], refs[1]
    o_ref, wb_ref = refs[-2], refs[-1]

    @pl.when(pl.program_id(1) == 0)
    def _():
        wb_ref[...] = w_ref[...].astype(BF16)

    acc = jnp.dot(x_ref[...], wb_ref[...], preferred_element_type=F32)
    tn = acc.shape[1]
    if mode == "plain":
        o_ref[...] = acc.astype(o_ref.dtype)
    elif mode == "silu":
        o_ref[...] = (acc * jax.nn.sigmoid(acc)).astype(o_ref.dtype)
    elif mode == "norm_rope":
        gc, gs = refs[2][...], refs[3][...]
        for j in range(tn // LANES):
            blk = _rms(acc[:, j * LANES:(j + 1) * LANES], LANES)
            o_ref[:, j * LANES:(j + 1) * LANES] = _half_swap_rope(blk, gc, gs).astype(o_ref.dtype)
    elif mode == "rope256":
        c, s = refs[2][...], refs[3][...]
        for j in range(tn // (2 * LANES)):
            x1 = acc[:, (2 * j) * LANES:(2 * j + 1) * LANES]
            x2 = acc[:, (2 * j + 1) * LANES:(2 * j + 2) * LANES]
            o_ref[:, (2 * j) * LANES:(2 * j + 1) * LANES] = (x1 * c - x2 * s).astype(o_ref.dtype)
            o_ref[:, (2 * j + 1) * LANES:(2 * j + 2) * LANES] = (x2 * c + x1 * s).astype(o_ref.dtype)
    else:
        raise ValueError(mode)


def _col_tile(k):
    return MM_COL_TILE if k <= MM_COL_TILE_MAX_K else MM_COL_TILE // 2


def _proj(x, w, col0, n_cols, *, mode="plain", extras=(), out_dtype=BF16, tn=None):
    n, k = x.shape
    tm = MM_ROW_TILE
    tn = min(_col_tile(k) if tn is None else tn, n_cols)
    j0 = col0 // tn
    extra_specs = []
    for e in extras:
        if e.shape[0] == 1:
            extra_specs.append(pl.BlockSpec(e.shape, lambda j, i: (0, 0)))
        else:
            extra_specs.append(pl.BlockSpec((tm, e.shape[1]), lambda j, i: (i, 0)))
    return pl.pallas_call(
        functools.partial(_proj_kernel, mode=mode),
        grid=(n_cols // tn, n // tm),
        in_specs=[pl.BlockSpec((tm, k), lambda j, i: (i, 0)),
                  pl.BlockSpec((k, tn), lambda j, i: (0, j0 + j))] + extra_specs,
        out_specs=pl.BlockSpec((tm, tn), lambda j, i: (i, j)),
        out_shape=jax.ShapeDtypeStruct((n, n_cols), out_dtype),
        scratch_shapes=[pltpu.VMEM((k, tn), BF16)],
        compiler_params=_params(("arbitrary", "arbitrary")),
        name="proj_" + mode,
    )(x, w, *extras)


def _out_res_kernel(y_ref, w_ref, x_ref, g_ref, o_ref, wb_ref):
    @pl.when(pl.program_id(1) == 0)
    def _():
        wb_ref[...] = w_ref[...].astype(BF16)

    acc = jnp.dot(y_ref[...], wb_ref[...], preferred_element_type=F32)
    o_ref[...] = x_ref[...] + g_ref[0] * acc


def _out_res(y, w, x, gate, rows_per_group, n_groups, rows):
    n, k = rows, y.shape[1]
    d = w.shape[1]
    tm = MM_ROW_TILE
    tn = _col_tile(k)
    return pl.pallas_call(
        _out_res_kernel,
        grid=(d // tn, n // tm),
        in_specs=[pl.BlockSpec((tm, k), lambda j, i: (i, 0)),
                  pl.BlockSpec((k, tn), lambda j, i: (0, j)),
                  pl.BlockSpec((tm, tn), lambda j, i: (i, j)),
                  pl.BlockSpec((1, 1, tn), lambda j, i: (jnp.minimum(i * tm // rows_per_group, n_groups - 1), 0, j))],
        out_specs=pl.BlockSpec((tm, tn), lambda j, i: (i, j)),
        out_shape=jax.ShapeDtypeStruct((n, d), F32),
        scratch_shapes=[pltpu.VMEM((k, tn), BF16)],
        compiler_params=_params(("arbitrary", "arbitrary")),
        name="out_res",
    )(y, w, x, gate)


def _mla_q_kernel(z_ref, ga_ref, w_ref, g_ref, gc_ref, gs_ref, o_ref):
    cq = z_ref[...]
    cn = (_rms(cq, cq.shape[-1]) * ga_ref[...]).astype(BF16)
    acc = jnp.dot(cn, w_ref[...], preferred_element_type=F32)
    g, gc, gs = g_ref[...], gc_ref[...], gs_ref[...]
    width = float(MLA_NOPE + MLA_ROPE)
    for j in range(acc.shape[1] // MLA_HEAD_PAD):
        qn = _rms(acc[:, j * MLA_HEAD_PAD:(j + 1) * MLA_HEAD_PAD], width)
        o_ref[:, j * MLA_HEAD_PAD:j * MLA_HEAD_PAD + LANES] = (qn[:, :LANES] * g).astype(o_ref.dtype)
        o_ref[:, j * MLA_HEAD_PAD + LANES:(j + 1) * MLA_HEAD_PAD] = _half_swap_rope(
            qn[:, LANES:], gc, gs).astype(o_ref.dtype)


def _mla_q(z, q_a_g, w_q_b_pad, g_nope, gc, gs, rows, tn=1024):
    n = rows
    tm = MM_ROW_TILE
    n_out = w_q_b_pad.shape[1]
    return pl.pallas_call(
        _mla_q_kernel,
        grid=(n // tm, n_out // tn),
        in_specs=[pl.BlockSpec((tm, MLA_Q_LORA), lambda i, j: (i, 0)),
                  pl.BlockSpec((1, MLA_Q_LORA), lambda i, j: (0, 0)),
                  pl.BlockSpec((MLA_Q_LORA, tn), lambda i, j: (0, j)),
                  pl.BlockSpec((1, LANES), lambda i, j: (0, 0)),
                  pl.BlockSpec((tm, LANES), lambda i, j: (i, 0)),
                  pl.BlockSpec((tm, LANES), lambda i, j: (i, 0))],
        out_specs=pl.BlockSpec((tm, tn), lambda i, j: (i, j)),
        out_shape=jax.ShapeDtypeStruct((n, n_out), BF16),
        compiler_params=_params(("parallel", "arbitrary")),
        name="mla_q",
    )(z, q_a_g.reshape(1, -1), w_q_b_pad, g_nope.reshape(1, LANES), gc, gs)


def _mla_kv_kernel(z_ref, kr_ref, ga_ref, w_ref, g_ref, gc_ref, gs_ref, k_ref, v_ref):
    ckv = z_ref[...]
    cn = (_rms(ckv, ckv.shape[-1]) * ga_ref[...]).astype(BF16)
    acc = jnp.dot(cn, w_ref[...], preferred_element_type=F32)
    kr = kr_ref[...]
    kr_ss = jnp.sum(kr * kr, axis=-1, keepdims=True)
    g, gc, gs = g_ref[...], gc_ref[...], gs_ref[...]
    width = float(MLA_NOPE + MLA_ROPE)
    for j in range(acc.shape[1] // MLA_HEAD_PAD):
        kn = acc[:, j * MLA_HEAD_PAD:j * MLA_HEAD_PAD + MLA_NOPE]
        vv = acc[:, j * MLA_HEAD_PAD + MLA_NOPE:(j + 1) * MLA_HEAD_PAD]
        f = lax.rsqrt((jnp.sum(kn * kn, axis=-1, keepdims=True) + kr_ss) / width + NORM_EPS)
        k_ref[:, j * MLA_HEAD_PAD:j * MLA_HEAD_PAD + LANES] = (kn * f * g).astype(k_ref.dtype)
        k_ref[:, j * MLA_HEAD_PAD + LANES:(j + 1) * MLA_HEAD_PAD] = _half_swap_rope(
            kr * f, gc, gs).astype(k_ref.dtype)
        v_ref[:, j * MLA_HEAD_PAD:j * MLA_HEAD_PAD + MLA_V] = vv.astype(v_ref.dtype)
        v_ref[:, j * MLA_HEAD_PAD + MLA_V:(j + 1) * MLA_HEAD_PAD] = jnp.ones((vv.shape[0], LANES), v_ref.dtype)


def _mla_kv(z, kv_a_g, w_kv_b, g_nope, gc, gs, tn=1024):
    n = z.shape[0]
    tm = MM_ROW_TILE
    n_out = w_kv_b.shape[1]
    kr_block = (MLA_Q_LORA + MLA_KV_LORA) // LANES
    return pl.pallas_call(
        _mla_kv_kernel,
        grid=(n // tm, n_out // tn),
        in_specs=[pl.BlockSpec((tm, MLA_KV_LORA), lambda i, j: (i, 1)),
                  pl.BlockSpec((tm, LANES), lambda i, j: (i, kr_block)),
                  pl.BlockSpec((1, MLA_KV_LORA), lambda i, j: (0, 0)),
                  pl.BlockSpec((MLA_KV_LORA, tn), lambda i, j: (0, j)),
                  pl.BlockSpec((1, LANES), lambda i, j: (0, 0)),
                  pl.BlockSpec((tm, LANES), lambda i, j: (i, 0)),
                  pl.BlockSpec((tm, LANES), lambda i, j: (i, 0))],
        out_specs=[pl.BlockSpec((tm, tn), lambda i, j: (i, j)),
                   pl.BlockSpec((tm, tn), lambda i, j: (i, j))],
        out_shape=[jax.ShapeDtypeStruct((n, n_out), BF16),
                   jax.ShapeDtypeStruct((n, n_out), BF16)],
        compiler_params=_params(("parallel", "arbitrary")),
        name="mla_kv",
    )(z, z, kv_a_g.reshape(1, -1), w_kv_b, g_nope.reshape(1, LANES), gc, gs)


def _softmax_pv(q, k_refs, v_refs, c0, dq, *, den_from_v):
    chunks = []
    for k_ref, v_ref in zip(k_refs, v_refs):
        step = min(k_ref.shape[0], KEY_CHUNK)
        chunks += [(k_ref, v_ref, r0, step) for r0 in range(0, k_ref.shape[0], step)]
    m = acc = den = None
    for k_ref, v_ref, r0, step in chunks:
        s = lax.dot_general(q, k_ref[r0:r0 + step, c0:c0 + dq], (((1,), (1,)), ((), ())),
                            preferred_element_type=F32)
        mc = jnp.max(s, axis=-1, keepdims=True)
        m_new = mc if m is None else jnp.maximum(m, mc)
        p = jnp.exp2((s - m_new).astype(BF16)) if den_from_v else jnp.exp2(s - m_new)
        pv = jnp.dot(p.astype(BF16), v_ref[r0:r0 + step, :], preferred_element_type=F32)
        if m is None:
            acc = pv
            if not den_from_v:
                den = jnp.sum(p, axis=-1, keepdims=True)
        else:
            alpha = jnp.exp2(m - m_new)
            acc = acc * alpha + pv
            if not den_from_v:
                den = den * alpha + jnp.sum(p, axis=-1, keepdims=True)
        m = m_new
    if den_from_v:
        dv = acc.shape[1] - LANES
        return acc[:, :dv] / acc[:, dv:dv + 1]
    return acc / den


def _attn_kernel(*refs, n_seg, dq):
    q_ref = refs[0]
    k_refs = refs[1:1 + n_seg]
    v_refs = refs[1 + n_seg:1 + 2 * n_seg]
    o_ref = refs[-1]
    o_ref[...] = _softmax_pv(q_ref[...], k_refs, v_refs, 0, dq, den_from_v=True).astype(o_ref.dtype)


def _diff_attn_kernel(*refs, n_seg, dq, lam_init):
    q_ref = refs[0]
    k_refs = refs[1:1 + n_seg]
    v_refs = refs[1 + n_seg:1 + 2 * n_seg]
    lam_ref, g_ref, o_ref = refs[-3], refs[-2], refs[-1]
    lf = lam_ref[...]
    lam = (jnp.exp(jnp.sum(lf[0:1] * lf[1:2], axis=-1, keepdims=True))
           - jnp.exp(jnp.sum(lf[2:3] * lf[3:4], axis=-1, keepdims=True)) + lam_init)
    q = q_ref[...]
    o1 = _softmax_pv(q[:, :dq], k_refs, v_refs, 0, dq, den_from_v=False)
    o2 = _softmax_pv(q[:, dq:], k_refs, v_refs, dq, dq, den_from_v=False)
    o = o1 - lam * o2
    o_ref[...] = (_rms(o, o.shape[-1]) * g_ref[...] * (1.0 - lam_init)).astype(o_ref.dtype)


def _attention(q, k, v, *, heads, q_width, v_width, dv, seq, n_ctx, batch, lat_queries, diff=None):
    lat_rows = batch * seq
    ctx_blk0 = lat_rows // n_ctx
    if lat_queries:
        tq = ATTN_Q_TILE
        nq = seq // tq
        q_map = lambda b, h, i: (b * nq + i, h)
        segs = [(seq, lambda b, h, i: (b, h)), (n_ctx, lambda b, h, i: (ctx_blk0 + b, h))]
        out_rows = lat_rows
    else:
        tq = n_ctx
        nq = 1
        q_map = lambda b, h, i: (ctx_blk0 + b, h)
        segs = [(n_ctx, lambda b, h, i: (ctx_blk0 + b, h))]
        out_rows = batch * n_ctx
    n_seg = len(segs)
    in_specs = [pl.BlockSpec((tq, q_width), q_map)]
    in_specs += [pl.BlockSpec((rows, q_width), m) for rows, m in segs]
    in_specs += [pl.BlockSpec((rows, v_width), m) for rows, m in segs]
    args = [q] + [k] * n_seg + [v] * n_seg
    if diff is None:
        kern = functools.partial(_attn_kernel, n_seg=n_seg, dq=q_width)
    else:
        lam, subln_g, lam_init = diff
        kern = functools.partial(_diff_attn_kernel, n_seg=n_seg, dq=q_width // 2, lam_init=lam_init)
        in_specs += [pl.BlockSpec(lam.shape, lambda b, h, i: (0, 0)), pl.BlockSpec((1, dv), lambda b, h, i: (0, 0))]
        args += [lam, subln_g.reshape(1, dv)]
    return pl.pallas_call(
        kern,
        grid=(batch, heads, nq),
        in_specs=in_specs,
        out_specs=pl.BlockSpec((tq, dv), lambda b, h, i: (b * nq + i, h)),
        out_shape=jax.ShapeDtypeStruct((out_rows, heads * dv), BF16),
        compiler_params=_params(("parallel", "parallel", "arbitrary")),
        name="attention" if diff is None else "diff_attention",
    )(*args)


def _ret_kernel(dec_ref, q_ref, k_ref, v_ref, o_ref, state_ref, *, backward):
    @pl.when(pl.program_id(2) == 0)
    def _():
        state_ref[...] = jnp.zeros_like(state_ref)

    ch = q_ref.shape[0]
    ii = lax.broadcasted_iota(jnp.int32, (ch, ch), 0).astype(F32)
    jj = lax.broadcasted_iota(jnp.int32, (ch, ch), 1).astype(F32)
    pos = lax.broadcasted_iota(jnp.int32, (ch, 1), 0).astype(F32)
    if backward:
        dist, valid = jnp.maximum(jj - ii, 0.0), jj > ii
        q_pow, k_pow = ch - pos, pos
    else:
        dist, valid = jnp.maximum(ii - jj, 0.0), ii >= jj
        q_pow, k_pow = pos + 1.0, ch - 1.0 - pos
    for h in range(state_ref.shape[0]):
        lg = jax.nn.log_sigmoid(dec_ref[h])[:, :1]
        intra = jnp.where(valid, jnp.exp(lg * dist), 0.0)
        q = q_ref[:, h * RET_DK:(h + 1) * RET_DK]
        k = k_ref[:, h * RET_DK:(h + 1) * RET_DK]
        v = v_ref[:, h * RET_DV:(h + 1) * RET_DV]
        scores = lax.dot_general(q, k, (((1,), (1,)), ((), ())), preferred_element_type=F32) * intra
        state = state_ref[h]
        o = jnp.dot(scores.astype(BF16), v, preferred_element_type=F32)
        o += jnp.dot((q.astype(F32) * jnp.exp(lg * q_pow)).astype(BF16), state.astype(BF16),
                     preferred_element_type=F32)
        o_ref[:, h * RET_DV:(h + 1) * RET_DV] = o.astype(o_ref.dtype)
        kd_t = (k.astype(F32) * jnp.exp(lg * k_pow)).T.astype(BF16)
        state_ref[h] = state * jnp.exp(lg * ch) + jnp.dot(kd_t, v, preferred_element_type=F32)


def _retention(dec, q, k, v, *, backward, seq, n_ctx, batch):
    ch = RET_CHUNK
    n_c, n_s = n_ctx // ch, seq // ch
    ctx_blk0 = batch * seq // ch

    def row_block(b, t):
        if backward:
            return jnp.where(t < n_c, ctx_blk0 + b * n_c + (n_c - 1 - t), b * n_s + (n_s - 1 - (t - n_c)))
        return jnp.where(t < n_c, ctx_blk0 + b * n_c + t, b * n_s + (t - n_c))

    hb = RET_HEAD_BLOCK
    spec = lambda width: pl.BlockSpec((ch, hb * width), lambda b, h, t: (row_block(b, t), h))
    return pl.pallas_call(
        functools.partial(_ret_kernel, backward=backward),
        grid=(batch, RET_HEADS // hb, n_c + n_s),
        in_specs=[pl.BlockSpec((hb, 1, LANES), lambda b, h, t: (h, 0, 0)), spec(RET_DK), spec(RET_DK), spec(RET_DV)],
        out_specs=spec(RET_DV),
        out_shape=jax.ShapeDtypeStruct((q.shape[0], RET_HEADS * RET_DV), BF16),
        scratch_shapes=[pltpu.VMEM((hb, RET_DK, RET_DV), F32)],
        compiler_params=_params(("parallel", "parallel", "arbitrary")),
        name="retention_bwd" if backward else "retention_fwd",
    )(dec, q, k, v)


def _ret_finish_kernel(of_ref, ob_ref, gate_ref, g_ref, o_ref):
    for h in range(RET_HEADS):
        cols = slice(h * RET_DV, (h + 1) * RET_DV)
        y = of_ref[:, cols].astype(F32) + ob_ref[:, cols].astype(F32)
        o_ref[:, cols] = (gate_ref[:, cols].astype(F32) * (_rms(y, RET_DV) * g_ref[:, cols])).astype(o_ref.dtype)


def _ret_finish(o_f, o_b, gate, norm_g):
    n, w = o_f.shape
    tm = ROW_TILE
    row = pl.BlockSpec((tm, w), lambda i: (i, 0))
    return pl.pallas_call(
        _ret_finish_kernel,
        grid=(n // tm,),
        in_specs=[row, row, row, pl.BlockSpec((1, w), lambda i: (0, 0))],
        out_specs=row,
        out_shape=jax.ShapeDtypeStruct((n, w), BF16),
        compiler_params=_params(("parallel",)),
        name="ret_finish",
    )(o_f, o_b, gate, norm_g.reshape(1, w))


def _moe_kernel(te_ref, nu_ref, first_ref, nxt_ref, x_ref, wg_hbm, wu_hbm, wd_hbm, o_ref,
                stage_g, stage_u, stage_d, cur_g, cur_u, cur_d, sem, *, layer):
    t = pl.program_id(0)

    def weight_copies(e):
        return (pltpu.make_async_copy(wg_hbm.at[layer, e], stage_g, sem.at[0]),
                pltpu.make_async_copy(wu_hbm.at[layer, e], stage_u, sem.at[1]),
                pltpu.make_async_copy(wd_hbm.at[layer, e], stage_d, sem.at[2]))

    @pl.when(t == 0)
    def _():
        for cp in weight_copies(te_ref[0]):
            cp.start()

    @pl.when((t < nu_ref[0]) & (first_ref[t] == 1))
    def _():
        for cp in weight_copies(te_ref[t]):
            cp.wait()
        for stage, cur in ((stage_g, cur_g), (stage_u, cur_u), (stage_d, cur_d)):
            rows = stage.shape[0] // WEIGHT_CAST_CHUNKS

            def cast(c, carry, stage=stage, cur=cur, rows=rows):
                r0 = pl.multiple_of(c * rows, rows)
                cur[pl.ds(r0, rows), :] = stage[pl.ds(r0, rows), :].astype(BF16)
                return carry

            lax.fori_loop(0, WEIGHT_CAST_CHUNKS, cast, 0)

        @pl.when(nxt_ref[t] >= 0)
        def _():
            for cp in weight_copies(nxt_ref[t]):
                cp.start()

    @pl.when(t < nu_ref[0])
    def _():
        x = x_ref[...].astype(BF16)
        a = jnp.dot(x, cur_g[...], preferred_element_type=F32)
        u = jnp.dot(x, cur_u[...], preferred_element_type=F32)
        act = (a * jax.nn.sigmoid(a) * u).astype(BF16)
        o_ref[...] = jnp.dot(act, cur_d[...], preferred_element_type=F32).astype(o_ref.dtype)

    @pl.when(t >= nu_ref[0])
    def _():
        o_ref[...] = jnp.zeros_like(o_ref)


def _moe_ffn(tile_expert, n_used, first, nxt, x_sorted, wg, wu, wd, layer):
    r, d = x_sorted.shape
    f = wg.shape[3]
    tm = MOE_ROW_TILE
    any_spec = pl.BlockSpec(memory_space=pl.ANY)
    return pl.pallas_call(
        functools.partial(_moe_kernel, layer=layer),
        grid_spec=pltpu.PrefetchScalarGridSpec(
            num_scalar_prefetch=4,
            grid=(r // tm,),
            in_specs=[pl.BlockSpec((tm, d), lambda t, *_: (t, 0)), any_spec, any_spec, any_spec],
            out_specs=pl.BlockSpec((tm, d), lambda t, *_: (t, 0)),
            scratch_shapes=[pltpu.VMEM((d, f), F32), pltpu.VMEM((d, f), F32), pltpu.VMEM((f, d), F32),
                            pltpu.VMEM((d, f), BF16), pltpu.VMEM((d, f), BF16), pltpu.VMEM((f, d), BF16),
                            pltpu.SemaphoreType.DMA((3,))],
        ),
        out_shape=jax.ShapeDtypeStruct((r, d), F32),
        compiler_params=_params(("arbitrary",)),
        name="moe_ffn",
    )(tile_expert, n_used, first, nxt, x_sorted, wg, wu, wd)


def _combine_kernel(x_ref, a_ref, b_ref, w_ref, g_ref, o_ref):
    w = w_ref[...]
    o_ref[...] = x_ref[...] + g_ref[0] * (w[:, 0:1] * a_ref[...] + w[:, 1:2] * b_ref[...])


def _combine(x, ab, w_col, gate, rows_per_group, n_groups):
    n, d = x.shape
    tm = ROW_TILE
    row = pl.BlockSpec((tm, d), lambda i: (i, 0))
    return pl.pallas_call(
        _combine_kernel,
        grid=(n // tm,),
        in_specs=[row, row, pl.BlockSpec((tm, d), lambda i: (i + n // tm, 0)),
                  pl.BlockSpec((tm, LANES), lambda i: (i, 0)), _group_spec(d, tm, rows_per_group, n_groups)],
        out_specs=row,
        out_shape=jax.ShapeDtypeStruct((n, d), F32),
        compiler_params=_params(("parallel",)),
        name="moe_combine",
    )(x, ab, ab, w_col, gate)


def _moe_layer(x, g, sh, sc, gate, rw_pad, rb_col, wg, wu, wd, layer, rows_per_group, n_groups):
    n, d = x.shape
    tm = MOE_ROW_TILE
    h, idx8, w_col, cnt = _norm_mod_route(x, g, sh, sc, rw_pad, rb_col, rows_per_group, n_groups)
    experts = jnp.arange(N_EXPERTS, dtype=jnp.int32)
    padded = (cnt[:, 0] + tm - 1) // tm * tm
    ends = jnp.cumsum(padded)
    starts = ends - padded
    e2, rank2 = idx8[0:2], idx8[2:4]
    dest = rank2 + jnp.sum(jnp.where(e2[..., None] == experts, starts, 0), axis=-1)
    r = (2 * n + N_EXPERTS * (tm - 1)) // tm * tm
    tok = jnp.tile(jnp.arange(n, dtype=jnp.int32), 2)
    src = (jnp.arange(r, dtype=jnp.int32) % n).at[dest.reshape(-1)].set(tok, unique_indices=True,
                                                                         mode="promise_in_bounds")
    tile_start = jnp.arange(r // tm, dtype=jnp.int32) * tm
    tile_expert = jnp.minimum(jnp.sum((ends[None, :] <= tile_start[:, None]).astype(jnp.int32), axis=1),
                              N_EXPERTS - 1)
    n_used = ends[-1:] // tm
    first = jnp.concatenate([jnp.ones((1,), jnp.int32), (tile_expert[1:] != tile_expert[:-1]).astype(jnp.int32)])
    later = (padded[None, :] > 0) & (experts[None, :] > experts[:, None])
    next_of = jnp.min(jnp.where(later, experts[None, :], N_EXPERTS), axis=1)
    next_of = jnp.where(next_of < N_EXPERTS, next_of, -1)
    nxt = jnp.sum(jnp.where(tile_expert[:, None] == experts, next_of, 0), axis=1)
    x_sorted = h.at[src].get(mode="promise_in_bounds")
    y_sorted = _moe_ffn(tile_expert, n_used, first, nxt, x_sorted, wg, wu, wd, layer)
    ab = y_sorted.at[dest.reshape(-1)].get(mode="promise_in_bounds")
    return _combine(x, ab, w_col, gate, rows_per_group, n_groups)


def _axial_tables(rows, rot_dim):
    n_freq = rot_dim // 4
    inv_freq = jnp.power(ROPE_BASE, -jnp.arange(n_freq, dtype=F32) / n_freq)
    row = jnp.repeat(jnp.arange(rows, dtype=F32), GRID_W)
    col = jnp.tile(jnp.arange(GRID_W, dtype=F32), rows)
    ang = jnp.concatenate([row[:, None] * inv_freq, col[:, None] * inv_freq], axis=-1)
    return jnp.cos(ang), jnp.sin(ang)


def _flat_tables(c_lat, s_lat, batch, n_ctx_rows):
    c = jnp.concatenate([jnp.tile(c_lat, (batch, 1)), jnp.broadcast_to(c_lat[:1], (n_ctx_rows, LANES))])
    s = jnp.concatenate([jnp.tile(s_lat, (batch, 1)), jnp.zeros((n_ctx_rows, LANES), F32)])
    return c, s


def _spread_pairs(a, axis):
    x1, x2 = jnp.split(a, 2, axis=axis)
    z = jnp.zeros_like(x1)
    return jnp.concatenate([x1, z, x2, z], axis=axis)


def kernel(x, c, ctx, c_ctx, ada_w, ada_b, norm_mix_g, norm_ffn_g, mla_w_in, mla_q_a_g, mla_w_q_b, mla_kv_a_g,
           mla_w_kv_b, mla_q_norm_g, mla_k_norm_g, mla_w_o, diff_w_in, diff_q_norm_g, diff_k_norm_g, diff_lambda,
           diff_subln_g, diff_w_o, ret_w_in, ret_decay_fwd, ret_decay_bwd, ret_norm_g, ret_w_o, router_w,
           router_bias, moe_w_gate, moe_w_up, moe_w_down):
    batch, seq, d = x.shape
    n_ctx = ctx.shape[1]
    depth = ada_w.shape[0]
    n_lat, n_cx = batch * seq, batch * n_ctx
    n_groups = batch + 1
    assert n_groups <= SUBLANES and seq % MM_ROW_TILE == 0 and n_cx % MM_ROW_TILE == 0
    assert seq % RET_CHUNK == 0 and n_ctx % RET_CHUNK == 0 and n_lat % n_ctx == 0 and n_ctx == ROW_TILE

    xa = jnp.concatenate([x.reshape(n_lat, d), ctx.reshape(n_cx, d)], axis=0)

    cond = jnp.concatenate([c, c_ctx[None, :], jnp.zeros((SUBLANES - n_groups, d), F32)], axis=0)
    mods = _ada_mods(cond.T, ada_w, ada_b, n_groups)

    rows = seq // GRID_W
    cos64, sin64 = _axial_tables(rows, MLA_ROPE)
    zeros32 = jnp.zeros_like(cos64)
    mla_c, mla_s = _flat_tables(jnp.concatenate([cos64, zeros32, cos64, zeros32], axis=1),
                                jnp.concatenate([-sin64, zeros32, sin64, zeros32], axis=1), batch, n_cx)
    cos128, sin128 = _axial_tables(rows, DIFF_HEAD_DIM)
    diff_c, diff_s = _flat_tables(jnp.concatenate([cos128, cos128], axis=1),
                                  jnp.concatenate([-sin128, sin128], axis=1), batch, n_cx)
    cos256, sin256 = _axial_tables(rows, RET_DK)
    ret_c, ret_s = _flat_tables(cos256, sin256, batch, n_cx)

    rw_pad = jnp.concatenate([router_w, router_w, jnp.zeros((d, LANES - 2 * N_EXPERTS), F32)], axis=1)
    rb_col = router_bias.reshape(N_EXPERTS, 1)

    for i in range(depth):
        kind, j, last = i % N_MIXERS, i // N_MIXERS, i == depth - 1
        m = mods[i].reshape(SUBLANES, 6, 1, d)
        sh_m, sc_m, g_m, sh_f, sc_f, g_f = (m[:, t] for t in range(6))
        n_all = xa.shape[0]
        h = _norm_mod(xa, norm_mix_g[i], sh_m, sc_m, seq, n_groups)
        q_rows = n_lat if last else n_all

        if kind == 0:
            w_in = jnp.concatenate([mla_w_in[j][:, :MLA_Q_LORA + MLA_KV_LORA],
                                    _spread_pairs(mla_w_in[j][:, MLA_Q_LORA + MLA_KV_LORA:], 1)], axis=1)
            z = _proj(h, w_in, 0, w_in.shape[1], out_dtype=F32, tn=w_in.shape[1])
            wq = mla_w_q_b[j].reshape(MLA_Q_LORA, MLA_HEADS, MLA_NOPE + MLA_ROPE)
            wq = jnp.concatenate([wq[..., :MLA_NOPE], _spread_pairs(wq[..., MLA_NOPE:], 2)], axis=2)
            wq = wq.reshape(MLA_Q_LORA, MLA_HEADS * MLA_HEAD_PAD).astype(BF16)
            scale = float(MLA_NOPE + MLA_ROPE) ** -0.5 * LOG2E
            qg, kg = mla_q_norm_g[j], mla_k_norm_g[j]
            q = _mla_q(z, mla_q_a_g[j], wq, qg[:MLA_NOPE] * scale,
                       *_fold_rope(mla_c, mla_s, _spread_pairs(qg[MLA_NOPE:], 0), scale), q_rows)
            k, v = _mla_kv(z, mla_kv_a_g[j], mla_w_kv_b[j].astype(BF16), kg[:MLA_NOPE],
                           *_fold_rope(mla_c, mla_s, _spread_pairs(kg[MLA_NOPE:], 0), 1.0))
            attn = functools.partial(_attention, q, k, v, heads=MLA_HEADS, q_width=MLA_HEAD_PAD,
                                     v_width=MLA_V + LANES, dv=MLA_V, seq=seq, n_ctx=n_ctx, batch=batch)
            w_o = mla_w_o[j]
        elif kind == 1:
            w_in = diff_w_in[j]
            hd = DIFF_HEADS * 2 * DIFF_HEAD_DIM
            scale = float(DIFF_HEAD_DIM) ** -0.5 * LOG2E
            q = _proj(h[:q_rows], w_in, 0, hd, mode="norm_rope",
                      extras=_fold_rope(diff_c[:q_rows], diff_s[:q_rows], diff_q_norm_g[j], scale))
            k = _proj(h, w_in, hd, hd, mode="norm_rope", extras=_fold_rope(diff_c, diff_s, diff_k_norm_g[j], 1.0))
            v = _proj(h, w_in, 2 * hd, hd)
            lam_init = 0.8 - 0.6 * math.exp(-0.3 * i)
            attn = functools.partial(_attention, q, k, v, heads=DIFF_HEADS, q_width=2 * DIFF_HEAD_DIM,
                                     v_width=2 * DIFF_HEAD_DIM, dv=2 * DIFF_HEAD_DIM, seq=seq, n_ctx=n_ctx,
                                     batch=batch,
                                     diff=(diff_lambda[j], diff_subln_g[j], lam_init))
            w_o = diff_w_o[j]
        else:
            w_in = ret_w_in[j]
            nq, nv = RET_HEADS * RET_DK, RET_HEADS * RET_DV
            q = _proj(h, w_in, 0, nq, mode="rope256", extras=(ret_c, ret_s))
            k_scale = float(RET_DK) ** -0.5
            k = _proj(h, w_in, nq, nq, mode="rope256", extras=(ret_c * k_scale, ret_s * k_scale))
            v = _proj(h, w_in, 2 * nq, nv)
            gate = _proj(h, w_in, 2 * nq + nv, nv, mode="silu")
            dec = lambda p: jnp.broadcast_to(p.astype(F32)[:, None, None], (RET_HEADS, 1, LANES))
            o_f = _retention(dec(ret_decay_fwd[j]), q, k, v, backward=False, seq=seq, n_ctx=n_ctx, batch=batch)
            o_b = _retention(dec(ret_decay_bwd[j]), q, k, v, backward=True, seq=seq, n_ctx=n_ctx, batch=batch)
            y = _ret_finish(o_f, o_b, gate, ret_norm_g[j])
            attn = None
            w_o = ret_w_o[j]

        if attn is not None:
            y = attn(lat_queries=True)
            if not last:
                y = jnp.concatenate([y, attn(lat_queries=False)], axis=0)
        xa = _out_res(y, w_o, xa, g_m, seq, n_groups, rows=n_lat if last else n_all)
        xa = _moe_layer(xa, norm_ffn_g[i], sh_f, sc_f, g_f, rw_pad, rb_col, moe_w_gate, moe_w_up, moe_w_down, i,
                        seq, n_groups)
    return xa[:n_lat].reshape(batch, seq, d)
```

```python
import functools
import math

import jax
import jax.numpy as jnp
from jax import lax
from jax.experimental import pallas as pl
from jax.experimental.pallas import tpu as pltpu

F32 = jnp.float32
BF16 = jnp.bfloat16

GRID_W = 64
ROPE_BASE = 10000.0
NORM_EPS = 1e-6
N_MIXERS = 3

MLA_HEADS = 16
MLA_Q_LORA = 512
MLA_KV_LORA = 512
MLA_NOPE = 128
MLA_ROPE = 64
MLA_V = 128
MLA_HEAD_PAD = 256

DIFF_HEADS = 8
DIFF_HEAD_DIM = 128

RET_HEADS = 8
RET_DK = 256
RET_DV = 512
RET_CHUNK = 256

N_EXPERTS = 16
N_GROUPS = 4
EXPERTS_PER_GROUP = 4
D_EXPERT = 1024

LANES = 128
SUBLANES = 8
VMEM_LIMIT = 56 * 1024 * 1024

ROW_TILE = 256
MM_ROW_TILE = 512
MM_COL_TILE = 1024
MM_COL_TILE_MAX_K = 2048
RET_HEAD_BLOCK = 4
WEIGHT_CAST_CHUNKS = 8
MOE_ROW_TILE = 256
KEY_CHUNK = 512
ATTN_Q_TILE = 512
LOG2E = 1.4426950408889634
ADA_COL_TILE = 1024


def _params(sem):
    return pltpu.CompilerParams(dimension_semantics=sem, vmem_limit_bytes=VMEM_LIMIT)


def _rms(x, width):
    return x * lax.rsqrt(jnp.sum(x * x, axis=-1, keepdims=True) / width + NORM_EPS)


def _ada_kernel(ct_ref, w_ref, b_ref, o_ref, sb_ref, *, n_cond):
    d = ct_ref.shape[0]
    tn = o_ref.shape[-1]

    @pl.when((pl.program_id(0) == 0) & (pl.program_id(1) == 0))
    def _():
        ct = ct_ref[...]
        s = ct * jax.nn.sigmoid(ct)
        for r in range(n_cond):
            sb_ref[r] = jnp.broadcast_to(s[:, r:r + 1], (d, LANES))

    def body(kc, accs):
        k0 = pl.multiple_of(kc * SUBLANES, SUBLANES)
        w8 = w_ref[0, pl.ds(k0, SUBLANES), :]
        out = []
        for r in range(n_cond):
            s8 = sb_ref[r, pl.ds(k0, SUBLANES), :]
            out.append(accs[r] + w8 * jnp.concatenate([s8] * (tn // LANES), axis=1))
        return tuple(out)

    accs = lax.fori_loop(0, d // SUBLANES, body,
                         tuple(jnp.zeros((SUBLANES, tn), F32) for _ in range(n_cond)), unroll=4)
    rows = [jnp.sum(a, axis=0, keepdims=True) + b_ref[0] for a in accs]
    rows.append(jnp.zeros((SUBLANES - n_cond, tn), F32))
    o_ref[0] = jnp.concatenate(rows, axis=0)


def _ada_mods(cond_t, ada_w, ada_b, n_cond):
    depth, d, n6 = ada_w.shape
    tn = ADA_COL_TILE
    return pl.pallas_call(
        functools.partial(_ada_kernel, n_cond=n_cond),
        grid=(depth, n6 // tn),
        in_specs=[pl.BlockSpec((d, SUBLANES), lambda l, j: (0, 0)),
                  pl.BlockSpec((1, d, tn), lambda l, j: (l, 0, j)),
                  pl.BlockSpec((1, 1, tn), lambda l, j: (l, 0, j))],
        out_specs=pl.BlockSpec((1, SUBLANES, tn), lambda l, j: (l, 0, j)),
        out_shape=jax.ShapeDtypeStruct((depth, SUBLANES, n6), F32),
        scratch_shapes=[pltpu.VMEM((n_cond, d, LANES), F32)],
        compiler_params=_params(("arbitrary", "arbitrary")),
        name="ada_mods",
    )(cond_t, ada_w, ada_b.reshape(depth, 1, n6))


def _group_spec(d, tm, rows_per_group, n_groups):
    return pl.BlockSpec((1, 1, d), lambda i, *_: (jnp.minimum(i * tm // rows_per_group, n_groups - 1), 0, 0))


def _norm_mod_kernel(x_ref, g_ref, sh_ref, sc_ref, o_ref):
    x = x_ref[...]
    h = _rms(x, x.shape[-1]) * g_ref[...] * (1.0 + sc_ref[0]) + sh_ref[0]
    o_ref[...] = h.astype(o_ref.dtype)


def _norm_mod(x, g, sh, sc, rows_per_group, n_groups):
    n, d = x.shape
    tm = ROW_TILE
    gs = _group_spec(d, tm, rows_per_group, n_groups)
    return pl.pallas_call(
        _norm_mod_kernel,
        grid=(n // tm,),
        in_specs=[pl.BlockSpec((tm, d), lambda i: (i, 0)), pl.BlockSpec((1, d), lambda i: (0, 0)), gs, gs],
        out_specs=pl.BlockSpec((tm, d), lambda i: (i, 0)),
        out_shape=jax.ShapeDtypeStruct((n, d), BF16),
        compiler_params=_params(("parallel",)),
        name="norm_mod",
    )(x, g.reshape(1, d), sh, sc)


def _route_kernel(x_ref, g_ref, sh_ref, sc_ref, rw_ref, rb_ref, h_ref, idx_ref, wt_ref, cnt_ref, carry_ref,
                  rwp_ref):
    @pl.when(pl.program_id(0) == 0)
    def _():
        carry_ref[...] = jnp.zeros_like(carry_ref)
        w = rw_ref[...]
        hi = w.astype(BF16)
        lo = (w - hi.astype(F32)).astype(BF16)
        rwp_ref[...] = jnp.where(lax.broadcasted_iota(jnp.int32, w.shape, 1) < N_EXPERTS, hi, lo)

    x = x_ref[...]
    tm = x.shape[0]
    h = _rms(x, x.shape[-1]) * g_ref[...] * (1.0 + sc_ref[0]) + sh_ref[0]
    h_ref[...] = h.astype(h_ref.dtype)
    h_hi = h.astype(BF16)
    h_lo = (h - h_hi.astype(F32)).astype(BF16)
    parts = (jnp.dot(h_hi, rwp_ref[...], preferred_element_type=F32)
             + jnp.dot(h_lo, rwp_ref[...], preferred_element_type=F32)).T
    logits = parts[:N_EXPERTS] + parts[N_EXPERTS:2 * N_EXPERTS]
    scores = jax.nn.sigmoid(logits)
    sel = scores + rb_ref[...]
    sel_r = [sel[e:e + 1, :] for e in range(N_EXPERTS)]
    sc_r = [scores[e:e + 1, :] for e in range(N_EXPERTS)]

    gscore = []
    for g in range(N_GROUPS):
        a, b, c, d = sel_r[4 * g:4 * g + 4]
        hi1, lo1, hi2, lo2 = jnp.maximum(a, b), jnp.minimum(a, b), jnp.maximum(c, d), jnp.minimum(c, d)
        gscore.append(jnp.maximum(hi1, hi2) + jnp.maximum(jnp.minimum(hi1, hi2), jnp.maximum(lo1, lo2)))
    grp = jnp.zeros((1, tm), jnp.int32)
    best = gscore[0]
    for g in range(1, N_GROUPS):
        better = gscore[g] > best
        grp = jnp.where(better, g, grp)
        best = jnp.where(better, gscore[g], best)

    def pick(rows, l):
        out = rows[l]
        for g in range(1, N_GROUPS):
            out = jnp.where(grp == g, rows[4 * g + l], out)
        return out

    v = [pick(sel_r, l) for l in range(EXPERTS_PER_GROUP)]
    s = [pick(sc_r, l) for l in range(EXPERTS_PER_GROUP)]

    def first_max(vals):
        m = jnp.maximum(jnp.maximum(vals[0], vals[1]), jnp.maximum(vals[2], vals[3]))
        l = jnp.where(vals[0] == m, 0, jnp.where(vals[1] == m, 1, jnp.where(vals[2] == m, 2, 3)))
        return l

    def at(vals, l):
        return jnp.where(l == 0, vals[0], jnp.where(l == 1, vals[1], jnp.where(l == 2, vals[2], vals[3])))

    l1 = first_max(v)
    l2 = first_max([jnp.where(l1 == l, -jnp.inf, v[l]) for l in range(EXPERTS_PER_GROUP)])
    s1, s2 = at(s, l1), at(s, l2)
    tot = s1 + s2
    e1, e2 = grp * EXPERTS_PER_GROUP + l1, grp * EXPERTS_PER_GROUP + l2

    eids = lax.broadcasted_iota(jnp.int32, (N_EXPERTS, tm), 0)
    oh1, oh2 = (eids == e1).astype(F32), (eids == e2).astype(F32)
    before = (lax.broadcasted_iota(jnp.int32, (tm, tm), 0) < lax.broadcasted_iota(jnp.int32, (tm, tm), 1)).astype(BF16)
    carry = carry_ref[:, :1]
    tot1 = jnp.sum(oh1, axis=1, keepdims=True)
    pre1 = carry + jnp.dot(oh1.astype(BF16), before, preferred_element_type=F32)
    pre2 = carry + tot1 + jnp.dot(oh2.astype(BF16), before, preferred_element_type=F32)
    r1 = jnp.sum(oh1 * pre1, axis=0, keepdims=True).astype(jnp.int32)
    r2 = jnp.sum(oh2 * pre2, axis=0, keepdims=True).astype(jnp.int32)
    carry_new = carry + tot1 + jnp.sum(oh2, axis=1, keepdims=True)
    carry_ref[...] = jnp.broadcast_to(carry_new, carry_ref.shape)
    cnt_ref[...] = jnp.broadcast_to(carry_new, cnt_ref.shape).astype(jnp.int32)

    idx_ref[...] = jnp.concatenate([e1, e2, r1, r2, jnp.zeros((SUBLANES - 4, tm), jnp.int32)], axis=0)
    wt_ref[...] = jnp.concatenate([s1 / tot, s2 / tot, jnp.zeros((LANES - 2, tm), F32)], axis=0).T


def _norm_mod_route(x, g, sh, sc, rw_pad, rb_col, rows_per_group, n_groups):
    n, d = x.shape
    tm = ROW_TILE
    gs = _group_spec(d, tm, rows_per_group, n_groups)
    return pl.pallas_call(
        _route_kernel,
        grid=(n // tm,),
        in_specs=[pl.BlockSpec((tm, d), lambda i: (i, 0)), pl.BlockSpec((1, d), lambda i: (0, 0)), gs, gs,
                  pl.BlockSpec((d, LANES), lambda i: (0, 0)), pl.BlockSpec((N_EXPERTS, 1), lambda i: (0, 0))],
        out_specs=[pl.BlockSpec((tm, d), lambda i: (i, 0)),
                   pl.BlockSpec((SUBLANES, tm), lambda i: (0, i)),
                   pl.BlockSpec((tm, LANES), lambda i: (i, 0)),
                   pl.BlockSpec((N_EXPERTS, LANES), lambda i: (0, 0))],
        out_shape=[jax.ShapeDtypeStruct((n, d), F32),
                   jax.ShapeDtypeStruct((SUBLANES, n), jnp.int32),
                   jax.ShapeDtypeStruct((n, LANES), F32),
                   jax.ShapeDtypeStruct((N_EXPERTS, LANES), jnp.int32)],
        scratch_shapes=[pltpu.VMEM((N_EXPERTS, LANES), F32), pltpu.VMEM((d, LANES), BF16)],
        compiler_params=_params(("arbitrary",)),
        name="norm_mod_route",
    )(x, g.reshape(1, d), sh, sc, rw_pad, rb_col)


def _half_swap_rope(x, c, s):
    return x * c + pltpu.roll(x, LANES // 2, 1) * s


def _fold_rope(c_t, s_t, g, scale):
    return c_t * (g * scale)[None, :], s_t * (jnp.roll(g, LANES // 2) * scale)[None, :]


def _proj_kernel(*refs, mode):
    x_ref, w_ref = refs[0], refs[1]
    o_ref, wb_ref = refs[-2], refs[-1]

    @pl.when(pl.program_id(1) == 0)
    def _():
        wb_ref[...] = w_ref[...].astype(BF16)

    acc = jnp.dot(x_ref[...], wb_ref[...], preferred_element_type=F32)
    tn = acc.shape[1]
    if mode == "plain":
        o_ref[...] = acc.astype(o_ref.dtype)
    elif mode == "silu":
        o_ref[...] = (acc * jax.nn.sigmoid(acc)).astype(o_ref.dtype)
    elif mode == "norm_rope":
        gc, gs = refs[2][...], refs[3][...]
        for j in range(tn // LANES):
            blk = _rms(acc[:, j * LANES:(j + 1) * LANES], LANES)
            o_ref[:, j * LANES:(j + 1) * LANES] = _half_swap_rope(blk, gc, gs).astype(o_ref.dtype)
    elif mode == "rope256":
        c, s = refs[2][...], refs[3][...]
        for j in range(tn // (2 * LANES)):
            x1 = acc[:, (2 * j) * LANES:(2 * j + 1) * LANES]
            x2 = acc[:, (2 * j + 1) * LANES:(2 * j + 2) * LANES]
            o_ref[:, (2 * j) * LANES:(2 * j + 1) * LANES] = (x1 * c - x2 * s).astype(o_ref.dtype)
            o_ref[:, (2 * j + 1) * LANES:(2 * j + 2) * LANES] = (x2 * c + x1 * s).astype(o_ref.dtype)
    else:
        raise ValueError(mode)


def _col_tile(k):
    return MM_COL_TILE if k <= MM_COL_TILE_MAX_K else MM_COL_TILE // 2


def _proj(x, w, col0, n_cols, *, mode="plain", extras=(), out_dtype=BF16, tn=None):
    n, k = x.shape
    tm = MM_ROW_TILE
    tn = min(_col_tile(k) if tn is None else tn, n_cols)
    j0 = col0 // tn
    extra_specs = []
    for e in extras:
        if e.shape[0] == 1:
            extra_specs.append(pl.BlockSpec(e.shape, lambda j, i: (0, 0)))
        else:
            extra_specs.append(pl.BlockSpec((tm, e.shape[1]), lambda j, i: (i, 0)))
    return pl.pallas_call(
        functools.partial(_proj_kernel, mode=mode),
        grid=(n_cols // tn, n // tm),
        in_specs=[pl.BlockSpec((tm, k), lambda j, i: (i, 0)),
                  pl.BlockSpec((k, tn), lambda j, i: (0, j0 + j))] + extra_specs,
        out_specs=pl.BlockSpec((tm, tn), lambda j, i: (i, j)),
        out_shape=jax.ShapeDtypeStruct((n, n_cols), out_dtype),
        scratch_shapes=[pltpu.VMEM((k, tn), BF16)],
        compiler_params=_params(("arbitrary", "arbitrary")),
        name="proj_" + mode,
    )(x, w, *extras)


def _out_res_kernel(y_ref, w_ref, x_ref, g_ref, o_ref, wb_ref):
    @pl.when(pl.program_id(1) == 0)
    def _():
        wb_ref[...] = w_ref[...].astype(BF16)

    acc = jnp.dot(y_ref[...], wb_ref[...], preferred_element_type=F32)
    o_ref[...] = x_ref[...] + g_ref[0] * acc


def _out_res(y, w, x, gate, rows_per_group, n_groups, rows):
    n, k = rows, y.shape[1]
    d = w.shape[1]
    tm = MM_ROW_TILE
    tn = _col_tile(k)
    return pl.pallas_call(
        _out_res_kernel,
        grid=(d // tn, n // tm),
        in_specs=[pl.BlockSpec((tm, k), lambda j, i: (i, 0)),
                  pl.BlockSpec((k, tn), lambda j, i: (0, j)),
                  pl.BlockSpec((tm, tn), lambda j, i: (i, j)),
                  pl.BlockSpec((1, 1, tn), lambda j, i: (jnp.minimum(i * tm // rows_per_group, n_groups - 1), 0, j))],
        out_specs=pl.BlockSpec((tm, tn), lambda j, i: (i, j)),
        out_shape=jax.ShapeDtypeStruct((n, d), F32),
        scratch_shapes=[pltpu.VMEM((k, tn), BF16)],
        compiler_params=_params(("arbitrary", "arbitrary")),
        name="out_res",
    )(y, w, x, gate)


def _mla_q_kernel(z_ref, ga_ref, w_ref, g_ref, gc_ref, gs_ref, o_ref, cn_ref):
    @pl.when(pl.program_id(1) == 0)
    def _():
        cq = z_ref[...]
        cn_ref[...] = (_rms(cq, cq.shape[-1]) * ga_ref[...]).astype(BF16)

    acc = jnp.dot(cn_ref[...], w_ref[...], preferred_element_type=F32)
    g, gc, gs = g_ref[...], gc_ref[...], gs_ref[...]
    width = float(MLA_NOPE + MLA_ROPE)
    for j in range(acc.shape[1] // MLA_HEAD_PAD):
        qn = _rms(acc[:, j * MLA_HEAD_PAD:(j + 1) * MLA_HEAD_PAD], width)
        o_ref[:, j * MLA_HEAD_PAD:j * MLA_HEAD_PAD + LANES] = (qn[:, :LANES] * g).astype(o_ref.dtype)
        o_ref[:, j * MLA_HEAD_PAD + LANES:(j + 1) * MLA_HEAD_PAD] = _half_swap_rope(
            qn[:, LANES:], gc, gs).astype(o_ref.dtype)


def _mla_q(z, q_a_g, w_q_b_pad, g_nope, gc, gs, rows, tn=1024):
    n = rows
    tm = MM_ROW_TILE
    n_out = w_q_b_pad.shape[1]
    return pl.pallas_call(
        _mla_q_kernel,
        grid=(n // tm, n_out // tn),
        in_specs=[pl.BlockSpec((tm, MLA_Q_LORA), lambda i, j: (i, 0)),
                  pl.BlockSpec((1, MLA_Q_LORA), lambda i, j: (0, 0)),
                  pl.BlockSpec((MLA_Q_LORA, tn), lambda i, j: (0, j)),
                  pl.BlockSpec((1, LANES), lambda i, j: (0, 0)),
                  pl.BlockSpec((tm, LANES), lambda i, j: (i, 0)),
                  pl.BlockSpec((tm, LANES), lambda i, j: (i, 0))],
        out_specs=pl.BlockSpec((tm, tn), lambda i, j: (i, j)),
        out_shape=jax.ShapeDtypeStruct((n, n_out), BF16),
        scratch_shapes=[pltpu.VMEM((tm, MLA_Q_LORA), BF16)],
        compiler_params=_params(("parallel", "arbitrary")),
        name="mla_q",
    )(z, q_a_g.reshape(1, -1), w_q_b_pad, g_nope.reshape(1, LANES), gc, gs)


def _mla_kv_kernel(z_ref, kr_ref, ga_ref, w_ref, g_ref, gc_ref, gs_ref, k_ref, v_ref, cn_ref):
    @pl.when(pl.program_id(1) == 0)
    def _():
        ckv = z_ref[...]
        cn_ref[...] = (_rms(ckv, ckv.shape[-1]) * ga_ref[...]).astype(BF16)

    acc = jnp.dot(cn_ref[...], w_ref[...], preferred_element_type=F32)
    kr = kr_ref[...]
    kr_ss = jnp.sum(kr * kr, axis=-1, keepdims=True)
    g, gc, gs = g_ref[...], gc_ref[...], gs_ref[...]
    width = float(MLA_NOPE + MLA_ROPE)
    for j in range(acc.shape[1] // MLA_HEAD_PAD):
        kn = acc[:, j * MLA_HEAD_PAD:j * MLA_HEAD_PAD + MLA_NOPE]
        vv = acc[:, j * MLA_HEAD_PAD + MLA_NOPE:(j + 1) * MLA_HEAD_PAD]
        f = lax.rsqrt((jnp.sum(kn * kn, axis=-1, keepdims=True) + kr_ss) / width + NORM_EPS)
        k_ref[:, j * MLA_HEAD_PAD:j * MLA_HEAD_PAD + LANES] = (kn * f * g).astype(k_ref.dtype)
        k_ref[:, j * MLA_HEAD_PAD + LANES:(j + 1) * MLA_HEAD_PAD] = _half_swap_rope(
            kr * f, gc, gs).astype(k_ref.dtype)
        v_ref[:, j * MLA_HEAD_PAD:j * MLA_HEAD_PAD + MLA_V] = vv.astype(v_ref.dtype)
        v_ref[:, j * MLA_HEAD_PAD + MLA_V:(j + 1) * MLA_HEAD_PAD] = jnp.ones((vv.shape[0], LANES), v_ref.dtype)


def _mla_kv(z, kv_a_g, w_kv_b, g_nope, gc, gs, tn=1024):
    n = z.shape[0]
    tm = MM_ROW_TILE
    n_out = w_kv_b.shape[1]
    kr_block = (MLA_Q_LORA + MLA_KV_LORA) // LANES
    return pl.pallas_call(
        _mla_kv_kernel,
        grid=(n // tm, n_out // tn),
        in_specs=[pl.BlockSpec((tm, MLA_KV_LORA), lambda i, j: (i, 1)),
                  pl.BlockSpec((tm, LANES), lambda i, j: (i, kr_block)),
                  pl.BlockSpec((1, MLA_KV_LORA), lambda i, j: (0, 0)),
                  pl.BlockSpec((MLA_KV_LORA, tn), lambda i, j: (0, j)),
                  pl.BlockSpec((1, LANES), lambda i, j: (0, 0)),
                  pl.BlockSpec((tm, LANES), lambda i, j: (i, 0)),
                  pl.BlockSpec((tm, LANES), lambda i, j: (i, 0))],
        out_specs=[pl.BlockSpec((tm, tn), lambda i, j: (i, j)),
                   pl.BlockSpec((tm, tn), lambda i, j: (i, j))],
        out_shape=[jax.ShapeDtypeStruct((n, n_out), BF16),
                   jax.ShapeDtypeStruct((n, n_out), BF16)],
        scratch_shapes=[pltpu.VMEM((tm, MLA_KV_LORA), BF16)],
        compiler_params=_params(("parallel", "arbitrary")),
        name="mla_kv",
    )(z, z, kv_a_g.reshape(1, -1), w_kv_b, g_nope.reshape(1, LANES), gc, gs)


def _softmax_pv(q, k_refs, v_refs, c0, dq, *, den_from_v):
    chunks = []
    for k_ref, v_ref in zip(k_refs, v_refs):
        step = min(k_ref.shape[0], KEY_CHUNK)
        chunks += [(k_ref, v_ref, r0, step) for r0 in range(0, k_ref.shape[0], step)]
    m = acc = den = None
    for k_ref, v_ref, r0, step in chunks:
        s = lax.dot_general(q, k_ref[r0:r0 + step, c0:c0 + dq], (((1,), (1,)), ((), ())),
                            preferred_element_type=F32)
        mc = jnp.max(s, axis=-1, keepdims=True)
        m_new = mc if m is None else jnp.maximum(m, mc)
        p = jnp.exp2((s - m_new).astype(BF16)) if den_from_v else jnp.exp2(s - m_new)
        pv = jnp.dot(p.astype(BF16), v_ref[r0:r0 + step, :], preferred_element_type=F32)
        if m is None:
            acc = pv
            if not den_from_v:
                den = jnp.sum(p, axis=-1, keepdims=True)
        else:
            alpha = jnp.exp2(m - m_new)
            acc = acc * alpha + pv
            if not den_from_v:
                den = den * alpha + jnp.sum(p, axis=-1, keepdims=True)
        m = m_new
    if den_from_v:
        dv = acc.shape[1] - LANES
        return acc[:, :dv] / acc[:, dv:dv + 1]
    return acc / den


def _attn_kernel(*refs, n_seg, dq):
    q_ref = refs[0]
    k_refs = refs[1:1 + n_seg]
    v_refs = refs[1 + n_seg:1 + 2 * n_seg]
    o_ref = refs[-1]
    o_ref[...] = _softmax_pv(q_ref[...], k_refs, v_refs, 0, dq, den_from_v=True).astype(o_ref.dtype)


def _diff_attn_kernel(*refs, n_seg, dq, lam_init):
    q_ref = refs[0]
    k_refs = refs[1:1 + n_seg]
    v_refs = refs[1 + n_seg:1 + 2 * n_seg]
    lam_ref, g_ref, o_ref = refs[-3], refs[-2], refs[-1]
    lf = lam_ref[...]
    lam = (jnp.exp(jnp.sum(lf[0:1] * lf[1:2], axis=-1, keepdims=True))
           - jnp.exp(jnp.sum(lf[2:3] * lf[3:4], axis=-1, keepdims=True)) + lam_init)
    q = q_ref[...]
    o1 = _softmax_pv(q[:, :dq], k_refs, v_refs, 0, dq, den_from_v=False)
    o2 = _softmax_pv(q[:, dq:], k_refs, v_refs, dq, dq, den_from_v=False)
    o = o1 - lam * o2
    o_ref[...] = (_rms(o, o.shape[-1]) * g_ref[...] * (1.0 - lam_init)).astype(o_ref.dtype)


def _attention(q, k, v, *, heads, q_width, v_width, dv, seq, n_ctx, batch, lat_queries, diff=None):
    lat_rows = batch * seq
    ctx_blk0 = lat_rows // n_ctx
    if lat_queries:
        tq = ATTN_Q_TILE
        nq = seq // tq
        q_map = lambda b, h, i: (b * nq + i, h)
        segs = [(seq, lambda b, h, i: (b, h)), (n_ctx, lambda b, h, i: (ctx_blk0 + b, h))]
        out_rows = lat_rows
    else:
        tq = n_ctx
        nq = 1
        q_map = lambda b, h, i: (ctx_blk0 + b, h)
        segs = [(n_ctx, lambda b, h, i: (ctx_blk0 + b, h))]
        out_rows = batch * n_ctx
    n_seg = len(segs)
    in_specs = [pl.BlockSpec((tq, q_width), q_map)]
    in_specs += [pl.BlockSpec((rows, q_width), m) for rows, m in segs]
    in_specs += [pl.BlockSpec((rows, v_width), m) for rows, m in segs]
    args = [q] + [k] * n_seg + [v] * n_seg
    if diff is None:
        kern = functools.partial(_attn_kernel, n_seg=n_seg, dq=q_width)
    else:
        lam, subln_g, lam_init = diff
        kern = functools.partial(_diff_attn_kernel, n_seg=n_seg, dq=q_width // 2, lam_init=lam_init)
        in_specs += [pl.BlockSpec(lam.shape, lambda b, h, i: (0, 0)), pl.BlockSpec((1, dv), lambda b, h, i: (0, 0))]
        args += [lam, subln_g.reshape(1, dv)]
    return pl.pallas_call(
        kern,
        grid=(batch, heads, nq),
        in_specs=in_specs,
        out_specs=pl.BlockSpec((tq, dv), lambda b, h, i: (b * nq + i, h)),
        out_shape=jax.ShapeDtypeStruct((out_rows, heads * dv), BF16),
        compiler_params=_params(("parallel", "parallel", "arbitrary")),
        name="attention" if diff is None else "diff_attention",
    )(*args)


def _ret_kernel(dec_ref, q_ref, k_ref, v_ref, o_ref, state_ref, *, backward):
    @pl.when(pl.program_id(2) == 0)
    def _():
        state_ref[...] = jnp.zeros_like(state_ref)

    ch = q_ref.shape[0]
    ii = lax.broadcasted_iota(jnp.int32, (ch, ch), 0).astype(F32)
    jj = lax.broadcasted_iota(jnp.int32, (ch, ch), 1).astype(F32)
    pos = lax.broadcasted_iota(jnp.int32, (ch, 1), 0).astype(F32)
    if backward:
        dist, valid = jnp.maximum(jj - ii, 0.0), jj > ii
        q_pow, k_pow = ch - pos, pos
    else:
        dist, valid = jnp.maximum(ii - jj, 0.0), ii >= jj
        q_pow, k_pow = pos + 1.0, ch - 1.0 - pos
    for h in range(state_ref.shape[0]):
        lg = jax.nn.log_sigmoid(dec_ref[h])[:, :1]
        intra = jnp.where(valid, jnp.exp(lg * dist), 0.0)
        q = q_ref[:, h * RET_DK:(h + 1) * RET_DK]
        k = k_ref[:, h * RET_DK:(h + 1) * RET_DK]
        v = v_ref[:, h * RET_DV:(h + 1) * RET_DV]
        scores = lax.dot_general(q, k, (((1,), (1,)), ((), ())), preferred_element_type=F32) * intra
        state = state_ref[h]
        o = jnp.dot(scores.astype(BF16), v, preferred_element_type=F32)
        o += jnp.dot((q.astype(F32) * jnp.exp(lg * q_pow)).astype(BF16), state.astype(BF16),
                     preferred_element_type=F32)
        o_ref[:, h * RET_DV:(h + 1) * RET_DV] = o.astype(o_ref.dtype)
        kd_t = (k.astype(F32) * jnp.exp(lg * k_pow)).T.astype(BF16)
        state_ref[h] = state * jnp.exp(lg * ch) + jnp.dot(kd_t, v, preferred_element_type=F32)


def _retention(dec, q, k, v, *, backward, seq, n_ctx, batch):
    ch = RET_CHUNK
    n_c, n_s = n_ctx // ch, seq // ch
    ctx_blk0 = batch * seq // ch

    def row_block(b, t):
        if backward:
            return jnp.where(t < n_c, ctx_blk0 + b * n_c + (n_c - 1 - t), b * n_s + (n_s - 1 - (t - n_c)))
        return jnp.where(t < n_c, ctx_blk0 + b * n_c + t, b * n_s + (t - n_c))

    hb = RET_HEAD_BLOCK
    spec = lambda width: pl.BlockSpec((ch, hb * width), lambda b, h, t: (row_block(b, t), h))
    return pl.pallas_call(
        functools.partial(_ret_kernel, backward=backward),
        grid=(batch, RET_HEADS // hb, n_c + n_s),
        in_specs=[pl.BlockSpec((hb, 1, LANES), lambda b, h, t: (h, 0, 0)), spec(RET_DK), spec(RET_DK), spec(RET_DV)],
        out_specs=spec(RET_DV),
        out_shape=jax.ShapeDtypeStruct((q.shape[0], RET_HEADS * RET_DV), BF16),
        scratch_shapes=[pltpu.VMEM((hb, RET_DK, RET_DV), F32)],
        compiler_params=_params(("parallel", "parallel", "arbitrary")),
        name="retention_bwd" if backward else "retention_fwd",
    )(dec, q, k, v)


def _ret_finish_kernel(of_ref, ob_ref, gate_ref, g_ref, o_ref):
    for h in range(RET_HEADS):
        cols = slice(h * RET_DV, (h + 1) * RET_DV)
        y = of_ref[:, cols].astype(F32) + ob_ref[:, cols].astype(F32)
        o_ref[:, cols] = (gate_ref[:, cols].astype(F32) * (_rms(y, RET_DV) * g_ref[:, cols])).astype(o_ref.dtype)


def _ret_finish(o_f, o_b, gate, norm_g):
    n, w = o_f.shape
    tm = ROW_TILE
    row = pl.BlockSpec((tm, w), lambda i: (i, 0))
    return pl.pallas_call(
        _ret_finish_kernel,
        grid=(n // tm,),
        in_specs=[row, row, row, pl.BlockSpec((1, w), lambda i: (0, 0))],
        out_specs=row,
        out_shape=jax.ShapeDtypeStruct((n, w), BF16),
        compiler_params=_params(("parallel",)),
        name="ret_finish",
    )(o_f, o_b, gate, norm_g.reshape(1, w))


def _moe_kernel(te_ref, nu_ref, first_ref, nxt_ref, x_ref, wg_hbm, wu_hbm, wd_hbm, *rest, layer, has_prev):
    o_ref, stage_g, stage_u, stage_d, cur_g, cur_u, cur_d, sem = rest[1:] if has_prev else rest
    t = pl.program_id(0)

    def weight_copies(e):
        return (pltpu.make_async_copy(wg_hbm.at[layer, e], stage_g, sem.at[0]),
                pltpu.make_async_copy(wu_hbm.at[layer, e], stage_u, sem.at[1]),
                pltpu.make_async_copy(wd_hbm.at[layer, e], stage_d, sem.at[2]))

    @pl.when((t == 0) & (nu_ref[0] > 0))
    def _():
        for cp in weight_copies(te_ref[0]):
            cp.start()

    @pl.when((t < nu_ref[0]) & (first_ref[t] == 1))
    def _():
        for cp in weight_copies(te_ref[t]):
            cp.wait()
        for stage, cur in ((stage_g, cur_g), (stage_u, cur_u), (stage_d, cur_d)):
            rows = stage.shape[0] // WEIGHT_CAST_CHUNKS

            def cast(c, carry, stage=stage, cur=cur, rows=rows):
                r0 = pl.multiple_of(c * rows, rows)
                cur[pl.ds(r0, rows), :] = stage[pl.ds(r0, rows), :].astype(BF16)
                return carry

            lax.fori_loop(0, WEIGHT_CAST_CHUNKS, cast, 0)

        @pl.when(nxt_ref[t] >= 0)
        def _():
            for cp in weight_copies(nxt_ref[t]):
                cp.start()

    @pl.when(t < nu_ref[0])
    def _():
        x = x_ref[...].astype(BF16)
        a = jnp.dot(x, cur_g[...], preferred_element_type=F32)
        u = jnp.dot(x, cur_u[...], preferred_element_type=F32)
        act = (a * jax.nn.sigmoid(a) * u).astype(BF16)
        o_ref[...] = jnp.dot(act, cur_d[...], preferred_element_type=F32).astype(o_ref.dtype)

    @pl.when(t >= nu_ref[0])
    def _():
        o_ref[...] = jnp.zeros_like(o_ref)


def _moe_ffn(tile_expert, n_used, first, nxt, x_part, wg, wu, wd, layer, total_rows, tile0, y_prev):
    rp, d = x_part.shape
    f = wg.shape[3]
    tm = MOE_ROW_TILE
    any_spec = pl.BlockSpec(memory_space=pl.ANY)
    has_prev = y_prev is not None
    prev = (y_prev,) if has_prev else ()
    return pl.pallas_call(
        functools.partial(_moe_kernel, layer=layer, has_prev=has_prev),
        grid_spec=pltpu.PrefetchScalarGridSpec(
            num_scalar_prefetch=4,
            grid=(rp // tm,),
            in_specs=[pl.BlockSpec((tm, d), lambda t, *_: (t, 0)), any_spec, any_spec, any_spec]
            + [any_spec] * len(prev),
            out_specs=pl.BlockSpec((tm, d), lambda t, *_: (tile0 + t, 0)),
            scratch_shapes=[pltpu.VMEM((d, f), F32), pltpu.VMEM((d, f), F32), pltpu.VMEM((f, d), F32),
                            pltpu.VMEM((d, f), BF16), pltpu.VMEM((d, f), BF16), pltpu.VMEM((f, d), BF16),
                            pltpu.SemaphoreType.DMA((3,))],
        ),
        out_shape=jax.ShapeDtypeStruct((total_rows, d), F32),
        input_output_aliases={8: 0} if has_prev else {},
        compiler_params=_params(("arbitrary",)),
        name="moe_ffn",
    )(tile_expert, n_used, first, nxt, x_part, wg, wu, wd, *prev)


def _combine_kernel(x_ref, a_ref, b_ref, w_ref, g_ref, o_ref):
    w = w_ref[...]
    o_ref[...] = x_ref[...] + g_ref[0] * (w[:, 0:1] * a_ref[...] + w[:, 1:2] * b_ref[...])


def _combine(x, ab, w_col, gate, rows_per_group, n_groups):
    n, d = x.shape
    tm = ROW_TILE
    row = pl.BlockSpec((tm, d), lambda i: (i, 0))
    return pl.pallas_call(
        _combine_kernel,
        grid=(n // tm,),
        in_specs=[row, row, pl.BlockSpec((tm, d), lambda i: (i + n // tm, 0)),
                  pl.BlockSpec((tm, LANES), lambda i: (i, 0)), _group_spec(d, tm, rows_per_group, n_groups)],
        out_specs=row,
        out_shape=jax.ShapeDtypeStruct((n, d), F32),
        compiler_params=_params(("parallel",)),
        name="moe_combine",
    )(x, ab, ab, w_col, gate)


def _moe_layer(x, g, sh, sc, gate, rw_pad, rb_col, wg, wu, wd, layer, rows_per_group, n_groups):
    n, d = x.shape
    tm = MOE_ROW_TILE
    h, idx8, w_col, cnt = _norm_mod_route(x, g, sh, sc, rw_pad, rb_col, rows_per_group, n_groups)
    experts = jnp.arange(N_EXPERTS, dtype=jnp.int32)
    padded = (cnt[:, 0] + tm - 1) // tm * tm
    ends = jnp.cumsum(padded)
    starts = ends - padded
    e2, rank2 = idx8[0:2], idx8[2:4]
    dest = rank2 + jnp.sum(jnp.where(e2[..., None] == experts, starts, 0), axis=-1)
    r = (2 * n + N_EXPERTS * (tm - 1)) // tm * tm
    tok = jnp.tile(jnp.arange(n, dtype=jnp.int32), 2)
    src = (jnp.arange(r, dtype=jnp.int32) % n).at[dest.reshape(-1)].set(tok, unique_indices=True,
                                                                         mode="promise_in_bounds")
    tile_start = jnp.arange(r // tm, dtype=jnp.int32) * tm
    tile_expert = jnp.minimum(jnp.sum((ends[None, :] <= tile_start[:, None]).astype(jnp.int32), axis=1),
                              N_EXPERTS - 1)
    n_used = ends[-1] // tm
    change = (tile_expert[1:] != tile_expert[:-1]).astype(jnp.int32)
    later = (padded[None, :] > 0) & (experts[None, :] > experts[:, None])
    next_of = jnp.min(jnp.where(later, experts[None, :], N_EXPERTS), axis=1)
    next_tile = jnp.sum(jnp.where(next_of[:, None] == experts, starts // tm, 0), axis=1)
    n_tiles = r // tm
    y_sorted = None
    for t0, t1 in ((0, n_tiles // 2), (n_tiles // 2, n_tiles)):
        reach = jnp.where((next_of < N_EXPERTS) & (next_tile < t1), next_of, -1)
        te = tile_expert[t0:t1]
        first = jnp.concatenate([jnp.ones((1,), jnp.int32), change[t0:t1 - 1]])
        nxt = jnp.sum(jnp.where(te[:, None] == experts, reach, 0), axis=1)
        used = jnp.clip(n_used - t0, 0, t1 - t0).reshape(1)
        x_part = h.at[src[t0 * tm:t1 * tm]].get(mode="promise_in_bounds")
        y_sorted = _moe_ffn(te, used, first, nxt, x_part, wg, wu, wd, layer, r, t0, y_sorted)
    ab = y_sorted.at[dest.reshape(-1)].get(mode="promise_in_bounds")
    return _combine(x, ab, w_col, gate, rows_per_group, n_groups)


def _axial_tables(rows, rot_dim):
    n_freq = rot_dim // 4
    inv_freq = jnp.power(ROPE_BASE, -jnp.arange(n_freq, dtype=F32) / n_freq)
    row = jnp.repeat(jnp.arange(rows, dtype=F32), GRID_W)
    col = jnp.tile(jnp.arange(GRID_W, dtype=F32), rows)
    ang = jnp.concatenate([row[:, None] * inv_freq, col[:, None] * inv_freq], axis=-1)
    return jnp.cos(ang), jnp.sin(ang)


def _flat_tables(c_lat, s_lat, batch, n_ctx_rows):
    c = jnp.concatenate([jnp.tile(c_lat, (batch, 1)), jnp.broadcast_to(c_lat[:1], (n_ctx_rows, LANES))])
    s = jnp.concatenate([jnp.tile(s_lat, (batch, 1)), jnp.zeros((n_ctx_rows, LANES), F32)])
    return c, s


def _spread_pairs(a, axis):
    x1, x2 = jnp.split(a, 2, axis=axis)
    z = jnp.zeros_like(x1)
    return jnp.concatenate([x1, z, x2, z], axis=axis)


def kernel(x, c, ctx, c_ctx, ada_w, ada_b, norm_mix_g, norm_ffn_g, mla_w_in, mla_q_a_g, mla_w_q_b, mla_kv_a_g,
           mla_w_kv_b, mla_q_norm_g, mla_k_norm_g, mla_w_o, diff_w_in, diff_q_norm_g, diff_k_norm_g, diff_lambda,
           diff_subln_g, diff_w_o, ret_w_in, ret_decay_fwd, ret_decay_bwd, ret_norm_g, ret_w_o, router_w,
           router_bias, moe_w_gate, moe_w_up, moe_w_down):
    batch, seq, d = x.shape
    n_ctx = ctx.shape[1]
    depth = ada_w.shape[0]
    n_lat, n_cx = batch * seq, batch * n_ctx
    n_groups = batch + 1
    assert n_groups <= SUBLANES and seq % MM_ROW_TILE == 0 and n_cx % MM_ROW_TILE == 0
    assert seq % RET_CHUNK == 0 and n_ctx % RET_CHUNK == 0 and n_lat % n_ctx == 0 and n_ctx == ROW_TILE

    xa = jnp.concatenate([x.reshape(n_lat, d), ctx.reshape(n_cx, d)], axis=0)

    cond = jnp.concatenate([c, c_ctx[None, :], jnp.zeros((SUBLANES - n_groups, d), F32)], axis=0)
    mods = _ada_mods(cond.T, ada_w, ada_b, n_groups)

    rows = seq // GRID_W
    cos64, sin64 = _axial_tables(rows, MLA_ROPE)
    zeros32 = jnp.zeros_like(cos64)
    mla_c, mla_s = _flat_tables(jnp.concatenate([cos64, zeros32, cos64, zeros32], axis=1),
                                jnp.concatenate([-sin64, zeros32, sin64, zeros32], axis=1), batch, n_cx)
    cos128, sin128 = _axial_tables(rows, DIFF_HEAD_DIM)
    diff_c, diff_s = _flat_tables(jnp.concatenate([cos128, cos128], axis=1),
                                  jnp.concatenate([-sin128, sin128], axis=1), batch, n_cx)
    cos256, sin256 = _axial_tables(rows, RET_DK)
    ret_c, ret_s = _flat_tables(cos256, sin256, batch, n_cx)

    rw_pad = jnp.concatenate([router_w, router_w, jnp.zeros((d, LANES - 2 * N_EXPERTS), F32)], axis=1)
    rb_col = router_bias.reshape(N_EXPERTS, 1)

    for i in range(depth):
        kind, j, last = i % N_MIXERS, i // N_MIXERS, i == depth - 1
        m = mods[i].reshape(SUBLANES, 6, 1, d)
        sh_m, sc_m, g_m, sh_f, sc_f, g_f = (m[:, t] for t in range(6))
        n_all = xa.shape[0]
        h = _norm_mod(xa, norm_mix_g[i], sh_m, sc_m, seq, n_groups)
        q_rows = n_lat if last else n_all

        if kind == 0:
            w_in = jnp.concatenate([mla_w_in[j][:, :MLA_Q_LORA + MLA_KV_LORA],
                                    _spread_pairs(mla_w_in[j][:, MLA_Q_LORA + MLA_KV_LORA:], 1)], axis=1)
            z = _proj(h, w_in, 0, w_in.shape[1], out_dtype=F32, tn=w_in.shape[1])
            wq = mla_w_q_b[j].reshape(MLA_Q_LORA, MLA_HEADS, MLA_NOPE + MLA_ROPE)
            wq = jnp.concatenate([wq[..., :MLA_NOPE], _spread_pairs(wq[..., MLA_NOPE:], 2)], axis=2)
            wq = wq.reshape(MLA_Q_LORA, MLA_HEADS * MLA_HEAD_PAD).astype(BF16)
            scale = float(MLA_NOPE + MLA_ROPE) ** -0.5 * LOG2E
            qg, kg = mla_q_norm_g[j], mla_k_norm_g[j]
            q = _mla_q(z, mla_q_a_g[j], wq, qg[:MLA_NOPE] * scale,
                       *_fold_rope(mla_c, mla_s, _spread_pairs(qg[MLA_NOPE:], 0), scale), q_rows)
            k, v = _mla_kv(z, mla_kv_a_g[j], mla_w_kv_b[j].astype(BF16), kg[:MLA_NOPE],
                           *_fold_rope(mla_c, mla_s, _spread_pairs(kg[MLA_NOPE:], 0), 1.0))
            attn = functools.partial(_attention, q, k, v, heads=MLA_HEADS, q_width=MLA_HEAD_PAD,
                                     v_width=MLA_V + LANES, dv=MLA_V, seq=seq, n_ctx=n_ctx, batch=batch)
            w_o = mla_w_o[j]
        elif kind == 1:
            w_in = diff_w_in[j]
            hd = DIFF_HEADS * 2 * DIFF_HEAD_DIM
            scale = float(DIFF_HEAD_DIM) ** -0.5 * LOG2E
            q = _proj(h[:q_rows], w_in, 0, hd, mode="norm_rope",
                      extras=_fold_rope(diff_c[:q_rows], diff_s[:q_rows], diff_q_norm_g[j], scale))
            k = _proj(h, w_in, hd, hd, mode="norm_rope", extras=_fold_rope(diff_c, diff_s, diff_k_norm_g[j], 1.0))
            v = _proj(h, w_in, 2 * hd, hd)
            lam_init = 0.8 - 0.6 * math.exp(-0.3 * i)
            attn = functools.partial(_attention, q, k, v, heads=DIFF_HEADS, q_width=2 * DIFF_HEAD_DIM,
                                     v_width=2 * DIFF_HEAD_DIM, dv=2 * DIFF_HEAD_DIM, seq=seq, n_ctx=n_ctx,
                                     batch=batch,
                                     diff=(diff_lambda[j], diff_subln_g[j], lam_init))
            w_o = diff_w_o[j]
        else:
            w_in = ret_w_in[j]
            nq, nv = RET_HEADS * RET_DK, RET_HEADS * RET_DV
            q = _proj(h, w_in, 0, nq, mode="rope256", extras=(ret_c, ret_s))
            k_scale = float(RET_DK) ** -0.5
            k = _proj(h, w_in, nq, nq, mode="rope256", extras=(ret_c * k_scale, ret_s * k_scale))
            v = _proj(h, w_in, 2 * nq, nv)
            gate = _proj(h, w_in, 2 * nq + nv, nv, mode="silu")
            dec = lambda p: jnp.broadcast_to(p.astype(F32)[:, None, None], (RET_HEADS, 1, LANES))
            o_f = _retention(dec(ret_decay_fwd[j]), q, k, v, backward=False, seq=seq, n_ctx=n_ctx, batch=batch)
            o_b = _retention(dec(ret_decay_bwd[j]), q, k, v, backward=True, seq=seq, n_ctx=n_ctx, batch=batch)
            y = _ret_finish(o_f, o_b, gate, ret_norm_g[j])
            attn = None
            w_o = ret_w_o[j]

        if attn is not None:
            y = attn(lat_queries=True)
            if not last:
                y = jnp.concatenate([y, attn(lat_queries=False)], axis=0)
        xa = _out_res(y, w_o, xa, g_m, seq, n_groups, rows=n_lat if last else n_all)
        xa = _moe_layer(xa, norm_ffn_g[i], sh_f, sc_f, g_f, rw_pad, rb_col, moe_w_gate, moe_w_up, moe_w_down, i,
                        seq, n_groups)
    return xa[:n_lat].reshape(batch, seq, d)
```

```python
import functools
import math

import jax
import jax.numpy as jnp
from jax import lax
from jax.experimental import pallas as pl
from jax.experimental.pallas import tpu as pltpu

F32 = jnp.float32
BF16 = jnp.bfloat16

GRID_W = 64
ROPE_BASE = 10000.0
NORM_EPS = 1e-6
N_MIXERS = 3

MLA_HEADS = 16
MLA_Q_LORA = 512
MLA_KV_LORA = 512
MLA_NOPE = 128
MLA_ROPE = 64
MLA_V = 128
MLA_HEAD_PAD = 256

DIFF_HEADS = 8
DIFF_HEAD_DIM = 128

RET_HEADS = 8
RET_DK = 256
RET_DV = 512
RET_CHUNK = 256

N_EXPERTS = 16
N_GROUPS = 4
EXPERTS_PER_GROUP = 4
D_EXPERT = 1024

LANES = 128
SUBLANES = 8
VMEM_LIMIT = 56 * 1024 * 1024

ROW_TILE = 256
MM_ROW_TILE = 512
MM_COL_TILE = 1024
MM_COL_TILE_MAX_K = 2048
RET_HEAD_BLOCK = 4
WEIGHT_CAST_CHUNKS = 8
MOE_ROW_TILE = 256
KEY_CHUNK = 512
ATTN_Q_TILE = 512
MLA_Q_TILE = 1024
LOG2E = 1.4426950408889634
ADA_COL_TILE = 1024


def _params(sem):
    return pltpu.CompilerParams(dimension_semantics=sem, vmem_limit_bytes=VMEM_LIMIT)


def _rms(x, width):
    return x * lax.rsqrt(jnp.sum(x * x, axis=-1, keepdims=True) / width + NORM_EPS)


def _ada_kernel(ct_ref, w_ref, b_ref, o_ref, sb_ref, *, n_cond):
    d = ct_ref.shape[0]
    tn = o_ref.shape[-1]

    @pl.when((pl.program_id(0) == 0) & (pl.program_id(1) == 0))
    def _():
        ct = ct_ref[...]
        s = ct * jax.nn.sigmoid(ct)
        for r in range(n_cond):
            sb_ref[r] = jnp.broadcast_to(s[:, r:r + 1], (d, LANES))

    def body(kc, accs):
        k0 = pl.multiple_of(kc * SUBLANES, SUBLANES)
        w8 = w_ref[0, pl.ds(k0, SUBLANES), :]
        out = []
        for r in range(n_cond):
            s8 = sb_ref[r, pl.ds(k0, SUBLANES), :]
            out.append(accs[r] + w8 * jnp.concatenate([s8] * (tn // LANES), axis=1))
        return tuple(out)

    accs = lax.fori_loop(0, d // SUBLANES, body,
                         tuple(jnp.zeros((SUBLANES, tn), F32) for _ in range(n_cond)), unroll=4)
    rows = [jnp.sum(a, axis=0, keepdims=True) + b_ref[0] for a in accs]
    rows.append(jnp.zeros((SUBLANES - n_cond, tn), F32))
    o_ref[0] = jnp.concatenate(rows, axis=0)


def _ada_mods(cond_t, ada_w, ada_b, n_cond):
    depth, d, n6 = ada_w.shape
    tn = ADA_COL_TILE
    return pl.pallas_call(
        functools.partial(_ada_kernel, n_cond=n_cond),
        grid=(depth, n6 // tn),
        in_specs=[pl.BlockSpec((d, SUBLANES), lambda l, j: (0, 0)),
                  pl.BlockSpec((1, d, tn), lambda l, j: (l, 0, j)),
                  pl.BlockSpec((1, 1, tn), lambda l, j: (l, 0, j))],
        out_specs=pl.BlockSpec((1, SUBLANES, tn), lambda l, j: (l, 0, j)),
        out_shape=jax.ShapeDtypeStruct((depth, SUBLANES, n6), F32),
        scratch_shapes=[pltpu.VMEM((n_cond, d, LANES), F32)],
        compiler_params=_params(("arbitrary", "arbitrary")),
        name="ada_mods",
    )(cond_t, ada_w, ada_b.reshape(depth, 1, n6))


def _group_spec(d, tm, rows_per_group, n_groups):
    return pl.BlockSpec((1, 1, d), lambda i, *_: (jnp.minimum(i * tm // rows_per_group, n_groups - 1), 0, 0))


def _norm_mod_kernel(x_ref, g_ref, sh_ref, sc_ref, o_ref):
    x = x_ref[...]
    h = _rms(x, x.shape[-1]) * g_ref[...] * (1.0 + sc_ref[0]) + sh_ref[0]
    o_ref[...] = h.astype(o_ref.dtype)


def _norm_mod(x, g, sh, sc, rows_per_group, n_groups):
    n, d = x.shape
    tm = ROW_TILE
    gs = _group_spec(d, tm, rows_per_group, n_groups)
    return pl.pallas_call(
        _norm_mod_kernel,
        grid=(n // tm,),
        in_specs=[pl.BlockSpec((tm, d), lambda i: (i, 0)), pl.BlockSpec((1, d), lambda i: (0, 0)), gs, gs],
        out_specs=pl.BlockSpec((tm, d), lambda i: (i, 0)),
        out_shape=jax.ShapeDtypeStruct((n, d), BF16),
        compiler_params=_params(("parallel",)),
        name="norm_mod",
    )(x, g.reshape(1, d), sh, sc)


def _route_kernel(x_ref, g_ref, sh_ref, sc_ref, rw_ref, rb_ref, h_ref, idx_ref, wt_ref, cnt_ref, carry_ref,
                  rwp_ref):
    @pl.when(pl.program_id(0) == 0)
    def _():
        carry_ref[...] = jnp.zeros_like(carry_ref)
        w = rw_ref[...]
        hi = w.astype(BF16)
        lo = (w - hi.astype(F32)).astype(BF16)
        rwp_ref[...] = jnp.where(lax.broadcasted_iota(jnp.int32, w.shape, 1) < N_EXPERTS, hi, lo)

    x = x_ref[...]
    tm = x.shape[0]
    h = _rms(x, x.shape[-1]) * g_ref[...] * (1.0 + sc_ref[0]) + sh_ref[0]
    h_ref[...] = h.astype(h_ref.dtype)
    h_hi = h.astype(BF16)
    h_lo = (h - h_hi.astype(F32)).astype(BF16)
    parts = (jnp.dot(h_hi, rwp_ref[...], preferred_element_type=F32)
             + jnp.dot(h_lo, rwp_ref[...], preferred_element_type=F32)).T
    logits = parts[:N_EXPERTS] + parts[N_EXPERTS:2 * N_EXPERTS]
    scores = jax.nn.sigmoid(logits)
    sel = scores + rb_ref[...]
    sel_r = [sel[e:e + 1, :] for e in range(N_EXPERTS)]
    sc_r = [scores[e:e + 1, :] for e in range(N_EXPERTS)]

    gscore = []
    for g in range(N_GROUPS):
        a, b, c, d = sel_r[4 * g:4 * g + 4]
        hi1, lo1, hi2, lo2 = jnp.maximum(a, b), jnp.minimum(a, b), jnp.maximum(c, d), jnp.minimum(c, d)
        gscore.append(jnp.maximum(hi1, hi2) + jnp.maximum(jnp.minimum(hi1, hi2), jnp.maximum(lo1, lo2)))
    grp = jnp.zeros((1, tm), jnp.int32)
    best = gscore[0]
    for g in range(1, N_GROUPS):
        better = gscore[g] > best
        grp = jnp.where(better, g, grp)
        best = jnp.where(better, gscore[g], best)

    def pick(rows, l):
        out = rows[l]
        for g in range(1, N_GROUPS):
            out = jnp.where(grp == g, rows[4 * g + l], out)
        return out

    v = [pick(sel_r, l) for l in range(EXPERTS_PER_GROUP)]
    s = [pick(sc_r, l) for l in range(EXPERTS_PER_GROUP)]

    def first_max(vals):
        m = jnp.maximum(jnp.maximum(vals[0], vals[1]), jnp.maximum(vals[2], vals[3]))
        l = jnp.where(vals[0] == m, 0, jnp.where(vals[1] == m, 1, jnp.where(vals[2] == m, 2, 3)))
        return l

    def at(vals, l):
        return jnp.where(l == 0, vals[0], jnp.where(l == 1, vals[1], jnp.where(l == 2, vals[2], vals[3])))

    l1 = first_max(v)
    l2 = first_max([jnp.where(l1 == l, -jnp.inf, v[l]) for l in range(EXPERTS_PER_GROUP)])
    s1, s2 = at(s, l1), at(s, l2)
    tot = s1 + s2
    e1, e2 = grp * EXPERTS_PER_GROUP + l1, grp * EXPERTS_PER_GROUP + l2

    eids = lax.broadcasted_iota(jnp.int32, (N_EXPERTS, tm), 0)
    oh1, oh2 = (eids == e1).astype(F32), (eids == e2).astype(F32)
    before = (lax.broadcasted_iota(jnp.int32, (tm, tm), 0) < lax.broadcasted_iota(jnp.int32, (tm, tm), 1)).astype(BF16)
    carry = carry_ref[:, :1]
    tot1 = jnp.sum(oh1, axis=1, keepdims=True)
    pre1 = carry + jnp.dot(oh1.astype(BF16), before, preferred_element_type=F32)
    pre2 = carry + tot1 + jnp.dot(oh2.astype(BF16), before, preferred_element_type=F32)
    r1 = jnp.sum(oh1 * pre1, axis=0, keepdims=True).astype(jnp.int32)
    r2 = jnp.sum(oh2 * pre2, axis=0, keepdims=True).astype(jnp.int32)
    carry_new = carry + tot1 + jnp.sum(oh2, axis=1, keepdims=True)
    carry_ref[...] = jnp.broadcast_to(carry_new, carry_ref.shape)
    cnt_ref[...] = jnp.broadcast_to(carry_new, cnt_ref.shape).astype(jnp.int32)

    idx_ref[...] = jnp.concatenate([e1, e2, r1, r2, jnp.zeros((SUBLANES - 4, tm), jnp.int32)], axis=0)
    wt_ref[...] = jnp.concatenate([s1 / tot, s2 / tot, jnp.zeros((LANES - 2, tm), F32)], axis=0).T


def _norm_mod_route(x, g, sh, sc, rw_pad, rb_col, rows_per_group, n_groups):
    n, d = x.shape
    tm = ROW_TILE
    gs = _group_spec(d, tm, rows_per_group, n_groups)
    return pl.pallas_call(
        _route_kernel,
        grid=(n // tm,),
        in_specs=[pl.BlockSpec((tm, d), lambda i: (i, 0)), pl.BlockSpec((1, d), lambda i: (0, 0)), gs, gs,
                  pl.BlockSpec((d, LANES), lambda i: (0, 0)), pl.BlockSpec((N_EXPERTS, 1), lambda i: (0, 0))],
        out_specs=[pl.BlockSpec((tm, d), lambda i: (i, 0)),
                   pl.BlockSpec((SUBLANES, tm), lambda i: (0, i)),
                   pl.BlockSpec((tm, LANES), lambda i: (i, 0)),
                   pl.BlockSpec((N_EXPERTS, LANES), lambda i: (0, 0))],
        out_shape=[jax.ShapeDtypeStruct((n, d), F32),
                   jax.ShapeDtypeStruct((SUBLANES, n), jnp.int32),
                   jax.ShapeDtypeStruct((n, LANES), F32),
                   jax.ShapeDtypeStruct((N_EXPERTS, LANES), jnp.int32)],
        scratch_shapes=[pltpu.VMEM((N_EXPERTS, LANES), F32), pltpu.VMEM((d, LANES), BF16)],
        compiler_params=_params(("arbitrary",)),
        name="norm_mod_route",
    )(x, g.reshape(1, d), sh, sc, rw_pad, rb_col)


def _half_swap_rope(x, c, s):
    return x * c + pltpu.roll(x, LANES // 2, 1) * s


def _row_sums(sq):
    return jnp.dot(sq.astype(BF16), jnp.ones((sq.shape[1], LANES), BF16), preferred_element_type=F32)


def _fold_rope(c_t, s_t, g, scale):
    return c_t * (g * scale)[None, :], s_t * (jnp.roll(g, LANES // 2) * scale)[None, :]


def _proj_kernel(*refs, mode):
    x_ref, w_ref = refs[0], refs[1]
    o_ref, wb_ref = refs[-2], refs[-1]

    @pl.when(pl.program_id(1) == 0)
    def _():
        wb_ref[...] = w_ref[...].astype(BF16)

    acc = jnp.dot(x_ref[...], wb_ref[...], preferred_element_type=F32)
    tn = acc.shape[1]
    if mode == "plain":
        o_ref[...] = acc.astype(o_ref.dtype)
    elif mode == "silu":
        o_ref[...] = (acc * jax.nn.sigmoid(acc)).astype(o_ref.dtype)
    elif mode == "norm_rope":
        gc, gs = refs[2][...], refs[3][...]
        for j in range(tn // LANES):
            blk = acc[:, j * LANES:(j + 1) * LANES]
            blk = blk * lax.rsqrt(_row_sums(blk * blk) / LANES + NORM_EPS)
            o_ref[:, j * LANES:(j + 1) * LANES] = _half_swap_rope(blk, gc, gs).astype(o_ref.dtype)
    elif mode == "rope256":
        c, s = refs[2][...], refs[3][...]
        for j in range(tn // (2 * LANES)):
            x1 = acc[:, (2 * j) * LANES:(2 * j + 1) * LANES]
            x2 = acc[:, (2 * j + 1) * LANES:(2 * j + 2) * LANES]
            o_ref[:, (2 * j) * LANES:(2 * j + 1) * LANES] = (x1 * c - x2 * s).astype(o_ref.dtype)
            o_ref[:, (2 * j + 1) * LANES:(2 * j + 2) * LANES] = (x2 * c + x1 * s).astype(o_ref.dtype)
    else:
        raise ValueError(mode)


def _col_tile(k):
    return MM_COL_TILE if k <= MM_COL_TILE_MAX_K else MM_COL_TILE // 2


def _proj(x, w, col0, n_cols, *, mode="plain", extras=(), out_dtype=BF16, tn=None):
    n, k = x.shape
    tm = MM_ROW_TILE
    tn = min(_col_tile(k) if tn is None else tn, n_cols)
    j0 = col0 // tn
    extra_specs = []
    for e in extras:
        if e.shape[0] == 1:
            extra_specs.append(pl.BlockSpec(e.shape, lambda j, i: (0, 0)))
        else:
            extra_specs.append(pl.BlockSpec((tm, e.shape[1]), lambda j, i: (i, 0)))
    return pl.pallas_call(
        functools.partial(_proj_kernel, mode=mode),
        grid=(n_cols // tn, n // tm),
        in_specs=[pl.BlockSpec((tm, k), lambda j, i: (i, 0)),
                  pl.BlockSpec((k, tn), lambda j, i: (0, j0 + j))] + extra_specs,
        out_specs=pl.BlockSpec((tm, tn), lambda j, i: (i, j)),
        out_shape=jax.ShapeDtypeStruct((n, n_cols), out_dtype),
        scratch_shapes=[pltpu.VMEM((k, tn), BF16)],
        compiler_params=_params(("arbitrary", "arbitrary")),
        name="proj_" + mode,
    )(x, w, *extras)


def _out_res_kernel(y_ref, w_ref, x_ref, g_ref, o_ref, wb_ref):
    @pl.when(pl.program_id(1) == 0)
    def _():
        wb_ref[...] = w_ref[...].astype(BF16)

    acc = jnp.dot(y_ref[...], wb_ref[...], preferred_element_type=F32)
    o_ref[...] = x_ref[...] + g_ref[0] * acc


def _out_res(y, w, x, gate, rows_per_group, n_groups, rows):
    n, k = rows, y.shape[1]
    d = w.shape[1]
    tm = MM_ROW_TILE
    tn = _col_tile(k)
    return pl.pallas_call(
        _out_res_kernel,
        grid=(d // tn, n // tm),
        in_specs=[pl.BlockSpec((tm, k), lambda j, i: (i, 0)),
                  pl.BlockSpec((k, tn), lambda j, i: (0, j)),
                  pl.BlockSpec((tm, tn), lambda j, i: (i, j)),
                  pl.BlockSpec((1, 1, tn), lambda j, i: (jnp.minimum(i * tm // rows_per_group, n_groups - 1), 0, j))],
        out_specs=pl.BlockSpec((tm, tn), lambda j, i: (i, j)),
        out_shape=jax.ShapeDtypeStruct((n, d), F32),
        scratch_shapes=[pltpu.VMEM((k, tn), BF16)],
        compiler_params=_params(("arbitrary", "arbitrary")),
        name="out_res",
    )(y, w, x, gate)


def _mla_q_kernel(z_ref, ga_ref, w_ref, g_ref, gc_ref, gs_ref, o_ref):
    cq = z_ref[...]
    cn = (_rms(cq, cq.shape[-1]) * ga_ref[...]).astype(BF16)
    acc = jnp.dot(cn, w_ref[...], preferred_element_type=F32)
    g, gc, gs = g_ref[...], gc_ref[...], gs_ref[...]
    width = float(MLA_NOPE + MLA_ROPE)
    for j in range(acc.shape[1] // MLA_HEAD_PAD):
        qh = acc[:, j * MLA_HEAD_PAD:(j + 1) * MLA_HEAD_PAD]
        f = lax.rsqrt(_row_sums(qh * qh) / width + NORM_EPS)
        o_ref[:, j * MLA_HEAD_PAD:j * MLA_HEAD_PAD + LANES] = (qh[:, :LANES] * f * g).astype(o_ref.dtype)
        o_ref[:, j * MLA_HEAD_PAD + LANES:(j + 1) * MLA_HEAD_PAD] = _half_swap_rope(
            qh[:, LANES:] * f, gc, gs).astype(o_ref.dtype)


def _mla_q(z, q_a_g, w_q_b_pad, g_nope, gc, gs, rows, tn=1024):
    n = rows
    tm = MM_ROW_TILE
    n_out = w_q_b_pad.shape[1]
    return pl.pallas_call(
        _mla_q_kernel,
        grid=(n // tm, n_out // tn),
        in_specs=[pl.BlockSpec((tm, MLA_Q_LORA), lambda i, j: (i, 0)),
                  pl.BlockSpec((1, MLA_Q_LORA), lambda i, j: (0, 0)),
                  pl.BlockSpec((MLA_Q_LORA, tn), lambda i, j: (0, j)),
                  pl.BlockSpec((1, LANES), lambda i, j: (0, 0)),
                  pl.BlockSpec((tm, LANES), lambda i, j: (i, 0)),
                  pl.BlockSpec((tm, LANES), lambda i, j: (i, 0))],
        out_specs=pl.BlockSpec((tm, tn), lambda i, j: (i, j)),
        out_shape=jax.ShapeDtypeStruct((n, n_out), BF16),
        compiler_params=_params(("parallel", "arbitrary")),
        name="mla_q",
    )(z, q_a_g.reshape(1, -1), w_q_b_pad, g_nope.reshape(1, LANES), gc, gs)


def _mla_kv_kernel(z_ref, kr_ref, ga_ref, w_ref, g_ref, gc_ref, gs_ref, k_ref, v_ref):
    ckv = z_ref[...]
    cn = (_rms(ckv, ckv.shape[-1]) * ga_ref[...]).astype(BF16)
    acc = jnp.dot(cn, w_ref[...], preferred_element_type=F32)
    kr = kr_ref[...]
    kr_ss = _row_sums(kr * kr)
    g, gc, gs = g_ref[...], gc_ref[...], gs_ref[...]
    width = float(MLA_NOPE + MLA_ROPE)
    for j in range(acc.shape[1] // MLA_HEAD_PAD):
        kn = acc[:, j * MLA_HEAD_PAD:j * MLA_HEAD_PAD + MLA_NOPE]
        vv = acc[:, j * MLA_HEAD_PAD + MLA_NOPE:(j + 1) * MLA_HEAD_PAD]
        f = lax.rsqrt((_row_sums(kn * kn) + kr_ss) / width + NORM_EPS)
        k_ref[:, j * MLA_HEAD_PAD:j * MLA_HEAD_PAD + LANES] = (kn * f * g).astype(k_ref.dtype)
        k_ref[:, j * MLA_HEAD_PAD + LANES:(j + 1) * MLA_HEAD_PAD] = _half_swap_rope(
            kr * f, gc, gs).astype(k_ref.dtype)
        v_ref[:, j * MLA_HEAD_PAD:j * MLA_HEAD_PAD + MLA_V] = vv.astype(v_ref.dtype)
        v_ref[:, j * MLA_HEAD_PAD + MLA_V:(j + 1) * MLA_HEAD_PAD] = jnp.ones((vv.shape[0], LANES), v_ref.dtype)


def _mla_kv(z, kv_a_g, w_kv_b, g_nope, gc, gs, tn=1024):
    n = z.shape[0]
    tm = MM_ROW_TILE
    n_out = w_kv_b.shape[1]
    kr_block = (MLA_Q_LORA + MLA_KV_LORA) // LANES
    return pl.pallas_call(
        _mla_kv_kernel,
        grid=(n // tm, n_out // tn),
        in_specs=[pl.BlockSpec((tm, MLA_KV_LORA), lambda i, j: (i, 1)),
                  pl.BlockSpec((tm, LANES), lambda i, j: (i, kr_block)),
                  pl.BlockSpec((1, MLA_KV_LORA), lambda i, j: (0, 0)),
                  pl.BlockSpec((MLA_KV_LORA, tn), lambda i, j: (0, j)),
                  pl.BlockSpec((1, LANES), lambda i, j: (0, 0)),
                  pl.BlockSpec((tm, LANES), lambda i, j: (i, 0)),
                  pl.BlockSpec((tm, LANES), lambda i, j: (i, 0))],
        out_specs=[pl.BlockSpec((tm, tn), lambda i, j: (i, j)),
                   pl.BlockSpec((tm, tn), lambda i, j: (i, j))],
        out_shape=[jax.ShapeDtypeStruct((n, n_out), BF16),
                   jax.ShapeDtypeStruct((n, n_out), BF16)],
        compiler_params=_params(("parallel", "arbitrary")),
        name="mla_kv",
    )(z, z, kv_a_g.reshape(1, -1), w_kv_b, g_nope.reshape(1, LANES), gc, gs)


def _softmax_pv(q, k_refs, v_refs, c0, dq, *, den_from_v):
    chunks = []
    for k_ref, v_ref in zip(k_refs, v_refs):
        step = min(k_ref.shape[0], KEY_CHUNK)
        chunks += [(k_ref, v_ref, r0, step) for r0 in range(0, k_ref.shape[0], step)]
    m = acc = den = None
    for k_ref, v_ref, r0, step in chunks:
        s = lax.dot_general(q, k_ref[r0:r0 + step, c0:c0 + dq], (((1,), (1,)), ((), ())),
                            preferred_element_type=F32)
        mc = jnp.max(s, axis=-1, keepdims=True)
        m_new = mc if m is None else jnp.maximum(m, mc)
        p = jnp.exp2((s - m_new).astype(BF16)) if den_from_v else jnp.exp2(s - m_new)
        pv = jnp.dot(p.astype(BF16), v_ref[r0:r0 + step, :], preferred_element_type=F32)
        if m is None:
            acc = pv
            if not den_from_v:
                den = jnp.sum(p, axis=-1, keepdims=True)
        else:
            alpha = jnp.exp2(m - m_new)
            acc = acc * alpha + pv
            if not den_from_v:
                den = den * alpha + jnp.sum(p, axis=-1, keepdims=True)
        m = m_new
    if den_from_v:
        dv = acc.shape[1] - LANES
        return acc[:, :dv] / acc[:, dv:dv + 1]
    return acc / den


def _attn_kernel(*refs, n_seg, dq):
    q_ref = refs[0]
    k_refs = refs[1:1 + n_seg]
    v_refs = refs[1 + n_seg:1 + 2 * n_seg]
    o_ref = refs[-1]
    o_ref[...] = _softmax_pv(q_ref[...], k_refs, v_refs, 0, dq, den_from_v=True).astype(o_ref.dtype)


def _diff_attn_kernel(*refs, n_seg, dq, lam_init):
    q_ref = refs[0]
    k_refs = refs[1:1 + n_seg]
    v_refs = refs[1 + n_seg:1 + 2 * n_seg]
    lam_ref, g_ref, o_ref = refs[-3], refs[-2], refs[-1]
    lf = lam_ref[...]
    lam = (jnp.exp(jnp.sum(lf[0:1] * lf[1:2], axis=-1, keepdims=True))
           - jnp.exp(jnp.sum(lf[2:3] * lf[3:4], axis=-1, keepdims=True)) + lam_init)
    q = q_ref[...]
    o1 = _softmax_pv(q[:, :dq], k_refs, v_refs, 0, dq, den_from_v=False)
    o2 = _softmax_pv(q[:, dq:], k_refs, v_refs, dq, dq, den_from_v=False)
    o = o1 - lam * o2
    o_ref[...] = (_rms(o, o.shape[-1]) * g_ref[...] * (1.0 - lam_init)).astype(o_ref.dtype)


def _attention(q, k, v, *, heads, q_width, v_width, dv, seq, n_ctx, batch, lat_queries, q_tile, diff=None):
    lat_rows = batch * seq
    ctx_blk0 = lat_rows // n_ctx
    if lat_queries:
        tq = min(q_tile, seq)
        nq = seq // tq
        q_map = lambda b, h, i: (b * nq + i, h)
        segs = [(seq, lambda b, h, i: (b, h)), (n_ctx, lambda b, h, i: (ctx_blk0 + b, h))]
        out_rows = lat_rows
    else:
        tq = n_ctx
        nq = 1
        q_map = lambda b, h, i: (ctx_blk0 + b, h)
        segs = [(n_ctx, lambda b, h, i: (ctx_blk0 + b, h))]
        out_rows = batch * n_ctx
    n_seg = len(segs)
    in_specs = [pl.BlockSpec((tq, q_width), q_map)]
    in_specs += [pl.BlockSpec((rows, q_width), m) for rows, m in segs]
    in_specs += [pl.BlockSpec((rows, v_width), m) for rows, m in segs]
    args = [q] + [k] * n_seg + [v] * n_seg
    if diff is None:
        kern = functools.partial(_attn_kernel, n_seg=n_seg, dq=q_width)
    else:
        lam, subln_g, lam_init = diff
        kern = functools.partial(_diff_attn_kernel, n_seg=n_seg, dq=q_width // 2, lam_init=lam_init)
        in_specs += [pl.BlockSpec(lam.shape, lambda b, h, i: (0, 0)), pl.BlockSpec((1, dv), lambda b, h, i: (0, 0))]
        args += [lam, subln_g.reshape(1, dv)]
    return pl.pallas_call(
        kern,
        grid=(batch, heads, nq),
        in_specs=in_specs,
        out_specs=pl.BlockSpec((tq, dv), lambda b, h, i: (b * nq + i, h)),
        out_shape=jax.ShapeDtypeStruct((out_rows, heads * dv), BF16),
        compiler_params=_params(("parallel", "parallel", "arbitrary")),
        name="attention" if diff is None else "diff_attention",
    )(*args)


def _ret_kernel(dec_ref, q_ref, k_ref, v_ref, *rest, backward, finish):
    if finish:
        other_ref, gate_ref, ng_ref, o_ref, state_ref = rest
    else:
        o_ref, state_ref = rest
    @pl.when(pl.program_id(2) == 0)
    def _():
        state_ref[...] = jnp.zeros_like(state_ref)

    ch = q_ref.shape[0]
    ii = lax.broadcasted_iota(jnp.int32, (ch, ch), 0).astype(F32)
    jj = lax.broadcasted_iota(jnp.int32, (ch, ch), 1).astype(F32)
    pos = lax.broadcasted_iota(jnp.int32, (ch, 1), 0).astype(F32)
    if backward:
        dist, valid = jnp.maximum(jj - ii, 0.0), jj > ii
        q_pow, k_pow = ch - pos, pos
    else:
        dist, valid = jnp.maximum(ii - jj, 0.0), ii >= jj
        q_pow, k_pow = pos + 1.0, ch - 1.0 - pos
    for h in range(state_ref.shape[0]):
        lg = jax.nn.log_sigmoid(dec_ref[h])[:, :1]
        intra = jnp.where(valid, jnp.exp(lg * dist), 0.0)
        q = q_ref[:, h * RET_DK:(h + 1) * RET_DK]
        k = k_ref[:, h * RET_DK:(h + 1) * RET_DK]
        v = v_ref[:, h * RET_DV:(h + 1) * RET_DV]
        scores = lax.dot_general(q, k, (((1,), (1,)), ((), ())), preferred_element_type=F32) * intra
        state = state_ref[h]
        o = jnp.dot(scores.astype(BF16), v, preferred_element_type=F32)
        o += jnp.dot((q.astype(F32) * jnp.exp(lg * q_pow)).astype(BF16), state.astype(BF16),
                     preferred_element_type=F32)
        cols = slice(h * RET_DV, (h + 1) * RET_DV)
        if finish:
            y = o + other_ref[:, cols].astype(F32)
            o = gate_ref[:, cols].astype(F32) * (_rms(y, RET_DV) * ng_ref[:, cols])
        o_ref[:, cols] = o.astype(o_ref.dtype)
        kd_t = (k.astype(F32) * jnp.exp(lg * k_pow)).T.astype(BF16)
        state_ref[h] = state * jnp.exp(lg * ch) + jnp.dot(kd_t, v, preferred_element_type=F32)


def _retention(dec, q, k, v, *, backward, seq, n_ctx, batch, finish=None):
    ch = RET_CHUNK
    n_c, n_s = n_ctx // ch, seq // ch
    ctx_blk0 = batch * seq // ch

    def row_block(b, t):
        if backward:
            return jnp.where(t < n_c, ctx_blk0 + b * n_c + (n_c - 1 - t), b * n_s + (n_s - 1 - (t - n_c)))
        return jnp.where(t < n_c, ctx_blk0 + b * n_c + t, b * n_s + (t - n_c))

    hb = RET_HEAD_BLOCK
    spec = lambda width: pl.BlockSpec((ch, hb * width), lambda b, h, t: (row_block(b, t), h))
    extra_specs, extra = [], ()
    if finish is not None:
        extra_specs = [spec(RET_DV), spec(RET_DV), pl.BlockSpec((1, hb * RET_DV), lambda b, h, t: (0, h))]
        extra = tuple(finish)
    return pl.pallas_call(
        functools.partial(_ret_kernel, backward=backward, finish=finish is not None),
        grid=(batch, RET_HEADS // hb, n_c + n_s),
        in_specs=[pl.BlockSpec((hb, 1, LANES), lambda b, h, t: (h, 0, 0)), spec(RET_DK), spec(RET_DK), spec(RET_DV)]
        + extra_specs,
        out_specs=spec(RET_DV),
        out_shape=jax.ShapeDtypeStruct((q.shape[0], RET_HEADS * RET_DV), BF16),
        scratch_shapes=[pltpu.VMEM((hb, RET_DK, RET_DV), F32)],
        compiler_params=_params(("parallel", "parallel", "arbitrary")),
        name="retention_bwd" if backward else "retention_fwd",
    )(dec, q, k, v, *extra)


def _moe_kernel(te_ref, nu_ref, first_ref, nxt_ref, x_ref, wg_hbm, wu_hbm, wd_hbm, o_ref,
                stage_g, stage_u, stage_d, cur_g, cur_u, cur_d, sem, *, layer):
    t = pl.program_id(0)

    def weight_copies(e):
        return (pltpu.make_async_copy(wg_hbm.at[layer, e], stage_g, sem.at[0]),
                pltpu.make_async_copy(wu_hbm.at[layer, e], stage_u, sem.at[1]),
                pltpu.make_async_copy(wd_hbm.at[layer, e], stage_d, sem.at[2]))

    @pl.when(t == 0)
    def _():
        for cp in weight_copies(te_ref[0]):
            cp.start()

    @pl.when((t < nu_ref[0]) & (first_ref[t] == 1))
    def _():
        for cp in weight_copies(te_ref[t]):
            cp.wait()
        for stage, cur in ((stage_g, cur_g), (stage_u, cur_u), (stage_d, cur_d)):
            rows = stage.shape[0] // WEIGHT_CAST_CHUNKS

            def cast(c, carry, stage=stage, cur=cur, rows=rows):
                r0 = pl.multiple_of(c * rows, rows)
                cur[pl.ds(r0, rows), :] = stage[pl.ds(r0, rows), :].astype(BF16)
                return carry

            lax.fori_loop(0, WEIGHT_CAST_CHUNKS, cast, 0)

        @pl.when(nxt_ref[t] >= 0)
        def _():
            for cp in weight_copies(nxt_ref[t]):
                cp.start()

    @pl.when(t < nu_ref[0])
    def _():
        x = x_ref[...].astype(BF16)
        a = jnp.dot(x, cur_g[...], preferred_element_type=F32)
        u = jnp.dot(x, cur_u[...], preferred_element_type=F32)
        act = (a * jax.nn.sigmoid(a) * u).astype(BF16)
        o_ref[...] = jnp.dot(act, cur_d[...], preferred_element_type=F32).astype(o_ref.dtype)

    @pl.when(t >= nu_ref[0])
    def _():
        o_ref[...] = jnp.zeros_like(o_ref)


def _moe_ffn(tile_expert, n_used, first, nxt, x_sorted, wg, wu, wd, layer):
    r, d = x_sorted.shape
    f = wg.shape[3]
    tm = MOE_ROW_TILE
    any_spec = pl.BlockSpec(memory_space=pl.ANY)
    return pl.pallas_call(
        functools.partial(_moe_kernel, layer=layer),
        grid_spec=pltpu.PrefetchScalarGridSpec(
            num_scalar_prefetch=4,
            grid=(r // tm,),
            in_specs=[pl.BlockSpec((tm, d), lambda t, *_: (t, 0)), any_spec, any_spec, any_spec],
            out_specs=pl.BlockSpec((tm, d), lambda t, *_: (t, 0)),
            scratch_shapes=[pltpu.VMEM((d, f), F32), pltpu.VMEM((d, f), F32), pltpu.VMEM((f, d), F32),
                            pltpu.VMEM((d, f), BF16), pltpu.VMEM((d, f), BF16), pltpu.VMEM((f, d), BF16),
                            pltpu.SemaphoreType.DMA((3,))],
        ),
        out_shape=jax.ShapeDtypeStruct((r, d), F32),
        compiler_params=_params(("arbitrary",)),
        name="moe_ffn",
    )(tile_expert, n_used, first, nxt, x_sorted, wg, wu, wd)


def _combine_kernel(x_ref, a_ref, b_ref, w_ref, g_ref, *rest):
    w = w_ref[...]
    xn = x_ref[...] + g_ref[0] * (w[:, 0:1] * a_ref[...] + w[:, 1:2] * b_ref[...])
    rest[-1 if len(rest) == 1 else -2][...] = xn
    if len(rest) > 1:
        gn_ref, sh_ref, sc_ref, _, h_ref = rest
        h_ref[...] = (_rms(xn, xn.shape[-1]) * gn_ref[...] * (1.0 + sc_ref[0]) + sh_ref[0]).astype(h_ref.dtype)


def _combine(x, ab, w_col, gate, rows_per_group, n_groups, next_norm=None):
    n, d = x.shape
    tm = ROW_TILE
    row = pl.BlockSpec((tm, d), lambda i: (i, 0))
    gs = _group_spec(d, tm, rows_per_group, n_groups)
    in_specs = [row, row, pl.BlockSpec((tm, d), lambda i: (i + n // tm, 0)),
                pl.BlockSpec((tm, LANES), lambda i: (i, 0)), gs]
    args = [x, ab, ab, w_col, gate]
    out_specs, out_shape = row, jax.ShapeDtypeStruct((n, d), F32)
    if next_norm is not None:
        gn, sh, sc = next_norm
        in_specs += [pl.BlockSpec((1, d), lambda i: (0, 0)), gs, gs]
        args += [gn.reshape(1, d), sh, sc]
        out_specs, out_shape = [row, row], [out_shape, jax.ShapeDtypeStruct((n, d), BF16)]
    return pl.pallas_call(
        _combine_kernel,
        grid=(n // tm,),
        in_specs=in_specs,
        out_specs=out_specs,
        out_shape=out_shape,
        compiler_params=_params(("parallel",)),
        name="moe_combine",
    )(*args)


def _moe_layer(x, g, sh, sc, gate, rw_pad, rb_col, wg, wu, wd, layer, rows_per_group, n_groups, next_norm):
    n, d = x.shape
    tm = MOE_ROW_TILE
    h, idx8, w_col, cnt = _norm_mod_route(x, g, sh, sc, rw_pad, rb_col, rows_per_group, n_groups)
    experts = jnp.arange(N_EXPERTS, dtype=jnp.int32)
    padded = (cnt[:, 0] + tm - 1) // tm * tm
    ends = jnp.cumsum(padded)
    starts = ends - padded
    e2, rank2 = idx8[0:2], idx8[2:4]
    dest = rank2 + jnp.sum(jnp.where(e2[..., None] == experts, starts, 0), axis=-1)
    r = (2 * n + N_EXPERTS * (tm - 1)) // tm * tm
    tok = jnp.tile(jnp.arange(n, dtype=jnp.int32), 2)
    src = (jnp.arange(r, dtype=jnp.int32) % n).at[dest.reshape(-1)].set(tok, unique_indices=True,
                                                                         mode="promise_in_bounds")
    tile_start = jnp.arange(r // tm, dtype=jnp.int32) * tm
    tile_expert = jnp.minimum(jnp.sum((ends[None, :] <= tile_start[:, None]).astype(jnp.int32), axis=1),
                              N_EXPERTS - 1)
    n_used = ends[-1:] // tm
    first = jnp.concatenate([jnp.ones((1,), jnp.int32), (tile_expert[1:] != tile_expert[:-1]).astype(jnp.int32)])
    later = (padded[None, :] > 0) & (experts[None, :] > experts[:, None])
    next_of = jnp.min(jnp.where(later, experts[None, :], N_EXPERTS), axis=1)
    next_of = jnp.where(next_of < N_EXPERTS, next_of, -1)
    nxt = jnp.sum(jnp.where(tile_expert[:, None] == experts, next_of, 0), axis=1)
    x_sorted = h.at[src].get(mode="promise_in_bounds")
    y_sorted = _moe_ffn(tile_expert, n_used, first, nxt, x_sorted, wg, wu, wd, layer)
    ab = y_sorted.at[dest.reshape(-1)].get(mode="promise_in_bounds")
    return _combine(x, ab, w_col, gate, rows_per_group, n_groups, next_norm)


def _axial_tables(rows, rot_dim):
    n_freq = rot_dim // 4
    inv_freq = jnp.power(ROPE_BASE, -jnp.arange(n_freq, dtype=F32) / n_freq)
    row = jnp.repeat(jnp.arange(rows, dtype=F32), GRID_W)
    col = jnp.tile(jnp.arange(GRID_W, dtype=F32), rows)
    ang = jnp.concatenate([row[:, None] * inv_freq, col[:, None] * inv_freq], axis=-1)
    return jnp.cos(ang), jnp.sin(ang)


def _flat_tables(c_lat, s_lat, batch, n_ctx_rows):
    c = jnp.concatenate([jnp.tile(c_lat, (batch, 1)), jnp.broadcast_to(c_lat[:1], (n_ctx_rows, LANES))])
    s = jnp.concatenate([jnp.tile(s_lat, (batch, 1)), jnp.zeros((n_ctx_rows, LANES), F32)])
    return c, s


def _spread_pairs(a, axis):
    x1, x2 = jnp.split(a, 2, axis=axis)
    z = jnp.zeros_like(x1)
    return jnp.concatenate([x1, z, x2, z], axis=axis)


def kernel(x, c, ctx, c_ctx, ada_w, ada_b, norm_mix_g, norm_ffn_g, mla_w_in, mla_q_a_g, mla_w_q_b, mla_kv_a_g,
           mla_w_kv_b, mla_q_norm_g, mla_k_norm_g, mla_w_o, diff_w_in, diff_q_norm_g, diff_k_norm_g, diff_lambda,
           diff_subln_g, diff_w_o, ret_w_in, ret_decay_fwd, ret_decay_bwd, ret_norm_g, ret_w_o, router_w,
           router_bias, moe_w_gate, moe_w_up, moe_w_down):
    batch, seq, d = x.shape
    n_ctx = ctx.shape[1]
    depth = ada_w.shape[0]
    n_lat, n_cx = batch * seq, batch * n_ctx
    n_groups = batch + 1
    assert n_groups <= SUBLANES and seq % MM_ROW_TILE == 0 and n_cx % MM_ROW_TILE == 0
    assert seq % RET_CHUNK == 0 and n_ctx % RET_CHUNK == 0 and n_lat % n_ctx == 0 and n_ctx == ROW_TILE

    xa = jnp.concatenate([x.reshape(n_lat, d), ctx.reshape(n_cx, d)], axis=0)

    cond = jnp.concatenate([c, c_ctx[None, :], jnp.zeros((SUBLANES - n_groups, d), F32)], axis=0)
    mods = _ada_mods(cond.T, ada_w, ada_b, n_groups)

    rows = seq // GRID_W
    cos64, sin64 = _axial_tables(rows, MLA_ROPE)
    zeros32 = jnp.zeros_like(cos64)
    mla_c, mla_s = _flat_tables(jnp.concatenate([cos64, zeros32, cos64, zeros32], axis=1),
                                jnp.concatenate([-sin64, zeros32, sin64, zeros32], axis=1), batch, n_cx)
    cos128, sin128 = _axial_tables(rows, DIFF_HEAD_DIM)
    diff_c, diff_s = _flat_tables(jnp.concatenate([cos128, cos128], axis=1),
                                  jnp.concatenate([-sin128, sin128], axis=1), batch, n_cx)
    cos256, sin256 = _axial_tables(rows, RET_DK)
    ret_c, ret_s = _flat_tables(cos256, sin256, batch, n_cx)

    rw_pad = jnp.concatenate([router_w, router_w, jnp.zeros((d, LANES - 2 * N_EXPERTS), F32)], axis=1)
    rb_col = router_bias.reshape(N_EXPERTS, 1)

    for i in range(depth):
        kind, j, last = i % N_MIXERS, i // N_MIXERS, i == depth - 1
        m = mods[i].reshape(SUBLANES, 6, 1, d)
        sh_m, sc_m, g_m, sh_f, sc_f, g_f = (m[:, t] for t in range(6))
        n_all = xa.shape[0]
        if i == 0:
            h = _norm_mod(xa, norm_mix_g[i], sh_m, sc_m, seq, n_groups)
        q_rows = n_lat if last else n_all

        if kind == 0:
            w_in = jnp.concatenate([mla_w_in[j][:, :MLA_Q_LORA + MLA_KV_LORA],
                                    _spread_pairs(mla_w_in[j][:, MLA_Q_LORA + MLA_KV_LORA:], 1)], axis=1)
            z = _proj(h, w_in, 0, w_in.shape[1], out_dtype=F32, tn=w_in.shape[1])
            wq = mla_w_q_b[j].reshape(MLA_Q_LORA, MLA_HEADS, MLA_NOPE + MLA_ROPE)
            wq = jnp.concatenate([wq[..., :MLA_NOPE], _spread_pairs(wq[..., MLA_NOPE:], 2)], axis=2)
            wq = wq.reshape(MLA_Q_LORA, MLA_HEADS * MLA_HEAD_PAD).astype(BF16)
            scale = float(MLA_NOPE + MLA_ROPE) ** -0.5 * LOG2E
            qg, kg = mla_q_norm_g[j], mla_k_norm_g[j]
            q = _mla_q(z, mla_q_a_g[j], wq, qg[:MLA_NOPE] * scale,
                       *_fold_rope(mla_c, mla_s, _spread_pairs(qg[MLA_NOPE:], 0), scale), q_rows)
            k, v = _mla_kv(z, mla_kv_a_g[j], mla_w_kv_b[j].astype(BF16), kg[:MLA_NOPE],
                           *_fold_rope(mla_c, mla_s, _spread_pairs(kg[MLA_NOPE:], 0), 1.0))
            attn = functools.partial(_attention, q, k, v, heads=MLA_HEADS, q_width=MLA_HEAD_PAD,
                                     v_width=MLA_V + LANES, dv=MLA_V, seq=seq, n_ctx=n_ctx, batch=batch,
                                     q_tile=MLA_Q_TILE)
            w_o = mla_w_o[j]
        elif kind == 1:
            w_in = diff_w_in[j]
            hd = DIFF_HEADS * 2 * DIFF_HEAD_DIM
            scale = float(DIFF_HEAD_DIM) ** -0.5 * LOG2E
            q = _proj(h[:q_rows], w_in, 0, hd, mode="norm_rope",
                      extras=_fold_rope(diff_c[:q_rows], diff_s[:q_rows], diff_q_norm_g[j], scale))
            k = _proj(h, w_in, hd, hd, mode="norm_rope", extras=_fold_rope(diff_c, diff_s, diff_k_norm_g[j], 1.0))
            v = _proj(h, w_in, 2 * hd, hd)
            lam_init = 0.8 - 0.6 * math.exp(-0.3 * i)
            attn = functools.partial(_attention, q, k, v, heads=DIFF_HEADS, q_width=2 * DIFF_HEAD_DIM,
                                     v_width=2 * DIFF_HEAD_DIM, dv=2 * DIFF_HEAD_DIM, seq=seq, n_ctx=n_ctx,
                                     batch=batch, q_tile=ATTN_Q_TILE,
                                     diff=(diff_lambda[j], diff_subln_g[j], lam_init))
            w_o = diff_w_o[j]
        else:
            w_in = ret_w_in[j]
            nq, nv = RET_HEADS * RET_DK, RET_HEADS * RET_DV
            q = _proj(h, w_in, 0, nq, mode="rope256", extras=(ret_c, ret_s))
            k_scale = float(RET_DK) ** -0.5
            k = _proj(h, w_in, nq, nq, mode="rope256", extras=(ret_c * k_scale, ret_s * k_scale))
            v = _proj(h, w_in, 2 * nq, nv)
            gate = _proj(h, w_in, 2 * nq + nv, nv, mode="silu")
            dec = lambda p: jnp.broadcast_to(p.astype(F32)[:, None, None], (RET_HEADS, 1, LANES))
            o_b = _retention(dec(ret_decay_bwd[j]), q, k, v, backward=True, seq=seq, n_ctx=n_ctx, batch=batch)
            y = _retention(dec(ret_decay_fwd[j]), q, k, v, backward=False, seq=seq, n_ctx=n_ctx, batch=batch,
                           finish=(o_b, gate, ret_norm_g[j].reshape(1, -1)))
            attn = None
            w_o = ret_w_o[j]

        if attn is not None:
            y = attn(lat_queries=True)
            if not last:
                y = jnp.concatenate([y, attn(lat_queries=False)], axis=0)
        xa = _out_res(y, w_o, xa, g_m, seq, n_groups, rows=n_lat if last else n_all)
        next_norm = None
        if not last:
            m_next = mods[i + 1].reshape(SUBLANES, 6, 1, d)
            next_norm = (norm_mix_g[i + 1], m_next[:, 0], m_next[:, 1])
        res = _moe_layer(xa, norm_ffn_g[i], sh_f, sc_f, g_f, rw_pad, rb_col, moe_w_gate, moe_w_up, moe_w_down, i,
                         seq, n_groups, next_norm)
        xa, h = (res, None) if last else res
    return xa[:n_lat].reshape(batch, seq, d)
```

```python
import functools
import math

import jax
import jax.numpy as jnp
from jax import lax
from jax.experimental import pallas as pl
from jax.experimental.pallas import tpu as pltpu

F32 = jnp.float32
BF16 = jnp.bfloat16

GRID_W = 64
ROPE_BASE = 10000.0
NORM_EPS = 1e-6
N_MIXERS = 3

MLA_HEADS = 16
MLA_Q_LORA = 512
MLA_KV_LORA = 512
MLA_NOPE = 128
MLA_ROPE = 64
MLA_V = 128
MLA_HEAD_PAD = 256

DIFF_HEADS = 8
DIFF_HEAD_DIM = 128

RET_HEADS = 8
RET_DK = 256
RET_DV = 512
RET_CHUNK = 256

N_EXPERTS = 16
N_GROUPS = 4
EXPERTS_PER_GROUP = 4
D_EXPERT = 1024

LANES = 128
SUBLANES = 8
VMEM_LIMIT = 56 * 1024 * 1024

ROW_TILE = 256
MM_ROW_TILE = 512
MM_COL_TILE = 1024
MM_COL_TILE_MAX_K = 2048
RET_HEAD_BLOCK = 4
WEIGHT_CAST_CHUNKS = 8
MOE_ROW_TILE = 256
KEY_CHUNK = 512
ATTN_Q_TILE = 512
MLA_Q_TILE = 1024
LOG2E = 1.4426950408889634
ADA_COL_TILE = 1024


def _params(sem):
    return pltpu.CompilerParams(dimension_semantics=sem, vmem_limit_bytes=VMEM_LIMIT)


def _rms(x, width):
    return x * lax.rsqrt(jnp.sum(x * x, axis=-1, keepdims=True) / width + NORM_EPS)


def _ada_kernel(ct_ref, w_ref, b_ref, o_ref, sb_ref, *, n_cond):
    d = ct_ref.shape[0]
    tn = o_ref.shape[-1]

    @pl.when((pl.program_id(0) == 0) & (pl.program_id(1) == 0))
    def _():
        ct = ct_ref[...]
        s = ct * jax.nn.sigmoid(ct)
        for r in range(n_cond):
            sb_ref[r] = jnp.broadcast_to(s[:, r:r + 1], (d, LANES))

    def body(kc, accs):
        k0 = pl.multiple_of(kc * SUBLANES, SUBLANES)
        w8 = w_ref[0, pl.ds(k0, SUBLANES), :]
        out = []
        for r in range(n_cond):
            s8 = sb_ref[r, pl.ds(k0, SUBLANES), :]
            out.append(accs[r] + w8 * jnp.concatenate([s8] * (tn // LANES), axis=1))
        return tuple(out)

    accs = lax.fori_loop(0, d // SUBLANES, body,
                         tuple(jnp.zeros((SUBLANES, tn), F32) for _ in range(n_cond)), unroll=4)
    rows = [jnp.sum(a, axis=0, keepdims=True) + b_ref[0] for a in accs]
    rows.append(jnp.zeros((SUBLANES - n_cond, tn), F32))
    o_ref[0] = jnp.concatenate(rows, axis=0)


def _ada_mods(cond_t, ada_w, ada_b, n_cond):
    depth, d, n6 = ada_w.shape
    tn = ADA_COL_TILE
    return pl.pallas_call(
        functools.partial(_ada_kernel, n_cond=n_cond),
        grid=(depth, n6 // tn),
        in_specs=[pl.BlockSpec((d, SUBLANES), lambda l, j: (0, 0)),
                  pl.BlockSpec((1, d, tn), lambda l, j: (l, 0, j)),
                  pl.BlockSpec((1, 1, tn), lambda l, j: (l, 0, j))],
        out_specs=pl.BlockSpec((1, SUBLANES, tn), lambda l, j: (l, 0, j)),
        out_shape=jax.ShapeDtypeStruct((depth, SUBLANES, n6), F32),
        scratch_shapes=[pltpu.VMEM((n_cond, d, LANES), F32)],
        compiler_params=_params(("arbitrary", "arbitrary")),
        name="ada_mods",
    )(cond_t, ada_w, ada_b.reshape(depth, 1, n6))


def _group_spec(d, tm, rows_per_group, n_groups):
    return pl.BlockSpec((1, 1, d), lambda i, *_: (jnp.minimum(i * tm // rows_per_group, n_groups - 1), 0, 0))


def _split_rows_specs(head_rows, tm, width, row_axis):
    head_tiles = head_rows // tm
    pick = lambda idx: idx[row_axis]
    head = pl.BlockSpec((tm, width), lambda *idx: (jnp.minimum(pick(idx), head_tiles - 1), 0))
    tail = pl.BlockSpec((tm, width), lambda *idx: (jnp.maximum(pick(idx) - head_tiles, 0), 0))
    return head, tail


def _norm_mod_kernel(x_ref, x2_ref, g_ref, sh_ref, sc_ref, o_ref, *, head_tiles):
    x = jnp.where(pl.program_id(0) < head_tiles, x_ref[...], x2_ref[...])
    h = _rms(x, x.shape[-1]) * g_ref[...] * (1.0 + sc_ref[0]) + sh_ref[0]
    o_ref[...] = h.astype(o_ref.dtype)


def _norm_mod(x, x2, g, sh, sc, rows_per_group, n_groups):
    n, d = x.shape[0] + x2.shape[0], x.shape[1]
    tm = ROW_TILE
    gs = _group_spec(d, tm, rows_per_group, n_groups)
    return pl.pallas_call(
        functools.partial(_norm_mod_kernel, head_tiles=x.shape[0] // tm),
        grid=(n // tm,),
        in_specs=[*_split_rows_specs(x.shape[0], tm, d, 0), pl.BlockSpec((1, d), lambda i: (0, 0)), gs, gs],
        out_specs=pl.BlockSpec((tm, d), lambda i: (i, 0)),
        out_shape=jax.ShapeDtypeStruct((n, d), BF16),
        compiler_params=_params(("parallel",)),
        name="norm_mod",
    )(x, x2, g.reshape(1, d), sh, sc)


def _route_kernel(x_ref, g_ref, sh_ref, sc_ref, rw_ref, rb_ref, h_ref, idx_ref, wt_ref, cnt_ref, carry_ref,
                  rwp_ref):
    @pl.when(pl.program_id(0) == 0)
    def _():
        carry_ref[...] = jnp.zeros_like(carry_ref)
        w = rw_ref[...]
        hi = w.astype(BF16)
        lo = (w - hi.astype(F32)).astype(BF16)
        rwp_ref[...] = jnp.where(lax.broadcasted_iota(jnp.int32, w.shape, 1) < N_EXPERTS, hi, lo)

    x = x_ref[...]
    tm = x.shape[0]
    h = _rms(x, x.shape[-1]) * g_ref[...] * (1.0 + sc_ref[0]) + sh_ref[0]
    h_ref[...] = h.astype(h_ref.dtype)
    h_hi = h.astype(BF16)
    h_lo = (h - h_hi.astype(F32)).astype(BF16)
    parts = (jnp.dot(h_hi, rwp_ref[...], preferred_element_type=F32)
             + jnp.dot(h_lo, rwp_ref[...], preferred_element_type=F32)).T
    logits = parts[:N_EXPERTS] + parts[N_EXPERTS:2 * N_EXPERTS]
    scores = jax.nn.sigmoid(logits)
    sel = scores + rb_ref[...]
    sel_r = [sel[e:e + 1, :] for e in range(N_EXPERTS)]
    sc_r = [scores[e:e + 1, :] for e in range(N_EXPERTS)]

    gscore = []
    for g in range(N_GROUPS):
        a, b, c, d = sel_r[4 * g:4 * g + 4]
        hi1, lo1, hi2, lo2 = jnp.maximum(a, b), jnp.minimum(a, b), jnp.maximum(c, d), jnp.minimum(c, d)
        gscore.append(jnp.maximum(hi1, hi2) + jnp.maximum(jnp.minimum(hi1, hi2), jnp.maximum(lo1, lo2)))
    grp = jnp.zeros((1, tm), jnp.int32)
    best = gscore[0]
    for g in range(1, N_GROUPS):
        better = gscore[g] > best
        grp = jnp.where(better, g, grp)
        best = jnp.where(better, gscore[g], best)

    def pick(rows, l):
        out = rows[l]
        for g in range(1, N_GROUPS):
            out = jnp.where(grp == g, rows[4 * g + l], out)
        return out

    v = [pick(sel_r, l) for l in range(EXPERTS_PER_GROUP)]
    s = [pick(sc_r, l) for l in range(EXPERTS_PER_GROUP)]

    def first_max(vals):
        m = jnp.maximum(jnp.maximum(vals[0], vals[1]), jnp.maximum(vals[2], vals[3]))
        l = jnp.where(vals[0] == m, 0, jnp.where(vals[1] == m, 1, jnp.where(vals[2] == m, 2, 3)))
        return l

    def at(vals, l):
        return jnp.where(l == 0, vals[0], jnp.where(l == 1, vals[1], jnp.where(l == 2, vals[2], vals[3])))

    l1 = first_max(v)
    l2 = first_max([jnp.where(l1 == l, -jnp.inf, v[l]) for l in range(EXPERTS_PER_GROUP)])
    s1, s2 = at(s, l1), at(s, l2)
    tot = s1 + s2
    e1, e2 = grp * EXPERTS_PER_GROUP + l1, grp * EXPERTS_PER_GROUP + l2

    eids = lax.broadcasted_iota(jnp.int32, (N_EXPERTS, tm), 0)
    oh1, oh2 = (eids == e1).astype(F32), (eids == e2).astype(F32)
    before = (lax.broadcasted_iota(jnp.int32, (tm, tm), 0) < lax.broadcasted_iota(jnp.int32, (tm, tm), 1)).astype(BF16)
    carry = carry_ref[:, :1]
    tot1 = jnp.sum(oh1, axis=1, keepdims=True)
    pre1 = carry + jnp.dot(oh1.astype(BF16), before, preferred_element_type=F32)
    pre2 = carry + tot1 + jnp.dot(oh2.astype(BF16), before, preferred_element_type=F32)
    r1 = jnp.sum(oh1 * pre1, axis=0, keepdims=True).astype(jnp.int32)
    r2 = jnp.sum(oh2 * pre2, axis=0, keepdims=True).astype(jnp.int32)
    carry_new = carry + tot1 + jnp.sum(oh2, axis=1, keepdims=True)
    carry_ref[...] = jnp.broadcast_to(carry_new, carry_ref.shape)
    cnt_ref[...] = jnp.broadcast_to(carry_new, cnt_ref.shape).astype(jnp.int32)

    idx_ref[...] = jnp.concatenate([e1, e2, r1, r2, jnp.zeros((SUBLANES - 4, tm), jnp.int32)], axis=0)
    wt_ref[...] = jnp.concatenate([s1 / tot, s2 / tot, jnp.zeros((LANES - 2, tm), F32)], axis=0).T


def _norm_mod_route(x, g, sh, sc, rw_pad, rb_col, rows_per_group, n_groups):
    n, d = x.shape
    tm = ROW_TILE
    gs = _group_spec(d, tm, rows_per_group, n_groups)
    return pl.pallas_call(
        _route_kernel,
        grid=(n // tm,),
        in_specs=[pl.BlockSpec((tm, d), lambda i: (i, 0)), pl.BlockSpec((1, d), lambda i: (0, 0)), gs, gs,
                  pl.BlockSpec((d, LANES), lambda i: (0, 0)), pl.BlockSpec((N_EXPERTS, 1), lambda i: (0, 0))],
        out_specs=[pl.BlockSpec((tm, d), lambda i: (i, 0)),
                   pl.BlockSpec((SUBLANES, tm), lambda i: (0, i)),
                   pl.BlockSpec((tm, LANES), lambda i: (i, 0)),
                   pl.BlockSpec((N_EXPERTS, LANES), lambda i: (0, 0))],
        out_shape=[jax.ShapeDtypeStruct((n, d), F32),
                   jax.ShapeDtypeStruct((SUBLANES, n), jnp.int32),
                   jax.ShapeDtypeStruct((n, LANES), F32),
                   jax.ShapeDtypeStruct((N_EXPERTS, LANES), jnp.int32)],
        scratch_shapes=[pltpu.VMEM((N_EXPERTS, LANES), F32), pltpu.VMEM((d, LANES), BF16)],
        compiler_params=_params(("arbitrary",)),
        name="norm_mod_route",
    )(x, g.reshape(1, d), sh, sc, rw_pad, rb_col)


def _half_swap_rope(x, c, s):
    return x * c + pltpu.roll(x, LANES // 2, 1) * s


def _row_sums(sq):
    return jnp.dot(sq.astype(BF16), jnp.ones((sq.shape[1], LANES), BF16), preferred_element_type=F32)


def _fold_rope(c_t, s_t, g, scale):
    return c_t * (g * scale)[None, :], s_t * (jnp.roll(g, LANES // 2) * scale)[None, :]


def _proj_kernel(*refs, mode):
    x_ref, w_ref = refs[0], refs[1]
    o_ref, wb_ref = refs[-2], refs[-1]

    @pl.when(pl.program_id(1) == 0)
    def _():
        wb_ref[...] = w_ref[...].astype(BF16)

    acc = jnp.dot(x_ref[...], wb_ref[...], preferred_element_type=F32)
    tn = acc.shape[1]
    if mode == "plain":
        o_ref[...] = acc.astype(o_ref.dtype)
    elif mode == "silu":
        o_ref[...] = (acc * jax.nn.sigmoid(acc)).astype(o_ref.dtype)
    elif mode == "norm_rope":
        gc, gs = refs[2][...], refs[3][...]
        for j in range(tn // LANES):
            blk = acc[:, j * LANES:(j + 1) * LANES]
            blk = blk * lax.rsqrt(_row_sums(blk * blk) / LANES + NORM_EPS)
            o_ref[:, j * LANES:(j + 1) * LANES] = _half_swap_rope(blk, gc, gs).astype(o_ref.dtype)
    elif mode == "rope256":
        c, s = refs[2][...], refs[3][...]
        for j in range(tn // (2 * LANES)):
            x1 = acc[:, (2 * j) * LANES:(2 * j + 1) * LANES]
            x2 = acc[:, (2 * j + 1) * LANES:(2 * j + 2) * LANES]
            o_ref[:, (2 * j) * LANES:(2 * j + 1) * LANES] = (x1 * c - x2 * s).astype(o_ref.dtype)
            o_ref[:, (2 * j + 1) * LANES:(2 * j + 2) * LANES] = (x2 * c + x1 * s).astype(o_ref.dtype)
    else:
        raise ValueError(mode)


def _col_tile(k):
    return MM_COL_TILE if k <= MM_COL_TILE_MAX_K else MM_COL_TILE // 2


def _proj(x, w, col0, n_cols, *, mode="plain", extras=(), out_dtype=BF16, tn=None):
    n, k = x.shape
    tm = MM_ROW_TILE
    tn = min(_col_tile(k) if tn is None else tn, n_cols)
    j0 = col0 // tn
    extra_specs = []
    for e in extras:
        if e.shape[0] == 1:
            extra_specs.append(pl.BlockSpec(e.shape, lambda j, i: (0, 0)))
        else:
            extra_specs.append(pl.BlockSpec((tm, e.shape[1]), lambda j, i: (i, 0)))
    return pl.pallas_call(
        functools.partial(_proj_kernel, mode=mode),
        grid=(n_cols // tn, n // tm),
        in_specs=[pl.BlockSpec((tm, k), lambda j, i: (i, 0)),
                  pl.BlockSpec((k, tn), lambda j, i: (0, j0 + j))] + extra_specs,
        out_specs=pl.BlockSpec((tm, tn), lambda j, i: (i, j)),
        out_shape=jax.ShapeDtypeStruct((n, n_cols), out_dtype),
        scratch_shapes=[pltpu.VMEM((k, tn), BF16)],
        compiler_params=_params(("arbitrary", "arbitrary")),
        name="proj_" + mode,
    )(x, w, *extras)


def _out_res_kernel(*refs, y_head_tiles, x_head_tiles):
    refs = list(refs)
    i = pl.program_id(1)
    y = refs.pop(0)[...] if y_head_tiles is None else jnp.where(i < y_head_tiles, refs.pop(0)[...], refs.pop(0)[...])
    w_ref = refs.pop(0)
    x = refs.pop(0)[...] if x_head_tiles is None else jnp.where(i < x_head_tiles, refs.pop(0)[...], refs.pop(0)[...])
    g_ref, o_ref, wb_ref = refs

    @pl.when(i == 0)
    def _():
        wb_ref[...] = w_ref[...].astype(BF16)

    o_ref[...] = x + g_ref[0] * jnp.dot(y, wb_ref[...], preferred_element_type=F32)


def _out_res(y, w, x, gate, rows_per_group, n_groups, rows, y_tail=None, x_tail=None):
    n, k = rows, y.shape[1]
    d = w.shape[1]
    tm = MM_ROW_TILE
    tn = _col_tile(k)
    y_specs = [pl.BlockSpec((tm, k), lambda j, i: (i, 0))] if y_tail is None else list(
        _split_rows_specs(y.shape[0], tm, k, 1))
    if x_tail is None:
        x_specs = [pl.BlockSpec((tm, tn), lambda j, i: (i, j))]
    else:
        xt = x.shape[0] // tm
        x_specs = [pl.BlockSpec((tm, tn), lambda j, i: (jnp.minimum(i, xt - 1), j)),
                   pl.BlockSpec((tm, tn), lambda j, i: (jnp.maximum(i - xt, 0), j))]
    args = [y] + ([] if y_tail is None else [y_tail]) + [w, x] + ([] if x_tail is None else [x_tail]) + [gate]
    return pl.pallas_call(
        functools.partial(_out_res_kernel, y_head_tiles=None if y_tail is None else y.shape[0] // tm,
                          x_head_tiles=None if x_tail is None else x.shape[0] // tm),
        grid=(d // tn, n // tm),
        in_specs=y_specs + [pl.BlockSpec((k, tn), lambda j, i: (0, j))] + x_specs
        + [pl.BlockSpec((1, 1, tn), lambda j, i: (jnp.minimum(i * tm // rows_per_group, n_groups - 1), 0, j))],
        out_specs=pl.BlockSpec((tm, tn), lambda j, i: (i, j)),
        out_shape=jax.ShapeDtypeStruct((n, d), F32),
        scratch_shapes=[pltpu.VMEM((k, tn), BF16)],
        compiler_params=_params(("arbitrary", "arbitrary")),
        name="out_res",
    )(*args)


def _mla_q_kernel(z_ref, ga_ref, w_ref, g_ref, gc_ref, gs_ref, o_ref):
    cq = z_ref[...]
    cn = (_rms(cq, cq.shape[-1]) * ga_ref[...]).astype(BF16)
    acc = jnp.dot(cn, w_ref[...], preferred_element_type=F32)
    g, gc, gs = g_ref[...], gc_ref[...], gs_ref[...]
    width = float(MLA_NOPE + MLA_ROPE)
    for j in range(acc.shape[1] // MLA_HEAD_PAD):
        qh = acc[:, j * MLA_HEAD_PAD:(j + 1) * MLA_HEAD_PAD]
        f = lax.rsqrt(_row_sums(qh * qh) / width + NORM_EPS)
        o_ref[:, j * MLA_HEAD_PAD:j * MLA_HEAD_PAD + LANES] = (qh[:, :LANES] * f * g).astype(o_ref.dtype)
        o_ref[:, j * MLA_HEAD_PAD + LANES:(j + 1) * MLA_HEAD_PAD] = _half_swap_rope(
            qh[:, LANES:] * f, gc, gs).astype(o_ref.dtype)


def _mla_q(z, q_a_g, w_q_b_pad, g_nope, gc, gs, rows, tn=1024):
    n = rows
    tm = MM_ROW_TILE
    n_out = w_q_b_pad.shape[1]
    return pl.pallas_call(
        _mla_q_kernel,
        grid=(n // tm, n_out // tn),
        in_specs=[pl.BlockSpec((tm, MLA_Q_LORA), lambda i, j: (i, 0)),
                  pl.BlockSpec((1, MLA_Q_LORA), lambda i, j: (0, 0)),
                  pl.BlockSpec((MLA_Q_LORA, tn), lambda i, j: (0, j)),
                  pl.BlockSpec((1, LANES), lambda i, j: (0, 0)),
                  pl.BlockSpec((tm, LANES), lambda i, j: (i, 0)),
                  pl.BlockSpec((tm, LANES), lambda i, j: (i, 0))],
        out_specs=pl.BlockSpec((tm, tn), lambda i, j: (i, j)),
        out_shape=jax.ShapeDtypeStruct((n, n_out), BF16),
        compiler_params=_params(("parallel", "arbitrary")),
        name="mla_q",
    )(z, q_a_g.reshape(1, -1), w_q_b_pad, g_nope.reshape(1, LANES), gc, gs)


def _mla_kv_kernel(z_ref, kr_ref, ga_ref, w_ref, g_ref, gc_ref, gs_ref, k_ref, v_ref):
    ckv = z_ref[...]
    cn = (_rms(ckv, ckv.shape[-1]) * ga_ref[...]).astype(BF16)
    acc = jnp.dot(cn, w_ref[...], preferred_element_type=F32)
    kr = kr_ref[...]
    kr_ss = _row_sums(kr * kr)
    g, gc, gs = g_ref[...], gc_ref[...], gs_ref[...]
    width = float(MLA_NOPE + MLA_ROPE)
    for j in range(acc.shape[1] // MLA_HEAD_PAD):
        kn = acc[:, j * MLA_HEAD_PAD:j * MLA_HEAD_PAD + MLA_NOPE]
        vv = acc[:, j * MLA_HEAD_PAD + MLA_NOPE:(j + 1) * MLA_HEAD_PAD]
        f = lax.rsqrt((_row_sums(kn * kn) + kr_ss) / width + NORM_EPS)
        k_ref[:, j * MLA_HEAD_PAD:j * MLA_HEAD_PAD + LANES] = (kn * f * g).astype(k_ref.dtype)
        k_ref[:, j * MLA_HEAD_PAD + LANES:(j + 1) * MLA_HEAD_PAD] = _half_swap_rope(
            kr * f, gc, gs).astype(k_ref.dtype)
        v_ref[:, j * MLA_HEAD_PAD:j * MLA_HEAD_PAD + MLA_V] = vv.astype(v_ref.dtype)
        v_ref[:, j * MLA_HEAD_PAD + MLA_V:(j + 1) * MLA_HEAD_PAD] = jnp.ones((vv.shape[0], LANES), v_ref.dtype)


def _mla_kv(z, kv_a_g, w_kv_b, g_nope, gc, gs, tn=1024):
    n = z.shape[0]
    tm = MM_ROW_TILE
    n_out = w_kv_b.shape[1]
    kr_block = (MLA_Q_LORA + MLA_KV_LORA) // LANES
    return pl.pallas_call(
        _mla_kv_kernel,
        grid=(n // tm, n_out // tn),
        in_specs=[pl.BlockSpec((tm, MLA_KV_LORA), lambda i, j: (i, 1)),
                  pl.BlockSpec((tm, LANES), lambda i, j: (i, kr_block)),
                  pl.BlockSpec((1, MLA_KV_LORA), lambda i, j: (0, 0)),
                  pl.BlockSpec((MLA_KV_LORA, tn), lambda i, j: (0, j)),
                  pl.BlockSpec((1, LANES), lambda i, j: (0, 0)),
                  pl.BlockSpec((tm, LANES), lambda i, j: (i, 0)),
                  pl.BlockSpec((tm, LANES), lambda i, j: (i, 0))],
        out_specs=[pl.BlockSpec((tm, tn), lambda i, j: (i, j)),
                   pl.BlockSpec((tm, tn), lambda i, j: (i, j))],
        out_shape=[jax.ShapeDtypeStruct((n, n_out), BF16),
                   jax.ShapeDtypeStruct((n, n_out), BF16)],
        compiler_params=_params(("parallel", "arbitrary")),
        name="mla_kv",
    )(z, z, kv_a_g.reshape(1, -1), w_kv_b, g_nope.reshape(1, LANES), gc, gs)


def _softmax_pv(q, k_refs, v_refs, c0, dq, *, den_from_v):
    chunks = []
    for k_ref, v_ref in zip(k_refs, v_refs):
        step = min(k_ref.shape[0], KEY_CHUNK)
        chunks += [(k_ref, v_ref, r0, step) for r0 in range(0, k_ref.shape[0], step)]
    m = acc = den = None
    for k_ref, v_ref, r0, step in chunks:
        s = lax.dot_general(q, k_ref[r0:r0 + step, c0:c0 + dq], (((1,), (1,)), ((), ())),
                            preferred_element_type=F32)
        mc = jnp.max(s, axis=-1, keepdims=True)
        m_new = mc if m is None else jnp.maximum(m, mc)
        p = jnp.exp2((s - m_new).astype(BF16)) if den_from_v else jnp.exp2(s - m_new)
        pv = jnp.dot(p.astype(BF16), v_ref[r0:r0 + step, :], preferred_element_type=F32)
        if m is None:
            acc = pv
            if not den_from_v:
                den = jnp.sum(p, axis=-1, keepdims=True)
        else:
            alpha = jnp.exp2(m - m_new)
            acc = acc * alpha + pv
            if not den_from_v:
                den = den * alpha + jnp.sum(p, axis=-1, keepdims=True)
        m = m_new
    if den_from_v:
        dv = acc.shape[1] - LANES
        return acc[:, :dv] / acc[:, dv:dv + 1]
    return acc / den


def _attn_kernel(*refs, n_seg, dq):
    q_ref = refs[0]
    k_refs = refs[1:1 + n_seg]
    v_refs = refs[1 + n_seg:1 + 2 * n_seg]
    o_ref = refs[-1]
    o_ref[...] = _softmax_pv(q_ref[...], k_refs, v_refs, 0, dq, den_from_v=True).astype(o_ref.dtype)


def _diff_attn_kernel(*refs, n_seg, dq, lam_init):
    q_ref = refs[0]
    k_refs = refs[1:1 + n_seg]
    v_refs = refs[1 + n_seg:1 + 2 * n_seg]
    lam_ref, g_ref, o_ref = refs[-3], refs[-2], refs[-1]
    lf = lam_ref[...]
    lam = (jnp.exp(jnp.sum(lf[0:1] * lf[1:2], axis=-1, keepdims=True))
           - jnp.exp(jnp.sum(lf[2:3] * lf[3:4], axis=-1, keepdims=True)) + lam_init)
    q = q_ref[...]
    o1 = _softmax_pv(q[:, :dq], k_refs, v_refs, 0, dq, den_from_v=False)
    o2 = _softmax_pv(q[:, dq:], k_refs, v_refs, dq, dq, den_from_v=False)
    o = o1 - lam * o2
    o_ref[...] = (_rms(o, o.shape[-1]) * g_ref[...] * (1.0 - lam_init)).astype(o_ref.dtype)


def _attention(q, k, v, *, heads, q_width, v_width, dv, seq, n_ctx, batch, lat_queries, q_tile, diff=None):
    lat_rows = batch * seq
    ctx_blk0 = lat_rows // n_ctx
    if lat_queries:
        tq = min(q_tile, seq)
        nq = seq // tq
        q_map = lambda b, h, i: (b * nq + i, h)
        segs = [(seq, lambda b, h, i: (b, h)), (n_ctx, lambda b, h, i: (ctx_blk0 + b, h))]
        out_rows = lat_rows
    else:
        tq = n_ctx
        nq = 1
        q_map = lambda b, h, i: (ctx_blk0 + b, h)
        segs = [(n_ctx, lambda b, h, i: (ctx_blk0 + b, h))]
        out_rows = batch * n_ctx
    n_seg = len(segs)
    in_specs = [pl.BlockSpec((tq, q_width), q_map)]
    in_specs += [pl.BlockSpec((rows, q_width), m) for rows, m in segs]
    in_specs += [pl.BlockSpec((rows, v_width), m) for rows, m in segs]
    args = [q] + [k] * n_seg + [v] * n_seg
    if diff is None:
        kern = functools.partial(_attn_kernel, n_seg=n_seg, dq=q_width)
    else:
        lam, subln_g, lam_init = diff
        kern = functools.partial(_diff_attn_kernel, n_seg=n_seg, dq=q_width // 2, lam_init=lam_init)
        in_specs += [pl.BlockSpec(lam.shape, lambda b, h, i: (0, 0)), pl.BlockSpec((1, dv), lambda b, h, i: (0, 0))]
        args += [lam, subln_g.reshape(1, dv)]
    return pl.pallas_call(
        kern,
        grid=(batch, heads, nq),
        in_specs=in_specs,
        out_specs=pl.BlockSpec((tq, dv), lambda b, h, i: (b * nq + i, h)),
        out_shape=jax.ShapeDtypeStruct((out_rows, heads * dv), BF16),
        compiler_params=_params(("parallel", "parallel", "arbitrary")),
        name="attention" if diff is None else "diff_attention",
    )(*args)


def _ret_kernel(dec_ref, q_ref, k_ref, v_ref, *rest, backward, finish):
    if finish:
        other_ref, gate_ref, ng_ref, o_ref, state_ref = rest
    else:
        o_ref, state_ref = rest
    @pl.when(pl.program_id(2) == 0)
    def _():
        state_ref[...] = jnp.zeros_like(state_ref)

    ch = q_ref.shape[0]
    ii = lax.broadcasted_iota(jnp.int32, (ch, ch), 0).astype(F32)
    jj = lax.broadcasted_iota(jnp.int32, (ch, ch), 1).astype(F32)
    pos = lax.broadcasted_iota(jnp.int32, (ch, 1), 0).astype(F32)
    if backward:
        dist, valid = jnp.maximum(jj - ii, 0.0), jj > ii
        q_pow, k_pow = ch - pos, pos
    else:
        dist, valid = jnp.maximum(ii - jj, 0.0), ii >= jj
        q_pow, k_pow = pos + 1.0, ch - 1.0 - pos
    for h in range(state_ref.shape[0]):
        lg = jax.nn.log_sigmoid(dec_ref[h])[:, :1]
        intra = jnp.where(valid, jnp.exp(lg * dist), 0.0)
        q = q_ref[:, h * RET_DK:(h + 1) * RET_DK]
        k = k_ref[:, h * RET_DK:(h + 1) * RET_DK]
        v = v_ref[:, h * RET_DV:(h + 1) * RET_DV]
        scores = lax.dot_general(q, k, (((1,), (1,)), ((), ())), preferred_element_type=F32) * intra
        state = state_ref[h]
        o = jnp.dot(scores.astype(BF16), v, preferred_element_type=F32)
        o += jnp.dot((q.astype(F32) * jnp.exp(lg * q_pow)).astype(BF16), state.astype(BF16),
                     preferred_element_type=F32)
        cols = slice(h * RET_DV, (h + 1) * RET_DV)
        if finish:
            y = o + other_ref[:, cols].astype(F32)
            o = gate_ref[:, cols].astype(F32) * (_rms(y, RET_DV) * ng_ref[:, cols])
        o_ref[:, cols] = o.astype(o_ref.dtype)
        kd_t = (k.astype(F32) * jnp.exp(lg * k_pow)).T.astype(BF16)
        state_ref[h] = state * jnp.exp(lg * ch) + jnp.dot(kd_t, v, preferred_element_type=F32)


def _retention(dec, q, k, v, *, backward, seq, n_ctx, batch, finish=None):
    ch = RET_CHUNK
    n_c, n_s = n_ctx // ch, seq // ch
    ctx_blk0 = batch * seq // ch

    def row_block(b, t):
        if backward:
            return jnp.where(t < n_c, ctx_blk0 + b * n_c + (n_c - 1 - t), b * n_s + (n_s - 1 - (t - n_c)))
        return jnp.where(t < n_c, ctx_blk0 + b * n_c + t, b * n_s + (t - n_c))

    hb = RET_HEAD_BLOCK
    spec = lambda width: pl.BlockSpec((ch, hb * width), lambda b, h, t: (row_block(b, t), h))
    extra_specs, extra = [], ()
    if finish is not None:
        extra_specs = [spec(RET_DV), spec(RET_DV), pl.BlockSpec((1, hb * RET_DV), lambda b, h, t: (0, h))]
        extra = tuple(finish)
    return pl.pallas_call(
        functools.partial(_ret_kernel, backward=backward, finish=finish is not None),
        grid=(batch, RET_HEADS // hb, n_c + n_s),
        in_specs=[pl.BlockSpec((hb, 1, LANES), lambda b, h, t: (h, 0, 0)), spec(RET_DK), spec(RET_DK), spec(RET_DV)]
        + extra_specs,
        out_specs=spec(RET_DV),
        out_shape=jax.ShapeDtypeStruct((q.shape[0], RET_HEADS * RET_DV), BF16),
        scratch_shapes=[pltpu.VMEM((hb, RET_DK, RET_DV), F32)],
        compiler_params=_params(("parallel", "parallel", "arbitrary")),
        name="retention_bwd" if backward else "retention_fwd",
    )(dec, q, k, v, *extra)


def _moe_kernel(te_ref, nu_ref, first_ref, nxt_ref, x_ref, wg_hbm, wu_hbm, wd_hbm, o_ref,
                stage_g, stage_u, stage_d, cur_g, cur_u, cur_d, sem, *, layer):
    t = pl.program_id(0)

    def weight_copies(e):
        return (pltpu.make_async_copy(wg_hbm.at[layer, e], stage_g, sem.at[0]),
                pltpu.make_async_copy(wu_hbm.at[layer, e], stage_u, sem.at[1]),
                pltpu.make_async_copy(wd_hbm.at[layer, e], stage_d, sem.at[2]))

    @pl.when(t == 0)
    def _():
        for cp in weight_copies(te_ref[0]):
            cp.start()

    @pl.when((t < nu_ref[0]) & (first_ref[t] == 1))
    def _():
        for cp in weight_copies(te_ref[t]):
            cp.wait()
        for stage, cur in ((stage_g, cur_g), (stage_u, cur_u), (stage_d, cur_d)):
            rows = stage.shape[0] // WEIGHT_CAST_CHUNKS

            def cast(c, carry, stage=stage, cur=cur, rows=rows):
                r0 = pl.multiple_of(c * rows, rows)
                cur[pl.ds(r0, rows), :] = stage[pl.ds(r0, rows), :].astype(BF16)
                return carry

            lax.fori_loop(0, WEIGHT_CAST_CHUNKS, cast, 0)

        @pl.when(nxt_ref[t] >= 0)
        def _():
            for cp in weight_copies(nxt_ref[t]):
                cp.start()

    @pl.when(t < nu_ref[0])
    def _():
        x = x_ref[...].astype(BF16)
        a = jnp.dot(x, cur_g[...], preferred_element_type=F32)
        u = jnp.dot(x, cur_u[...], preferred_element_type=F32)
        act = (a * jax.nn.sigmoid(a) * u).astype(BF16)
        o_ref[...] = jnp.dot(act, cur_d[...], preferred_element_type=F32).astype(o_ref.dtype)

    @pl.when(t >= nu_ref[0])
    def _():
        o_ref[...] = jnp.zeros_like(o_ref)


def _moe_ffn(tile_expert, n_used, first, nxt, x_sorted, wg, wu, wd, layer):
    r, d = x_sorted.shape
    f = wg.shape[3]
    tm = MOE_ROW_TILE
    any_spec = pl.BlockSpec(memory_space=pl.ANY)
    return pl.pallas_call(
        functools.partial(_moe_kernel, layer=layer),
        grid_spec=pltpu.PrefetchScalarGridSpec(
            num_scalar_prefetch=4,
            grid=(r // tm,),
            in_specs=[pl.BlockSpec((tm, d), lambda t, *_: (t, 0)), any_spec, any_spec, any_spec],
            out_specs=pl.BlockSpec((tm, d), lambda t, *_: (t, 0)),
            scratch_shapes=[pltpu.VMEM((d, f), F32), pltpu.VMEM((d, f), F32), pltpu.VMEM((f, d), F32),
                            pltpu.VMEM((d, f), BF16), pltpu.VMEM((d, f), BF16), pltpu.VMEM((f, d), BF16),
                            pltpu.SemaphoreType.DMA((3,))],
        ),
        out_shape=jax.ShapeDtypeStruct((r, d), F32),
        compiler_params=_params(("arbitrary",)),
        name="moe_ffn",
    )(tile_expert, n_used, first, nxt, x_sorted, wg, wu, wd)


def _combine_kernel(x_ref, a_ref, b_ref, w_ref, g_ref, *rest):
    w = w_ref[...]
    xn = x_ref[...] + g_ref[0] * (w[:, 0:1] * a_ref[...] + w[:, 1:2] * b_ref[...])
    rest[-1 if len(rest) == 1 else -2][...] = xn
    if len(rest) > 1:
        gn_ref, sh_ref, sc_ref, _, h_ref = rest
        h_ref[...] = (_rms(xn, xn.shape[-1]) * gn_ref[...] * (1.0 + sc_ref[0]) + sh_ref[0]).astype(h_ref.dtype)


def _combine(x, ab, w_col, gate, rows_per_group, n_groups, next_norm=None):
    n, d = x.shape
    tm = ROW_TILE
    row = pl.BlockSpec((tm, d), lambda i: (i, 0))
    gs = _group_spec(d, tm, rows_per_group, n_groups)
    in_specs = [row, row, pl.BlockSpec((tm, d), lambda i: (i + n // tm, 0)),
                pl.BlockSpec((tm, LANES), lambda i: (i, 0)), gs]
    args = [x, ab, ab, w_col, gate]
    out_specs, out_shape = row, jax.ShapeDtypeStruct((n, d), F32)
    if next_norm is not None:
        gn, sh, sc = next_norm
        in_specs += [pl.BlockSpec((1, d), lambda i: (0, 0)), gs, gs]
        args += [gn.reshape(1, d), sh, sc]
        out_specs, out_shape = [row, row], [out_shape, jax.ShapeDtypeStruct((n, d), BF16)]
    return pl.pallas_call(
        _combine_kernel,
        grid=(n // tm,),
        in_specs=in_specs,
        out_specs=out_specs,
        out_shape=out_shape,
        compiler_params=_params(("parallel",)),
        name="moe_combine",
    )(*args)


def _moe_layer(x, g, sh, sc, gate, rw_pad, rb_col, wg, wu, wd, layer, rows_per_group, n_groups, next_norm):
    n, d = x.shape
    tm = MOE_ROW_TILE
    h, idx8, w_col, cnt = _norm_mod_route(x, g, sh, sc, rw_pad, rb_col, rows_per_group, n_groups)
    experts = jnp.arange(N_EXPERTS, dtype=jnp.int32)
    padded = (cnt[:, 0] + tm - 1) // tm * tm
    ends = jnp.cumsum(padded)
    starts = ends - padded
    e2, rank2 = idx8[0:2], idx8[2:4]
    dest = rank2 + jnp.sum(jnp.where(e2[..., None] == experts, starts, 0), axis=-1)
    r = (2 * n + N_EXPERTS * (tm - 1)) // tm * tm
    tok = jnp.tile(jnp.arange(n, dtype=jnp.int32), 2)
    src = (jnp.arange(r, dtype=jnp.int32) % n).at[dest.reshape(-1)].set(tok, unique_indices=True,
                                                                         mode="promise_in_bounds")
    tile_start = jnp.arange(r // tm, dtype=jnp.int32) * tm
    tile_expert = jnp.minimum(jnp.sum((ends[None, :] <= tile_start[:, None]).astype(jnp.int32), axis=1),
                              N_EXPERTS - 1)
    n_used = ends[-1:] // tm
    first = jnp.concatenate([jnp.ones((1,), jnp.int32), (tile_expert[1:] != tile_expert[:-1]).astype(jnp.int32)])
    later = (padded[None, :] > 0) & (experts[None, :] > experts[:, None])
    next_of = jnp.min(jnp.where(later, experts[None, :], N_EXPERTS), axis=1)
    next_of = jnp.where(next_of < N_EXPERTS, next_of, -1)
    nxt = jnp.sum(jnp.where(tile_expert[:, None] == experts, next_of, 0), axis=1)
    x_sorted = h.at[src].get(mode="promise_in_bounds")
    y_sorted = _moe_ffn(tile_expert, n_used, first, nxt, x_sorted, wg, wu, wd, layer)
    ab = y_sorted.at[dest.reshape(-1)].get(mode="promise_in_bounds")
    return _combine(x, ab, w_col, gate, rows_per_group, n_groups, next_norm)


def _axial_tables(rows, rot_dim):
    n_freq = rot_dim // 4
    inv_freq = jnp.power(ROPE_BASE, -jnp.arange(n_freq, dtype=F32) / n_freq)
    row = jnp.repeat(jnp.arange(rows, dtype=F32), GRID_W)
    col = jnp.tile(jnp.arange(GRID_W, dtype=F32), rows)
    ang = jnp.concatenate([row[:, None] * inv_freq, col[:, None] * inv_freq], axis=-1)
    return jnp.cos(ang), jnp.sin(ang)


def _flat_tables(c_lat, s_lat, batch, n_ctx_rows):
    c = jnp.concatenate([jnp.tile(c_lat, (batch, 1)), jnp.broadcast_to(c_lat[:1], (n_ctx_rows, LANES))])
    s = jnp.concatenate([jnp.tile(s_lat, (batch, 1)), jnp.zeros((n_ctx_rows, LANES), F32)])
    return c, s


def _spread_pairs(a, axis):
    x1, x2 = jnp.split(a, 2, axis=axis)
    z = jnp.zeros_like(x1)
    return jnp.concatenate([x1, z, x2, z], axis=axis)


def kernel(x, c, ctx, c_ctx, ada_w, ada_b, norm_mix_g, norm_ffn_g, mla_w_in, mla_q_a_g, mla_w_q_b, mla_kv_a_g,
           mla_w_kv_b, mla_q_norm_g, mla_k_norm_g, mla_w_o, diff_w_in, diff_q_norm_g, diff_k_norm_g, diff_lambda,
           diff_subln_g, diff_w_o, ret_w_in, ret_decay_fwd, ret_decay_bwd, ret_norm_g, ret_w_o, router_w,
           router_bias, moe_w_gate, moe_w_up, moe_w_down):
    batch, seq, d = x.shape
    n_ctx = ctx.shape[1]
    depth = ada_w.shape[0]
    n_lat, n_cx = batch * seq, batch * n_ctx
    n_groups = batch + 1
    assert n_groups <= SUBLANES and seq % MM_ROW_TILE == 0 and n_cx % MM_ROW_TILE == 0
    assert seq % RET_CHUNK == 0 and n_ctx % RET_CHUNK == 0 and n_lat % n_ctx == 0 and n_ctx == ROW_TILE

    x_lat, x_ctx = x.reshape(n_lat, d), ctx.reshape(n_cx, d)
    xa = None
    n_all = n_lat + n_cx

    cond = jnp.concatenate([c, c_ctx[None, :], jnp.zeros((SUBLANES - n_groups, d), F32)], axis=0)
    mods = _ada_mods(cond.T, ada_w, ada_b, n_groups)

    rows = seq // GRID_W
    cos64, sin64 = _axial_tables(rows, MLA_ROPE)
    zeros32 = jnp.zeros_like(cos64)
    mla_c, mla_s = _flat_tables(jnp.concatenate([cos64, zeros32, cos64, zeros32], axis=1),
                                jnp.concatenate([-sin64, zeros32, sin64, zeros32], axis=1), batch, n_cx)
    cos128, sin128 = _axial_tables(rows, DIFF_HEAD_DIM)
    diff_c, diff_s = _flat_tables(jnp.concatenate([cos128, cos128], axis=1),
                                  jnp.concatenate([-sin128, sin128], axis=1), batch, n_cx)
    cos256, sin256 = _axial_tables(rows, RET_DK)
    ret_c, ret_s = _flat_tables(cos256, sin256, batch, n_cx)

    rw_pad = jnp.concatenate([router_w, router_w, jnp.zeros((d, LANES - 2 * N_EXPERTS), F32)], axis=1)
    rb_col = router_bias.reshape(N_EXPERTS, 1)

    for i in range(depth):
        kind, j, last = i % N_MIXERS, i // N_MIXERS, i == depth - 1
        m = mods[i].reshape(SUBLANES, 6, 1, d)
        sh_m, sc_m, g_m, sh_f, sc_f, g_f = (m[:, t] for t in range(6))
        if i == 0:
            h = _norm_mod(x_lat, x_ctx, norm_mix_g[i], sh_m, sc_m, seq, n_groups)
        q_rows = n_lat if last else n_all

        if kind == 0:
            w_in = jnp.concatenate([mla_w_in[j][:, :MLA_Q_LORA + MLA_KV_LORA],
                                    _spread_pairs(mla_w_in[j][:, MLA_Q_LORA + MLA_KV_LORA:], 1)], axis=1)
            z = _proj(h, w_in, 0, w_in.shape[1], out_dtype=F32, tn=w_in.shape[1])
            wq = mla_w_q_b[j].reshape(MLA_Q_LORA, MLA_HEADS, MLA_NOPE + MLA_ROPE)
            wq = jnp.concatenate([wq[..., :MLA_NOPE], _spread_pairs(wq[..., MLA_NOPE:], 2)], axis=2)
            wq = wq.reshape(MLA_Q_LORA, MLA_HEADS * MLA_HEAD_PAD).astype(BF16)
            scale = float(MLA_NOPE + MLA_ROPE) ** -0.5 * LOG2E
            qg, kg = mla_q_norm_g[j], mla_k_norm_g[j]
            q = _mla_q(z, mla_q_a_g[j], wq, qg[:MLA_NOPE] * scale,
                       *_fold_rope(mla_c, mla_s, _spread_pairs(qg[MLA_NOPE:], 0), scale), q_rows)
            k, v = _mla_kv(z, mla_kv_a_g[j], mla_w_kv_b[j].astype(BF16), kg[:MLA_NOPE],
                           *_fold_rope(mla_c, mla_s, _spread_pairs(kg[MLA_NOPE:], 0), 1.0))
            attn = functools.partial(_attention, q, k, v, heads=MLA_HEADS, q_width=MLA_HEAD_PAD,
                                     v_width=MLA_V + LANES, dv=MLA_V, seq=seq, n_ctx=n_ctx, batch=batch,
                                     q_tile=MLA_Q_TILE)
            w_o = mla_w_o[j]
        elif kind == 1:
            w_in = diff_w_in[j]
            hd = DIFF_HEADS * 2 * DIFF_HEAD_DIM
            scale = float(DIFF_HEAD_DIM) ** -0.5 * LOG2E
            q = _proj(h[:q_rows], w_in, 0, hd, mode="norm_rope",
                      extras=_fold_rope(diff_c[:q_rows], diff_s[:q_rows], diff_q_norm_g[j], scale))
            k = _proj(h, w_in, hd, hd, mode="norm_rope", extras=_fold_rope(diff_c, diff_s, diff_k_norm_g[j], 1.0))
            v = _proj(h, w_in, 2 * hd, hd)
            lam_init = 0.8 - 0.6 * math.exp(-0.3 * i)
            attn = functools.partial(_attention, q, k, v, heads=DIFF_HEADS, q_width=2 * DIFF_HEAD_DIM,
                                     v_width=2 * DIFF_HEAD_DIM, dv=2 * DIFF_HEAD_DIM, seq=seq, n_ctx=n_ctx,
                                     batch=batch, q_tile=ATTN_Q_TILE,
                                     diff=(diff_lambda[j], diff_subln_g[j], lam_init))
            w_o = diff_w_o[j]
        else:
            w_in = ret_w_in[j]
            nq, nv = RET_HEADS * RET_DK, RET_HEADS * RET_DV
            q = _proj(h, w_in, 0, nq, mode="rope256", extras=(ret_c, ret_s))
            k_scale = float(RET_DK) ** -0.5
            k = _proj(h, w_in, nq, nq, mode="rope256", extras=(ret_c * k_scale, ret_s * k_scale))
            v = _proj(h, w_in, 2 * nq, nv)
            gate = _proj(h, w_in, 2 * nq + nv, nv, mode="silu")
            dec = lambda p: jnp.broadcast_to(p.astype(F32)[:, None, None], (RET_HEADS, 1, LANES))
            o_b = _retention(dec(ret_decay_bwd[j]), q, k, v, backward=True, seq=seq, n_ctx=n_ctx, batch=batch)
            y = _retention(dec(ret_decay_fwd[j]), q, k, v, backward=False, seq=seq, n_ctx=n_ctx, batch=batch,
                           finish=(o_b, gate, ret_norm_g[j].reshape(1, -1)))
            attn = None
            w_o = ret_w_o[j]

        y_tail = None
        if attn is not None:
            y = attn(lat_queries=True)
            if not last:
                y_tail = attn(lat_queries=False)
        res_in, res_tail = (x_lat, x_ctx) if xa is None else (xa, None)
        xa = _out_res(y, w_o, res_in, g_m, seq, n_groups, n_lat if last else n_all, y_tail, res_tail)
        next_norm = None
        if not last:
            m_next = mods[i + 1].reshape(SUBLANES, 6, 1, d)
            next_norm = (norm_mix_g[i + 1], m_next[:, 0], m_next[:, 1])
        res = _moe_layer(xa, norm_ffn_g[i], sh_f, sc_f, g_f, rw_pad, rb_col, moe_w_gate, moe_w_up, moe_w_down, i,
                         seq, n_groups, next_norm)
        xa, h = (res, None) if last else res
    return xa[:n_lat].reshape(batch, seq, d)
```

```python
import functools
import math

import jax
import jax.numpy as jnp
from jax import lax
from jax.experimental import pallas as pl
from jax.experimental.pallas import tpu as pltpu

F32 = jnp.float32
BF16 = jnp.bfloat16

GRID_W = 64
ROPE_BASE = 10000.0
NORM_EPS = 1e-6
N_MIXERS = 3

MLA_HEADS = 16
MLA_Q_LORA = 512
MLA_KV_LORA = 512
MLA_NOPE = 128
MLA_ROPE = 64
MLA_V = 128
MLA_HEAD_PAD = 256

DIFF_HEADS = 8
DIFF_HEAD_DIM = 128

RET_HEADS = 8
RET_DK = 256
RET_DV = 512
RET_CHUNK = 256

N_EXPERTS = 16
N_GROUPS = 4
EXPERTS_PER_GROUP = 4
D_EXPERT = 1024

LANES = 128
SUBLANES = 8
VMEM_LIMIT = 56 * 1024 * 1024

ROW_TILE = 256
MM_ROW_TILE = 512
MM_COL_TILE = 1024
MM_COL_TILE_MAX_K = 2048
RET_HEAD_BLOCK = 4
WEIGHT_CAST_CHUNKS = 8
MOE_ROW_TILE = 256
KEY_CHUNK = 512
ATTN_Q_TILE = 512
MLA_Q_TILE = 1024
LOG2E = 1.4426950408889634
ADA_COL_TILE = 1024


def _params(sem):
    return pltpu.CompilerParams(dimension_semantics=sem, vmem_limit_bytes=VMEM_LIMIT)


def _rms(x, width):
    return x * lax.rsqrt(jnp.sum(x * x, axis=-1, keepdims=True) / width + NORM_EPS)


def _ada_kernel(ct_ref, w_ref, b_ref, o_ref, sb_ref, *, n_cond):
    d = ct_ref.shape[0]
    tn = o_ref.shape[-1]

    @pl.when((pl.program_id(0) == 0) & (pl.program_id(1) == 0))
    def _():
        ct = ct_ref[...]
        s = ct * jax.nn.sigmoid(ct)
        for r in range(n_cond):
            sb_ref[r] = jnp.broadcast_to(s[:, r:r + 1], (d, LANES))

    def body(kc, accs):
        k0 = pl.multiple_of(kc * SUBLANES, SUBLANES)
        w8 = w_ref[0, pl.ds(k0, SUBLANES), :]
        out = []
        for r in range(n_cond):
            s8 = sb_ref[r, pl.ds(k0, SUBLANES), :]
            out.append(accs[r] + w8 * jnp.concatenate([s8] * (tn // LANES), axis=1))
        return tuple(out)

    accs = lax.fori_loop(0, d // SUBLANES, body,
                         tuple(jnp.zeros((SUBLANES, tn), F32) for _ in range(n_cond)), unroll=4)
    rows = [jnp.sum(a, axis=0, keepdims=True) + b_ref[0] for a in accs]
    rows.append(jnp.zeros((SUBLANES - n_cond, tn), F32))
    o_ref[0] = jnp.concatenate(rows, axis=0)


def _ada_mods(cond_t, ada_w, ada_b, n_cond):
    depth, d, n6 = ada_w.shape
    tn = ADA_COL_TILE
    return pl.pallas_call(
        functools.partial(_ada_kernel, n_cond=n_cond),
        grid=(depth, n6 // tn),
        in_specs=[pl.BlockSpec((d, SUBLANES), lambda l, j: (0, 0)),
                  pl.BlockSpec((1, d, tn), lambda l, j: (l, 0, j)),
                  pl.BlockSpec((1, 1, tn), lambda l, j: (l, 0, j))],
        out_specs=pl.BlockSpec((1, SUBLANES, tn), lambda l, j: (l, 0, j)),
        out_shape=jax.ShapeDtypeStruct((depth, SUBLANES, n6), F32),
        scratch_shapes=[pltpu.VMEM((n_cond, d, LANES), F32)],
        compiler_params=_params(("arbitrary", "arbitrary")),
        name="ada_mods",
    )(cond_t, ada_w, ada_b.reshape(depth, 1, n6))


def _group_spec(d, tm, rows_per_group, n_groups):
    return pl.BlockSpec((1, 1, d), lambda i, *_: (jnp.minimum(i * tm // rows_per_group, n_groups - 1), 0, 0))


def _split_rows_specs(head_rows, tm, width, row_axis):
    head_tiles = head_rows // tm
    pick = lambda idx: idx[row_axis]
    head = pl.BlockSpec((tm, width), lambda *idx: (jnp.minimum(pick(idx), head_tiles - 1), 0))
    tail = pl.BlockSpec((tm, width), lambda *idx: (jnp.maximum(pick(idx) - head_tiles, 0), 0))
    return head, tail


def _norm_mod_kernel(x_ref, x2_ref, g_ref, sh_ref, sc_ref, o_ref, *, head_tiles):
    x = jnp.where(pl.program_id(0) < head_tiles, x_ref[...], x2_ref[...])
    h = _rms(x, x.shape[-1]) * g_ref[...] * (1.0 + sc_ref[0]) + sh_ref[0]
    o_ref[...] = h.astype(o_ref.dtype)


def _norm_mod(x, x2, g, sh, sc, rows_per_group, n_groups):
    n, d = x.shape[0] + x2.shape[0], x.shape[1]
    tm = ROW_TILE
    gs = _group_spec(d, tm, rows_per_group, n_groups)
    return pl.pallas_call(
        functools.partial(_norm_mod_kernel, head_tiles=x.shape[0] // tm),
        grid=(n // tm,),
        in_specs=[*_split_rows_specs(x.shape[0], tm, d, 0), pl.BlockSpec((1, d), lambda i: (0, 0)), gs, gs],
        out_specs=pl.BlockSpec((tm, d), lambda i: (i, 0)),
        out_shape=jax.ShapeDtypeStruct((n, d), BF16),
        compiler_params=_params(("parallel",)),
        name="norm_mod",
    )(x, x2, g.reshape(1, d), sh, sc)


def _route_kernel(x_ref, g_ref, sh_ref, sc_ref, rw_ref, rb_ref, h_ref, idx_ref, wt_ref, cnt_ref, carry_ref,
                  rwp_ref):
    @pl.when(pl.program_id(0) == 0)
    def _():
        carry_ref[...] = jnp.zeros_like(carry_ref)
        w = rw_ref[...]
        hi = w.astype(BF16)
        lo = (w - hi.astype(F32)).astype(BF16)
        rwp_ref[...] = jnp.where(lax.broadcasted_iota(jnp.int32, w.shape, 1) < N_EXPERTS, hi, lo)

    x = x_ref[...]
    tm = x.shape[0]
    h = _rms(x, x.shape[-1]) * g_ref[...] * (1.0 + sc_ref[0]) + sh_ref[0]
    h_ref[...] = h.astype(h_ref.dtype)
    h_hi = h.astype(BF16)
    h_lo = (h - h_hi.astype(F32)).astype(BF16)
    parts = (jnp.dot(h_hi, rwp_ref[...], preferred_element_type=F32)
             + jnp.dot(h_lo, rwp_ref[...], preferred_element_type=F32)).T
    logits = parts[:N_EXPERTS] + parts[N_EXPERTS:2 * N_EXPERTS]
    scores = jax.nn.sigmoid(logits)
    sel = scores + rb_ref[...]
    sel_r = [sel[e:e + 1, :] for e in range(N_EXPERTS)]
    sc_r = [scores[e:e + 1, :] for e in range(N_EXPERTS)]

    gscore = []
    for g in range(N_GROUPS):
        a, b, c, d = sel_r[4 * g:4 * g + 4]
        hi1, lo1, hi2, lo2 = jnp.maximum(a, b), jnp.minimum(a, b), jnp.maximum(c, d), jnp.minimum(c, d)
        gscore.append(jnp.maximum(hi1, hi2) + jnp.maximum(jnp.minimum(hi1, hi2), jnp.maximum(lo1, lo2)))
    grp = jnp.zeros((1, tm), jnp.int32)
    best = gscore[0]
    for g in range(1, N_GROUPS):
        better = gscore[g] > best
        grp = jnp.where(better, g, grp)
        best = jnp.where(better, gscore[g], best)

    def pick(rows, l):
        out = rows[l]
        for g in range(1, N_GROUPS):
            out = jnp.where(grp == g, rows[4 * g + l], out)
        return out

    v = [pick(sel_r, l) for l in range(EXPERTS_PER_GROUP)]
    s = [pick(sc_r, l) for l in range(EXPERTS_PER_GROUP)]

    def first_max(vals):
        m = jnp.maximum(jnp.maximum(vals[0], vals[1]), jnp.maximum(vals[2], vals[3]))
        l = jnp.where(vals[0] == m, 0, jnp.where(vals[1] == m, 1, jnp.where(vals[2] == m, 2, 3)))
        return l

    def at(vals, l):
        return jnp.where(l == 0, vals[0], jnp.where(l == 1, vals[1], jnp.where(l == 2, vals[2], vals[3])))

    l1 = first_max(v)
    l2 = first_max([jnp.where(l1 == l, -jnp.inf, v[l]) for l in range(EXPERTS_PER_GROUP)])
    s1, s2 = at(s, l1), at(s, l2)
    tot = s1 + s2
    e1, e2 = grp * EXPERTS_PER_GROUP + l1, grp * EXPERTS_PER_GROUP + l2

    eids = lax.broadcasted_iota(jnp.int32, (N_EXPERTS, tm), 0)
    oh1, oh2 = (eids == e1).astype(F32), (eids == e2).astype(F32)
    before = (lax.broadcasted_iota(jnp.int32, (tm, tm), 0) < lax.broadcasted_iota(jnp.int32, (tm, tm), 1)).astype(BF16)
    carry = carry_ref[:, :1]
    tot1 = jnp.sum(oh1, axis=1, keepdims=True)
    pre1 = carry + jnp.dot(oh1.astype(BF16), before, preferred_element_type=F32)
    pre2 = carry + tot1 + jnp.dot(oh2.astype(BF16), before, preferred_element_type=F32)
    r1 = jnp.sum(oh1 * pre1, axis=0, keepdims=True).astype(jnp.int32)
    r2 = jnp.sum(oh2 * pre2, axis=0, keepdims=True).astype(jnp.int32)
    carry_new = carry + tot1 + jnp.sum(oh2, axis=1, keepdims=True)
    carry_ref[...] = jnp.broadcast_to(carry_new, carry_ref.shape)
    cnt_ref[...] = jnp.broadcast_to(carry_new, cnt_ref.shape).astype(jnp.int32)

    idx_ref[...] = jnp.concatenate([e1, e2, r1, r2, jnp.zeros((SUBLANES - 4, tm), jnp.int32)], axis=0)
    wt_ref[...] = jnp.concatenate([s1 / tot, s2 / tot, jnp.zeros((LANES - 2, tm), F32)], axis=0).T


def _norm_mod_route(x, g, sh, sc, rw_pad, rb_col, rows_per_group, n_groups):
    n, d = x.shape
    tm = ROW_TILE
    gs = _group_spec(d, tm, rows_per_group, n_groups)
    return pl.pallas_call(
        _route_kernel,
        grid=(n // tm,),
        in_specs=[pl.BlockSpec((tm, d), lambda i: (i, 0)), pl.BlockSpec((1, d), lambda i: (0, 0)), gs, gs,
                  pl.BlockSpec((d, LANES), lambda i: (0, 0)), pl.BlockSpec((N_EXPERTS, 1), lambda i: (0, 0))],
        out_specs=[pl.BlockSpec((tm, d), lambda i: (i, 0)),
                   pl.BlockSpec((SUBLANES, tm), lambda i: (0, i)),
                   pl.BlockSpec((tm, LANES), lambda i: (i, 0)),
                   pl.BlockSpec((N_EXPERTS, LANES), lambda i: (0, 0))],
        out_shape=[jax.ShapeDtypeStruct((n, d), F32),
                   jax.ShapeDtypeStruct((SUBLANES, n), jnp.int32),
                   jax.ShapeDtypeStruct((n, LANES), F32),
                   jax.ShapeDtypeStruct((N_EXPERTS, LANES), jnp.int32)],
        scratch_shapes=[pltpu.VMEM((N_EXPERTS, LANES), F32), pltpu.VMEM((d, LANES), BF16)],
        compiler_params=_params(("arbitrary",)),
        name="norm_mod_route",
    )(x, g.reshape(1, d), sh, sc, rw_pad, rb_col)


def _half_swap_rope(x, c, s):
    return x * c + pltpu.roll(x, LANES // 2, 1) * s


def _row_sums(sq):
    return jnp.dot(sq.astype(BF16), jnp.ones((sq.shape[1], LANES), BF16), preferred_element_type=F32)


def _fold_rope(c_t, s_t, g, scale):
    return c_t * (g * scale)[None, :], s_t * (jnp.roll(g, LANES // 2) * scale)[None, :]


def _proj_kernel(*refs, mode):
    x_ref, w_ref = refs[0], refs[1]
    o_ref, wb_ref = refs[-2], refs[-1]

    @pl.when(pl.program_id(1) == 0)
    def _():
        wb_ref[...] = w_ref[...].astype(BF16)

    acc = jnp.dot(x_ref[...], wb_ref[...], preferred_element_type=F32)
    tn = acc.shape[1]
    if mode == "plain":
        o_ref[...] = acc.astype(o_ref.dtype)
    elif mode == "silu":
        o_ref[...] = (acc * jax.nn.sigmoid(acc)).astype(o_ref.dtype)
    elif mode == "norm_rope":
        gc, gs = refs[2][...], refs[3][...]
        for j in range(tn // LANES):
            blk = acc[:, j * LANES:(j + 1) * LANES]
            blk = blk * lax.rsqrt(_row_sums(blk * blk) / LANES + NORM_EPS)
            o_ref[:, j * LANES:(j + 1) * LANES] = _half_swap_rope(blk, gc, gs).astype(o_ref.dtype)
    elif mode == "rope256":
        c, s = refs[2][...], refs[3][...]
        for j in range(tn // (2 * LANES)):
            x1 = acc[:, (2 * j) * LANES:(2 * j + 1) * LANES]
            x2 = acc[:, (2 * j + 1) * LANES:(2 * j + 2) * LANES]
            o_ref[:, (2 * j) * LANES:(2 * j + 1) * LANES] = (x1 * c - x2 * s).astype(o_ref.dtype)
            o_ref[:, (2 * j + 1) * LANES:(2 * j + 2) * LANES] = (x2 * c + x1 * s).astype(o_ref.dtype)
    else:
        raise ValueError(mode)


def _col_tile(k):
    return MM_COL_TILE if k <= MM_COL_TILE_MAX_K else MM_COL_TILE // 2


def _proj(x, w, col0, n_cols, *, mode="plain", extras=(), out_dtype=BF16, tn=None):
    n, k = x.shape
    tm = MM_ROW_TILE
    tn = min(_col_tile(k) if tn is None else tn, n_cols)
    j0 = col0 // tn
    extra_specs = []
    for e in extras:
        if e.shape[0] == 1:
            extra_specs.append(pl.BlockSpec(e.shape, lambda j, i: (0, 0)))
        else:
            extra_specs.append(pl.BlockSpec((tm, e.shape[1]), lambda j, i: (i, 0)))
    return pl.pallas_call(
        functools.partial(_proj_kernel, mode=mode),
        grid=(n_cols // tn, n // tm),
        in_specs=[pl.BlockSpec((tm, k), lambda j, i: (i, 0)),
                  pl.BlockSpec((k, tn), lambda j, i: (0, j0 + j))] + extra_specs,
        out_specs=pl.BlockSpec((tm, tn), lambda j, i: (i, j)),
        out_shape=jax.ShapeDtypeStruct((n, n_cols), out_dtype),
        scratch_shapes=[pltpu.VMEM((k, tn), BF16)],
        compiler_params=_params(("arbitrary", "arbitrary")),
        name="proj_" + mode,
    )(x, w, *extras)


def _out_res_kernel(*refs, y_head_tiles, x_head_tiles):
    refs = list(refs)
    i = pl.program_id(1)
    y = refs.pop(0)[...] if y_head_tiles is None else jnp.where(i < y_head_tiles, refs.pop(0)[...], refs.pop(0)[...])
    w_ref = refs.pop(0)
    x = refs.pop(0)[...] if x_head_tiles is None else jnp.where(i < x_head_tiles, refs.pop(0)[...], refs.pop(0)[...])
    g_ref, o_ref, wb_ref = refs

    @pl.when(i == 0)
    def _():
        wb_ref[...] = w_ref[...].astype(BF16)

    o_ref[...] = x + g_ref[0] * jnp.dot(y, wb_ref[...], preferred_element_type=F32)


def _out_res(y, w, x, gate, rows_per_group, n_groups, rows, y_tail=None, x_tail=None):
    n, k = rows, y.shape[1]
    d = w.shape[1]
    tm = MM_ROW_TILE
    tn = _col_tile(k)
    y_specs = [pl.BlockSpec((tm, k), lambda j, i: (i, 0))] if y_tail is None else list(
        _split_rows_specs(y.shape[0], tm, k, 1))
    if x_tail is None:
        x_specs = [pl.BlockSpec((tm, tn), lambda j, i: (i, j))]
    else:
        xt = x.shape[0] // tm
        x_specs = [pl.BlockSpec((tm, tn), lambda j, i: (jnp.minimum(i, xt - 1), j)),
                   pl.BlockSpec((tm, tn), lambda j, i: (jnp.maximum(i - xt, 0), j))]
    args = [y] + ([] if y_tail is None else [y_tail]) + [w, x] + ([] if x_tail is None else [x_tail]) + [gate]
    return pl.pallas_call(
        functools.partial(_out_res_kernel, y_head_tiles=None if y_tail is None else y.shape[0] // tm,
                          x_head_tiles=None if x_tail is None else x.shape[0] // tm),
        grid=(d // tn, n // tm),
        in_specs=y_specs + [pl.BlockSpec((k, tn), lambda j, i: (0, j))] + x_specs
        + [pl.BlockSpec((1, 1, tn), lambda j, i: (jnp.minimum(i * tm // rows_per_group, n_groups - 1), 0, j))],
        out_specs=pl.BlockSpec((tm, tn), lambda j, i: (i, j)),
        out_shape=jax.ShapeDtypeStruct((n, d), F32),
        scratch_shapes=[pltpu.VMEM((k, tn), BF16)],
        compiler_params=_params(("arbitrary", "arbitrary")),
        name="out_res",
    )(*args)


def _mla_q_kernel(z_ref, ga_ref, w_ref, g_ref, gc_ref, gs_ref, o_ref):
    cq = z_ref[...]
    cn = (_rms(cq, cq.shape[-1]) * ga_ref[...]).astype(BF16)
    acc = jnp.dot(cn, w_ref[...], preferred_element_type=F32)
    g, gc, gs = g_ref[...], gc_ref[...], gs_ref[...]
    width = float(MLA_NOPE + MLA_ROPE)
    for j in range(acc.shape[1] // MLA_HEAD_PAD):
        qh = acc[:, j * MLA_HEAD_PAD:(j + 1) * MLA_HEAD_PAD]
        f = lax.rsqrt(_row_sums(qh * qh) / width + NORM_EPS)
        o_ref[:, j * MLA_HEAD_PAD:j * MLA_HEAD_PAD + LANES] = (qh[:, :LANES] * f * g).astype(o_ref.dtype)
        o_ref[:, j * MLA_HEAD_PAD + LANES:(j + 1) * MLA_HEAD_PAD] = _half_swap_rope(
            qh[:, LANES:] * f, gc, gs).astype(o_ref.dtype)


def _mla_q(z, q_a_g, w_q_b_pad, g_nope, gc, gs, rows, tn=1024):
    n = rows
    tm = MM_ROW_TILE
    n_out = w_q_b_pad.shape[1]
    return pl.pallas_call(
        _mla_q_kernel,
        grid=(n // tm, n_out // tn),
        in_specs=[pl.BlockSpec((tm, MLA_Q_LORA), lambda i, j: (i, 0)),
                  pl.BlockSpec((1, MLA_Q_LORA), lambda i, j: (0, 0)),
                  pl.BlockSpec((MLA_Q_LORA, tn), lambda i, j: (0, j)),
                  pl.BlockSpec((1, LANES), lambda i, j: (0, 0)),
                  pl.BlockSpec((tm, LANES), lambda i, j: (i, 0)),
                  pl.BlockSpec((tm, LANES), lambda i, j: (i, 0))],
        out_specs=pl.BlockSpec((tm, tn), lambda i, j: (i, j)),
        out_shape=jax.ShapeDtypeStruct((n, n_out), BF16),
        compiler_params=_params(("parallel", "arbitrary")),
        name="mla_q",
    )(z, q_a_g.reshape(1, -1), w_q_b_pad, g_nope.reshape(1, LANES), gc, gs)


def _mla_kv_kernel(z_ref, kr_ref, ga_ref, w_ref, g_ref, gc_ref, gs_ref, k_ref, v_ref):
    ckv = z_ref[...]
    cn = (_rms(ckv, ckv.shape[-1]) * ga_ref[...]).astype(BF16)
    acc = jnp.dot(cn, w_ref[...], preferred_element_type=F32)
    kr = kr_ref[...]
    kr_ss = _row_sums(kr * kr)
    g, gc, gs = g_ref[...], gc_ref[...], gs_ref[...]
    width = float(MLA_NOPE + MLA_ROPE)
    for j in range(acc.shape[1] // MLA_HEAD_PAD):
        kn = acc[:, j * MLA_HEAD_PAD:j * MLA_HEAD_PAD + MLA_NOPE]
        vv = acc[:, j * MLA_HEAD_PAD + MLA_NOPE:(j + 1) * MLA_HEAD_PAD]
        f = lax.rsqrt((_row_sums(kn * kn) + kr_ss) / width + NORM_EPS)
        k_ref[:, j * MLA_HEAD_PAD:j * MLA_HEAD_PAD + LANES] = (kn * f * g).astype(k_ref.dtype)
        k_ref[:, j * MLA_HEAD_PAD + LANES:(j + 1) * MLA_HEAD_PAD] = _half_swap_rope(
            kr * f, gc, gs).astype(k_ref.dtype)
        v_ref[:, j * MLA_V:(j + 1) * MLA_V] = vv.astype(v_ref.dtype)


def _mla_kv(z, kv_a_g, w_kv_b, g_nope, gc, gs, tn=1024):
    n = z.shape[0]
    tm = MM_ROW_TILE
    n_out = w_kv_b.shape[1]
    kr_block = (MLA_Q_LORA + MLA_KV_LORA) // LANES
    return pl.pallas_call(
        _mla_kv_kernel,
        grid=(n // tm, n_out // tn),
        in_specs=[pl.BlockSpec((tm, MLA_KV_LORA), lambda i, j: (i, 1)),
                  pl.BlockSpec((tm, LANES), lambda i, j: (i, kr_block)),
                  pl.BlockSpec((1, MLA_KV_LORA), lambda i, j: (0, 0)),
                  pl.BlockSpec((MLA_KV_LORA, tn), lambda i, j: (0, j)),
                  pl.BlockSpec((1, LANES), lambda i, j: (0, 0)),
                  pl.BlockSpec((tm, LANES), lambda i, j: (i, 0)),
                  pl.BlockSpec((tm, LANES), lambda i, j: (i, 0))],
        out_specs=[pl.BlockSpec((tm, tn), lambda i, j: (i, j)),
                   pl.BlockSpec((tm, tn // 2), lambda i, j: (i, j))],
        out_shape=[jax.ShapeDtypeStruct((n, n_out), BF16),
                   jax.ShapeDtypeStruct((n, n_out // 2), BF16)],
        compiler_params=_params(("parallel", "arbitrary")),
        name="mla_kv",
    )(z, z, kv_a_g.reshape(1, -1), w_kv_b, g_nope.reshape(1, LANES), gc, gs)


def _softmax_pv(q, k_refs, v_refs, c0, dq, *, den_from_v):
    chunks = []
    for k_ref, v_ref in zip(k_refs, v_refs):
        step = min(k_ref.shape[0], KEY_CHUNK)
        chunks += [(k_ref, v_ref, r0, step) for r0 in range(0, k_ref.shape[0], step)]
    m = acc = den = None
    for k_ref, v_ref, r0, step in chunks:
        s = lax.dot_general(q, k_ref[r0:r0 + step, c0:c0 + dq], (((1,), (1,)), ((), ())),
                            preferred_element_type=F32)
        mc = jnp.max(s, axis=-1, keepdims=True)
        m_new = mc if m is None else jnp.maximum(m, mc)
        p = jnp.exp2((s - m_new).astype(BF16)) if den_from_v else jnp.exp2(s - m_new)
        vblk = v_ref[r0:r0 + step, :]
        if den_from_v:
            vblk = jnp.concatenate([vblk, jnp.ones((step, LANES), BF16)], axis=1)
        pv = jnp.dot(p.astype(BF16), vblk, preferred_element_type=F32)
        if m is None:
            acc = pv
            if not den_from_v:
                den = jnp.sum(p, axis=-1, keepdims=True)
        else:
            alpha = jnp.exp2(m - m_new)
            acc = acc * alpha + pv
            if not den_from_v:
                den = den * alpha + jnp.sum(p, axis=-1, keepdims=True)
        m = m_new
    if den_from_v:
        dv = acc.shape[1] - LANES
        return acc[:, :dv] / acc[:, dv:dv + 1]
    return acc / den


def _attn_kernel(*refs, n_seg, dq):
    q_ref = refs[0]
    k_refs = refs[1:1 + n_seg]
    v_refs = refs[1 + n_seg:1 + 2 * n_seg]
    o_ref = refs[-1]
    o_ref[...] = _softmax_pv(q_ref[...], k_refs, v_refs, 0, dq, den_from_v=True).astype(o_ref.dtype)


def _diff_attn_kernel(*refs, n_seg, dq, lam_init):
    q_ref = refs[0]
    k_refs = refs[1:1 + n_seg]
    v_refs = refs[1 + n_seg:1 + 2 * n_seg]
    lam_ref, g_ref, o_ref = refs[-3], refs[-2], refs[-1]
    lf = lam_ref[...]
    lam = (jnp.exp(jnp.sum(lf[0:1] * lf[1:2], axis=-1, keepdims=True))
           - jnp.exp(jnp.sum(lf[2:3] * lf[3:4], axis=-1, keepdims=True)) + lam_init)
    q = q_ref[...]
    o1 = _softmax_pv(q[:, :dq], k_refs, v_refs, 0, dq, den_from_v=False)
    o2 = _softmax_pv(q[:, dq:], k_refs, v_refs, dq, dq, den_from_v=False)
    o = o1 - lam * o2
    o_ref[...] = (_rms(o, o.shape[-1]) * g_ref[...] * (1.0 - lam_init)).astype(o_ref.dtype)


def _attention(q, k, v, *, heads, q_width, v_width, dv, seq, n_ctx, batch, lat_queries, q_tile, diff=None):
    lat_rows = batch * seq
    ctx_blk0 = lat_rows // n_ctx
    if lat_queries:
        tq = min(q_tile, seq)
        nq = seq // tq
        q_map = lambda b, h, i: (b * nq + i, h)
        segs = [(seq, lambda b, h, i: (b, h)), (n_ctx, lambda b, h, i: (ctx_blk0 + b, h))]
        out_rows = lat_rows
    else:
        tq = n_ctx
        nq = 1
        q_map = lambda b, h, i: (ctx_blk0 + b, h)
        segs = [(n_ctx, lambda b, h, i: (ctx_blk0 + b, h))]
        out_rows = batch * n_ctx
    n_seg = len(segs)
    in_specs = [pl.BlockSpec((tq, q_width), q_map)]
    in_specs += [pl.BlockSpec((rows, q_width), m) for rows, m in segs]
    in_specs += [pl.BlockSpec((rows, v_width), m) for rows, m in segs]
    args = [q] + [k] * n_seg + [v] * n_seg
    if diff is None:
        kern = functools.partial(_attn_kernel, n_seg=n_seg, dq=q_width)
    else:
        lam, subln_g, lam_init = diff
        kern = functools.partial(_diff_attn_kernel, n_seg=n_seg, dq=q_width // 2, lam_init=lam_init)
        in_specs += [pl.BlockSpec(lam.shape, lambda b, h, i: (0, 0)), pl.BlockSpec((1, dv), lambda b, h, i: (0, 0))]
        args += [lam, subln_g.reshape(1, dv)]
    return pl.pallas_call(
        kern,
        grid=(batch, heads, nq),
        in_specs=in_specs,
        out_specs=pl.BlockSpec((tq, dv), lambda b, h, i: (b * nq + i, h)),
        out_shape=jax.ShapeDtypeStruct((out_rows, heads * dv), BF16),
        compiler_params=_params(("parallel", "parallel", "arbitrary")),
        name="attention" if diff is None else "diff_attention",
    )(*args)


def _ret_kernel(dec_ref, q_ref, k_ref, v_ref, *rest, backward, finish):
    if finish:
        other_ref, gate_ref, ng_ref, o_ref, state_ref, intra_ref, qd_ref, kd_ref = rest
    else:
        o_ref, state_ref, intra_ref, qd_ref, kd_ref = rest
    ch = q_ref.shape[0]
    heads = state_ref.shape[0]

    @pl.when(pl.program_id(2) == 0)
    def _():
        state_ref[...] = jnp.zeros_like(state_ref)
        ii = lax.broadcasted_iota(jnp.int32, (ch, ch), 0).astype(F32)
        jj = lax.broadcasted_iota(jnp.int32, (ch, ch), 1).astype(F32)
        pos = lax.broadcasted_iota(jnp.int32, (ch, LANES), 0).astype(F32)
        if backward:
            dist, valid = jnp.maximum(jj - ii, 0.0), jj > ii
            q_pow, k_pow = ch - pos, pos
        else:
            dist, valid = jnp.maximum(ii - jj, 0.0), ii >= jj
            q_pow, k_pow = pos + 1.0, ch - 1.0 - pos
        for h in range(heads):
            lg = jax.nn.log_sigmoid(dec_ref[h])[:, :1]
            intra_ref[h] = jnp.where(valid, jnp.exp(lg * dist), 0.0)
            qd_ref[h] = jnp.exp(lg * q_pow)
            kd_ref[h] = jnp.exp(lg * k_pow)

    wide = lambda t: jnp.concatenate([t] * (RET_DK // LANES), axis=1)
    for h in range(heads):
        lg = jax.nn.log_sigmoid(dec_ref[h])[:, :1]
        intra = intra_ref[h]
        q = q_ref[:, h * RET_DK:(h + 1) * RET_DK]
        k = k_ref[:, h * RET_DK:(h + 1) * RET_DK]
        v = v_ref[:, h * RET_DV:(h + 1) * RET_DV]
        scores = lax.dot_general(q, k, (((1,), (1,)), ((), ())), preferred_element_type=F32) * intra
        state = state_ref[h]
        o = jnp.dot(scores.astype(BF16), v, preferred_element_type=F32)
        o += jnp.dot((q.astype(F32) * wide(qd_ref[h])).astype(BF16), state.astype(BF16),
                     preferred_element_type=F32)
        cols = slice(h * RET_DV, (h + 1) * RET_DV)
        if finish:
            y = o + other_ref[:, cols].astype(F32)
            o = gate_ref[:, cols].astype(F32) * (_rms(y, RET_DV) * ng_ref[:, cols])
        o_ref[:, cols] = o.astype(o_ref.dtype)
        kd_t = (k.astype(F32) * wide(kd_ref[h])).T.astype(BF16)
        state_ref[h] = state * jnp.exp(lg * ch) + jnp.dot(kd_t, v, preferred_element_type=F32)


def _retention(dec, q, k, v, *, backward, seq, n_ctx, batch, finish=None):
    ch = RET_CHUNK
    n_c, n_s = n_ctx // ch, seq // ch
    ctx_blk0 = batch * seq // ch

    def row_block(b, t):
        if backward:
            return jnp.where(t < n_c, ctx_blk0 + b * n_c + (n_c - 1 - t), b * n_s + (n_s - 1 - (t - n_c)))
        return jnp.where(t < n_c, ctx_blk0 + b * n_c + t, b * n_s + (t - n_c))

    hb = RET_HEAD_BLOCK
    spec = lambda width: pl.BlockSpec((ch, hb * width), lambda b, h, t: (row_block(b, t), h))
    extra_specs, extra = [], ()
    if finish is not None:
        extra_specs = [spec(RET_DV), spec(RET_DV), pl.BlockSpec((1, hb * RET_DV), lambda b, h, t: (0, h))]
        extra = tuple(finish)
    return pl.pallas_call(
        functools.partial(_ret_kernel, backward=backward, finish=finish is not None),
        grid=(batch, RET_HEADS // hb, n_c + n_s),
        in_specs=[pl.BlockSpec((hb, 1, LANES), lambda b, h, t: (h, 0, 0)), spec(RET_DK), spec(RET_DK), spec(RET_DV)]
        + extra_specs,
        out_specs=spec(RET_DV),
        out_shape=jax.ShapeDtypeStruct((q.shape[0], RET_HEADS * RET_DV), BF16),
        scratch_shapes=[pltpu.VMEM((hb, RET_DK, RET_DV), F32), pltpu.VMEM((hb, ch, ch), F32),
                        pltpu.VMEM((hb, ch, LANES), F32), pltpu.VMEM((hb, ch, LANES), F32)],
        compiler_params=_params(("parallel", "parallel", "arbitrary")),
        name="retention_bwd" if backward else "retention_fwd",
    )(dec, q, k, v, *extra)


def _moe_kernel(te_ref, nu_ref, first_ref, nxt_ref, x_ref, wg_hbm, wu_hbm, wd_hbm, o_ref,
                stage_g, stage_u, stage_d, cur_g, cur_u, cur_d, sem, *, layer):
    t = pl.program_id(0)

    def weight_copies(e):
        return (pltpu.make_async_copy(wg_hbm.at[layer, e], stage_g, sem.at[0]),
                pltpu.make_async_copy(wu_hbm.at[layer, e], stage_u, sem.at[1]),
                pltpu.make_async_copy(wd_hbm.at[layer, e], stage_d, sem.at[2]))

    @pl.when(t == 0)
    def _():
        for cp in weight_copies(te_ref[0]):
            cp.start()

    @pl.when((t < nu_ref[0]) & (first_ref[t] == 1))
    def _():
        for cp in weight_copies(te_ref[t]):
            cp.wait()
        for stage, cur in ((stage_g, cur_g), (stage_u, cur_u), (stage_d, cur_d)):
            rows = stage.shape[0] // WEIGHT_CAST_CHUNKS

            def cast(c, carry, stage=stage, cur=cur, rows=rows):
                r0 = pl.multiple_of(c * rows, rows)
                cur[pl.ds(r0, rows), :] = stage[pl.ds(r0, rows), :].astype(BF16)
                return carry

            lax.fori_loop(0, WEIGHT_CAST_CHUNKS, cast, 0)

        @pl.when(nxt_ref[t] >= 0)
        def _():
            for cp in weight_copies(nxt_ref[t]):
                cp.start()

    @pl.when(t < nu_ref[0])
    def _():
        x = x_ref[...].astype(BF16)
        a = jnp.dot(x, cur_g[...], preferred_element_type=F32)
        u = jnp.dot(x, cur_u[...], preferred_element_type=F32)
        act = (a * jax.nn.sigmoid(a) * u).astype(BF16)
        o_ref[...] = jnp.dot(act, cur_d[...], preferred_element_type=F32).astype(o_ref.dtype)

    @pl.when(t >= nu_ref[0])
    def _():
        o_ref[...] = jnp.zeros_like(o_ref)


def _moe_ffn(tile_expert, n_used, first, nxt, x_sorted, wg, wu, wd, layer):
    r, d = x_sorted.shape
    f = wg.shape[3]
    tm = MOE_ROW_TILE
    any_spec = pl.BlockSpec(memory_space=pl.ANY)
    return pl.pallas_call(
        functools.partial(_moe_kernel, layer=layer),
        grid_spec=pltpu.PrefetchScalarGridSpec(
            num_scalar_prefetch=4,
            grid=(r // tm,),
            in_specs=[pl.BlockSpec((tm, d), lambda t, *_: (t, 0)), any_spec, any_spec, any_spec],
            out_specs=pl.BlockSpec((tm, d), lambda t, *_: (t, 0)),
            scratch_shapes=[pltpu.VMEM((d, f), F32), pltpu.VMEM((d, f), F32), pltpu.VMEM((f, d), F32),
                            pltpu.VMEM((d, f), BF16), pltpu.VMEM((d, f), BF16), pltpu.VMEM((f, d), BF16),
                            pltpu.SemaphoreType.DMA((3,))],
        ),
        out_shape=jax.ShapeDtypeStruct((r, d), F32),
        compiler_params=_params(("arbitrary",)),
        name="moe_ffn",
    )(tile_expert, n_used, first, nxt, x_sorted, wg, wu, wd)


def _combine_kernel(x_ref, a_ref, b_ref, w_ref, g_ref, *rest):
    w = w_ref[...]
    xn = x_ref[...] + g_ref[0] * (w[:, 0:1] * a_ref[...] + w[:, 1:2] * b_ref[...])
    rest[-1 if len(rest) == 1 else -2][...] = xn
    if len(rest) > 1:
        gn_ref, sh_ref, sc_ref, _, h_ref = rest
        h_ref[...] = (_rms(xn, xn.shape[-1]) * gn_ref[...] * (1.0 + sc_ref[0]) + sh_ref[0]).astype(h_ref.dtype)


def _combine(x, ab, w_col, gate, rows_per_group, n_groups, next_norm=None):
    n, d = x.shape
    tm = ROW_TILE
    row = pl.BlockSpec((tm, d), lambda i: (i, 0))
    gs = _group_spec(d, tm, rows_per_group, n_groups)
    in_specs = [row, row, pl.BlockSpec((tm, d), lambda i: (i + n // tm, 0)),
                pl.BlockSpec((tm, LANES), lambda i: (i, 0)), gs]
    args = [x, ab, ab, w_col, gate]
    out_specs, out_shape = row, jax.ShapeDtypeStruct((n, d), F32)
    if next_norm is not None:
        gn, sh, sc = next_norm
        in_specs += [pl.BlockSpec((1, d), lambda i: (0, 0)), gs, gs]
        args += [gn.reshape(1, d), sh, sc]
        out_specs, out_shape = [row, row], [out_shape, jax.ShapeDtypeStruct((n, d), BF16)]
    return pl.pallas_call(
        _combine_kernel,
        grid=(n // tm,),
        in_specs=in_specs,
        out_specs=out_specs,
        out_shape=out_shape,
        compiler_params=_params(("parallel",)),
        name="moe_combine",
    )(*args)


def _moe_layer(x, g, sh, sc, gate, rw_pad, rb_col, wg, wu, wd, layer, rows_per_group, n_groups, next_norm):
    n, d = x.shape
    tm = MOE_ROW_TILE
    h, idx8, w_col, cnt = _norm_mod_route(x, g, sh, sc, rw_pad, rb_col, rows_per_group, n_groups)
    experts = jnp.arange(N_EXPERTS, dtype=jnp.int32)
    padded = (cnt[:, 0] + tm - 1) // tm * tm
    ends = jnp.cumsum(padded)
    starts = ends - padded
    e2, rank2 = idx8[0:2], idx8[2:4]
    dest = rank2 + jnp.sum(jnp.where(e2[..., None] == experts, starts, 0), axis=-1)
    r = (2 * n + N_EXPERTS * (tm - 1)) // tm * tm
    tok = jnp.tile(jnp.arange(n, dtype=jnp.int32), 2)
    src = (jnp.arange(r, dtype=jnp.int32) % n).at[dest.reshape(-1)].set(tok, unique_indices=True,
                                                                         mode="promise_in_bounds")
    tile_start = jnp.arange(r // tm, dtype=jnp.int32) * tm
    tile_expert = jnp.minimum(jnp.sum((ends[None, :] <= tile_start[:, None]).astype(jnp.int32), axis=1),
                              N_EXPERTS - 1)
    n_used = ends[-1:] // tm
    first = jnp.concatenate([jnp.ones((1,), jnp.int32), (tile_expert[1:] != tile_expert[:-1]).astype(jnp.int32)])
    later = (padded[None, :] > 0) & (experts[None, :] > experts[:, None])
    next_of = jnp.min(jnp.where(later, experts[None, :], N_EXPERTS), axis=1)
    next_of = jnp.where(next_of < N_EXPERTS, next_of, -1)
    nxt = jnp.sum(jnp.where(tile_expert[:, None] == experts, next_of, 0), axis=1)
    x_sorted = h.at[src].get(mode="promise_in_bounds")
    y_sorted = _moe_ffn(tile_expert, n_used, first, nxt, x_sorted, wg, wu, wd, layer)
    ab = y_sorted.at[dest.reshape(-1)].get(mode="promise_in_bounds")
    return _combine(x, ab, w_col, gate, rows_per_group, n_groups, next_norm)


def _axial_tables(rows, rot_dim):
    n_freq = rot_dim // 4
    inv_freq = jnp.power(ROPE_BASE, -jnp.arange(n_freq, dtype=F32) / n_freq)
    row = jnp.repeat(jnp.arange(rows, dtype=F32), GRID_W)
    col = jnp.tile(jnp.arange(GRID_W, dtype=F32), rows)
    ang = jnp.concatenate([row[:, None] * inv_freq, col[:, None] * inv_freq], axis=-1)
    return jnp.cos(ang), jnp.sin(ang)


def _flat_tables(c_lat, s_lat, batch, n_ctx_rows):
    c = jnp.concatenate([jnp.tile(c_lat, (batch, 1)), jnp.broadcast_to(c_lat[:1], (n_ctx_rows, LANES))])
    s = jnp.concatenate([jnp.tile(s_lat, (batch, 1)), jnp.zeros((n_ctx_rows, LANES), F32)])
    return c, s


def _spread_pairs(a, axis):
    x1, x2 = jnp.split(a, 2, axis=axis)
    z = jnp.zeros_like(x1)
    return jnp.concatenate([x1, z, x2, z], axis=axis)


def kernel(x, c, ctx, c_ctx, ada_w, ada_b, norm_mix_g, norm_ffn_g, mla_w_in, mla_q_a_g, mla_w_q_b, mla_kv_a_g,
           mla_w_kv_b, mla_q_norm_g, mla_k_norm_g, mla_w_o, diff_w_in, diff_q_norm_g, diff_k_norm_g, diff_lambda,
           diff_subln_g, diff_w_o, ret_w_in, ret_decay_fwd, ret_decay_bwd, ret_norm_g, ret_w_o, router_w,
           router_bias, moe_w_gate, moe_w_up, moe_w_down):
    batch, seq, d = x.shape
    n_ctx = ctx.shape[1]
    depth = ada_w.shape[0]
    n_lat, n_cx = batch * seq, batch * n_ctx
    n_groups = batch + 1
    assert n_groups <= SUBLANES and seq % MM_ROW_TILE == 0 and n_cx % MM_ROW_TILE == 0
    assert seq % RET_CHUNK == 0 and n_ctx % RET_CHUNK == 0 and n_lat % n_ctx == 0 and n_ctx == ROW_TILE

    x_lat, x_ctx = x.reshape(n_lat, d), ctx.reshape(n_cx, d)
    xa = None
    n_all = n_lat + n_cx

    cond = jnp.concatenate([c, c_ctx[None, :], jnp.zeros((SUBLANES - n_groups, d), F32)], axis=0)
    mods = _ada_mods(cond.T, ada_w, ada_b, n_groups)

    rows = seq // GRID_W
    cos64, sin64 = _axial_tables(rows, MLA_ROPE)
    zeros32 = jnp.zeros_like(cos64)
    mla_c, mla_s = _flat_tables(jnp.concatenate([cos64, zeros32, cos64, zeros32], axis=1),
                                jnp.concatenate([-sin64, zeros32, sin64, zeros32], axis=1), batch, n_cx)
    cos128, sin128 = _axial_tables(rows, DIFF_HEAD_DIM)
    diff_c, diff_s = _flat_tables(jnp.concatenate([cos128, cos128], axis=1),
                                  jnp.concatenate([-sin128, sin128], axis=1), batch, n_cx)
    cos256, sin256 = _axial_tables(rows, RET_DK)
    ret_c, ret_s = _flat_tables(cos256, sin256, batch, n_cx)

    rw_pad = jnp.concatenate([router_w, router_w, jnp.zeros((d, LANES - 2 * N_EXPERTS), F32)], axis=1)
    rb_col = router_bias.reshape(N_EXPERTS, 1)

    for i in range(depth):
        kind, j, last = i % N_MIXERS, i // N_MIXERS, i == depth - 1
        m = mods[i].reshape(SUBLANES, 6, 1, d)
        sh_m, sc_m, g_m, sh_f, sc_f, g_f = (m[:, t] for t in range(6))
        if i == 0:
            h = _norm_mod(x_lat, x_ctx, norm_mix_g[i], sh_m, sc_m, seq, n_groups)
        q_rows = n_lat if last else n_all

        if kind == 0:
            w_in = jnp.concatenate([mla_w_in[j][:, :MLA_Q_LORA + MLA_KV_LORA],
                                    _spread_pairs(mla_w_in[j][:, MLA_Q_LORA + MLA_KV_LORA:], 1)], axis=1)
            z = _proj(h, w_in, 0, w_in.shape[1], out_dtype=F32, tn=w_in.shape[1])
            wq = mla_w_q_b[j].reshape(MLA_Q_LORA, MLA_HEADS, MLA_NOPE + MLA_ROPE)
            wq = jnp.concatenate([wq[..., :MLA_NOPE], _spread_pairs(wq[..., MLA_NOPE:], 2)], axis=2)
            wq = wq.reshape(MLA_Q_LORA, MLA_HEADS * MLA_HEAD_PAD).astype(BF16)
            scale = float(MLA_NOPE + MLA_ROPE) ** -0.5 * LOG2E
            qg, kg = mla_q_norm_g[j], mla_k_norm_g[j]
            q = _mla_q(z, mla_q_a_g[j], wq, qg[:MLA_NOPE] * scale,
                       *_fold_rope(mla_c, mla_s, _spread_pairs(qg[MLA_NOPE:], 0), scale), q_rows)
            k, v = _mla_kv(z, mla_kv_a_g[j], mla_w_kv_b[j].astype(BF16), kg[:MLA_NOPE],
                           *_fold_rope(mla_c, mla_s, _spread_pairs(kg[MLA_NOPE:], 0), 1.0))
            attn = functools.partial(_attention, q, k, v, heads=MLA_HEADS, q_width=MLA_HEAD_PAD,
                                     v_width=MLA_V, dv=MLA_V, seq=seq, n_ctx=n_ctx, batch=batch,
                                     q_tile=MLA_Q_TILE)
            w_o = mla_w_o[j]
        elif kind == 1:
            w_in = diff_w_in[j]
            hd = DIFF_HEADS * 2 * DIFF_HEAD_DIM
            scale = float(DIFF_HEAD_DIM) ** -0.5 * LOG2E
            q = _proj(h[:q_rows], w_in, 0, hd, mode="norm_rope",
                      extras=_fold_rope(diff_c[:q_rows], diff_s[:q_rows], diff_q_norm_g[j], scale))
            k = _proj(h, w_in, hd, hd, mode="norm_rope", extras=_fold_rope(diff_c, diff_s, diff_k_norm_g[j], 1.0))
            v = _proj(h, w_in, 2 * hd, hd)
            lam_init = 0.8 - 0.6 * math.exp(-0.3 * i)
            attn = functools.partial(_attention, q, k, v, heads=DIFF_HEADS, q_width=2 * DIFF_HEAD_DIM,
                                     v_width=2 * DIFF_HEAD_DIM, dv=2 * DIFF_HEAD_DIM, seq=seq, n_ctx=n_ctx,
                                     batch=batch, q_tile=ATTN_Q_TILE,
                                     diff=(diff_lambda[j], diff_subln_g[j], lam_init))
            w_o = diff_w_o[j]
        else:
            w_in = ret_w_in[j]
            nq, nv = RET_HEADS * RET_DK, RET_HEADS * RET_DV
            q = _proj(h, w_in, 0, nq, mode="rope256", extras=(ret_c, ret_s))
            k_scale = float(RET_DK) ** -0.5
            k = _proj(h, w_in, nq, nq, mode="rope256", extras=(ret_c * k_scale, ret_s * k_scale))
            v = _proj(h, w_in, 2 * nq, nv)
            gate = _proj(h, w_in, 2 * nq + nv, nv, mode="silu")
            dec = lambda p: jnp.broadcast_to(p.astype(F32)[:, None, None], (RET_HEADS, 1, LANES))
            o_b = _retention(dec(ret_decay_bwd[j]), q, k, v, backward=True, seq=seq, n_ctx=n_ctx, batch=batch)
            y = _retention(dec(ret_decay_fwd[j]), q, k, v, backward=False, seq=seq, n_ctx=n_ctx, batch=batch,
                           finish=(o_b, gate, ret_norm_g[j].reshape(1, -1)))
            attn = None
            w_o = ret_w_o[j]

        y_tail = None
        if attn is not None:
            y = attn(lat_queries=True)
            if not last:
                y_tail = attn(lat_queries=False)
        res_in, res_tail = (x_lat, x_ctx) if xa is None else (xa, None)
        xa = _out_res(y, w_o, res_in, g_m, seq, n_groups, n_lat if last else n_all, y_tail, res_tail)
        next_norm = None
        if not last:
            m_next = mods[i + 1].reshape(SUBLANES, 6, 1, d)
            next_norm = (norm_mix_g[i + 1], m_next[:, 0], m_next[:, 1])
        res = _moe_layer(xa, norm_ffn_g[i], sh_f, sc_f, g_f, rw_pad, rb_col, moe_w_gate, moe_w_up, moe_w_down, i,
                         seq, n_groups, next_norm)
        xa, h = (res, None) if last else res
    return xa[:n_lat].reshape(batch, seq, d)
```

```python
import functools
import math

import jax
import jax.numpy as jnp
from jax import lax
from jax.experimental import pallas as pl
from jax.experimental.pallas import tpu as pltpu

F32 = jnp.float32
BF16 = jnp.bfloat16

GRID_W = 64
ROPE_BASE = 10000.0
NORM_EPS = 1e-6
N_MIXERS = 3

MLA_HEADS = 16
MLA_Q_LORA = 512
MLA_KV_LORA = 512
MLA_NOPE = 128
MLA_ROPE = 64
MLA_V = 128
MLA_HEAD_PAD = 256

DIFF_HEADS = 8
DIFF_HEAD_DIM = 128

RET_HEADS = 8
RET_DK = 256
RET_DV = 512
RET_CHUNK = 256

N_EXPERTS = 16
N_GROUPS = 4
EXPERTS_PER_GROUP = 4
D_EXPERT = 1024

LANES = 128
SUBLANES = 8
VMEM_LIMIT = 56 * 1024 * 1024

ROW_TILE = 256
MM_ROW_TILE = 512
MM_COL_TILE = 2048
MM_COL_TILE_MAX_K = 2048
RET_HEAD_BLOCK = 4
WEIGHT_CAST_CHUNKS = 8
MOE_ROW_TILE = 256
KEY_CHUNK = 512
ATTN_Q_TILE = 512
MLA_Q_TILE = 1024
LOG2E = 1.4426950408889634
ADA_COL_TILE = 1024


def _params(sem):
    return pltpu.CompilerParams(dimension_semantics=sem, vmem_limit_bytes=VMEM_LIMIT)


def _rms(x, width):
    return x * lax.rsqrt(jnp.sum(x * x, axis=-1, keepdims=True) / width + NORM_EPS)


def _ada_kernel(ct_ref, w_ref, b_ref, o_ref, sb_ref, *, n_cond):
    d = ct_ref.shape[0]
    tn = o_ref.shape[-1]

    @pl.when((pl.program_id(0) == 0) & (pl.program_id(1) == 0))
    def _():
        ct = ct_ref[...]
        s = ct * jax.nn.sigmoid(ct)
        for r in range(n_cond):
            sb_ref[r] = jnp.broadcast_to(s[:, r:r + 1], (d, LANES))

    def body(kc, accs):
        k0 = pl.multiple_of(kc * SUBLANES, SUBLANES)
        w8 = w_ref[0, pl.ds(k0, SUBLANES), :]
        out = []
        for r in range(n_cond):
            s8 = sb_ref[r, pl.ds(k0, SUBLANES), :]
            out.append(accs[r] + w8 * jnp.concatenate([s8] * (tn // LANES), axis=1))
        return tuple(out)

    accs = lax.fori_loop(0, d // SUBLANES, body,
                         tuple(jnp.zeros((SUBLANES, tn), F32) for _ in range(n_cond)), unroll=4)
    rows = [jnp.sum(a, axis=0, keepdims=True) + b_ref[0] for a in accs]
    rows.append(jnp.zeros((SUBLANES - n_cond, tn), F32))
    o_ref[0] = jnp.concatenate(rows, axis=0)


def _ada_mods(cond_t, ada_w, ada_b, n_cond):
    depth, d, n6 = ada_w.shape
    tn = ADA_COL_TILE
    return pl.pallas_call(
        functools.partial(_ada_kernel, n_cond=n_cond),
        grid=(depth, n6 // tn),
        in_specs=[pl.BlockSpec((d, SUBLANES), lambda l, j: (0, 0)),
                  pl.BlockSpec((1, d, tn), lambda l, j: (l, 0, j)),
                  pl.BlockSpec((1, 1, tn), lambda l, j: (l, 0, j))],
        out_specs=pl.BlockSpec((1, SUBLANES, tn), lambda l, j: (l, 0, j)),
        out_shape=jax.ShapeDtypeStruct((depth, SUBLANES, n6), F32),
        scratch_shapes=[pltpu.VMEM((n_cond, d, LANES), F32)],
        compiler_params=_params(("arbitrary", "arbitrary")),
        name="ada_mods",
    )(cond_t, ada_w, ada_b.reshape(depth, 1, n6))


def _group_spec(d, tm, rows_per_group, n_groups):
    return pl.BlockSpec((1, 1, d), lambda i, *_: (jnp.minimum(i * tm // rows_per_group, n_groups - 1), 0, 0))


def _split_rows_specs(head_rows, tm, width, row_axis):
    head_tiles = head_rows // tm
    pick = lambda idx: idx[row_axis]
    head = pl.BlockSpec((tm, width), lambda *idx: (jnp.minimum(pick(idx), head_tiles - 1), 0))
    tail = pl.BlockSpec((tm, width), lambda *idx: (jnp.maximum(pick(idx) - head_tiles, 0), 0))
    return head, tail


def _norm_mod_kernel(x_ref, x2_ref, g_ref, sh_ref, sc_ref, o_ref, *, head_tiles):
    x = jnp.where(pl.program_id(0) < head_tiles, x_ref[...], x2_ref[...])
    h = _rms(x, x.shape[-1]) * g_ref[...] * (1.0 + sc_ref[0]) + sh_ref[0]
    o_ref[...] = h.astype(o_ref.dtype)


def _norm_mod(x, x2, g, sh, sc, rows_per_group, n_groups):
    n, d = x.shape[0] + x2.shape[0], x.shape[1]
    tm = ROW_TILE
    gs = _group_spec(d, tm, rows_per_group, n_groups)
    return pl.pallas_call(
        functools.partial(_norm_mod_kernel, head_tiles=x.shape[0] // tm),
        grid=(n // tm,),
        in_specs=[*_split_rows_specs(x.shape[0], tm, d, 0), pl.BlockSpec((1, d), lambda i: (0, 0)), gs, gs],
        out_specs=pl.BlockSpec((tm, d), lambda i: (i, 0)),
        out_shape=jax.ShapeDtypeStruct((n, d), BF16),
        compiler_params=_params(("parallel",)),
        name="norm_mod",
    )(x, x2, g.reshape(1, d), sh, sc)


def _route_kernel(x_ref, g_ref, sh_ref, sc_ref, rw_ref, rb_ref, h_ref, idx_ref, wt_ref, cnt_ref, carry_ref,
                  rwp_ref):
    @pl.when(pl.program_id(0) == 0)
    def _():
        carry_ref[...] = jnp.zeros_like(carry_ref)
        w = rw_ref[...]
        hi = w.astype(BF16)
        lo = (w - hi.astype(F32)).astype(BF16)
        rwp_ref[...] = jnp.where(lax.broadcasted_iota(jnp.int32, w.shape, 1) < N_EXPERTS, hi, lo)

    x = x_ref[...]
    tm = x.shape[0]
    h = _rms(x, x.shape[-1]) * g_ref[...] * (1.0 + sc_ref[0]) + sh_ref[0]
    h_ref[...] = h.astype(h_ref.dtype)
    h_hi = h.astype(BF16)
    h_lo = (h - h_hi.astype(F32)).astype(BF16)
    parts = (jnp.dot(h_hi, rwp_ref[...], preferred_element_type=F32)
             + jnp.dot(h_lo, rwp_ref[...], preferred_element_type=F32)).T
    logits = parts[:N_EXPERTS] + parts[N_EXPERTS:2 * N_EXPERTS]
    scores = jax.nn.sigmoid(logits)
    sel = scores + rb_ref[...]
    sel_r = [sel[e:e + 1, :] for e in range(N_EXPERTS)]
    sc_r = [scores[e:e + 1, :] for e in range(N_EXPERTS)]

    gscore = []
    for g in range(N_GROUPS):
        a, b, c, d = sel_r[4 * g:4 * g + 4]
        hi1, lo1, hi2, lo2 = jnp.maximum(a, b), jnp.minimum(a, b), jnp.maximum(c, d), jnp.minimum(c, d)
        gscore.append(jnp.maximum(hi1, hi2) + jnp.maximum(jnp.minimum(hi1, hi2), jnp.maximum(lo1, lo2)))
    grp = jnp.zeros((1, tm), jnp.int32)
    best = gscore[0]
    for g in range(1, N_GROUPS):
        better = gscore[g] > best
        grp = jnp.where(better, g, grp)
        best = jnp.where(better, gscore[g], best)

    def pick(rows, l):
        out = rows[l]
        for g in range(1, N_GROUPS):
            out = jnp.where(grp == g, rows[4 * g + l], out)
        return out

    v = [pick(sel_r, l) for l in range(EXPERTS_PER_GROUP)]
    s = [pick(sc_r, l) for l in range(EXPERTS_PER_GROUP)]

    def first_max(vals):
        m = jnp.maximum(jnp.maximum(vals[0], vals[1]), jnp.maximum(vals[2], vals[3]))
        l = jnp.where(vals[0] == m, 0, jnp.where(vals[1] == m, 1, jnp.where(vals[2] == m, 2, 3)))
        return l

    def at(vals, l):
        return jnp.where(l == 0, vals[0], jnp.where(l == 1, vals[1], jnp.where(l == 2, vals[2], vals[3])))

    l1 = first_max(v)
    l2 = first_max([jnp.where(l1 == l, -jnp.inf, v[l]) for l in range(EXPERTS_PER_GROUP)])
    s1, s2 = at(s, l1), at(s, l2)
    tot = s1 + s2
    e1, e2 = grp * EXPERTS_PER_GROUP + l1, grp * EXPERTS_PER_GROUP + l2

    eids = lax.broadcasted_iota(jnp.int32, (N_EXPERTS, tm), 0)
    oh1, oh2 = (eids == e1).astype(F32), (eids == e2).astype(F32)
    before = (lax.broadcasted_iota(jnp.int32, (tm, tm), 0) < lax.broadcasted_iota(jnp.int32, (tm, tm), 1)).astype(BF16)
    carry = carry_ref[:, :1]
    tot1 = jnp.sum(oh1, axis=1, keepdims=True)
    pre1 = carry + jnp.dot(oh1.astype(BF16), before, preferred_element_type=F32)
    pre2 = carry + tot1 + jnp.dot(oh2.astype(BF16), before, preferred_element_type=F32)
    r1 = jnp.sum(oh1 * pre1, axis=0, keepdims=True).astype(jnp.int32)
    r2 = jnp.sum(oh2 * pre2, axis=0, keepdims=True).astype(jnp.int32)
    carry_new = carry + tot1 + jnp.sum(oh2, axis=1, keepdims=True)
    carry_ref[...] = jnp.broadcast_to(carry_new, carry_ref.shape)
    cnt_ref[...] = jnp.broadcast_to(carry_new, cnt_ref.shape).astype(jnp.int32)

    idx_ref[...] = jnp.concatenate([e1, e2, r1, r2, jnp.zeros((SUBLANES - 4, tm), jnp.int32)], axis=0)
    wt_ref[...] = jnp.concatenate([s1 / tot, s2 / tot, jnp.zeros((LANES - 2, tm), F32)], axis=0).T


def _norm_mod_route(x, g, sh, sc, rw_pad, rb_col, rows_per_group, n_groups):
    n, d = x.shape
    tm = ROW_TILE
    gs = _group_spec(d, tm, rows_per_group, n_groups)
    return pl.pallas_call(
        _route_kernel,
        grid=(n // tm,),
        in_specs=[pl.BlockSpec((tm, d), lambda i: (i, 0)), pl.BlockSpec((1, d), lambda i: (0, 0)), gs, gs,
                  pl.BlockSpec((d, LANES), lambda i: (0, 0)), pl.BlockSpec((N_EXPERTS, 1), lambda i: (0, 0))],
        out_specs=[pl.BlockSpec((tm, d), lambda i: (i, 0)),
                   pl.BlockSpec((SUBLANES, tm), lambda i: (0, i)),
                   pl.BlockSpec((tm, LANES), lambda i: (i, 0)),
                   pl.BlockSpec((N_EXPERTS, LANES), lambda i: (0, 0))],
        out_shape=[jax.ShapeDtypeStruct((n, d), F32),
                   jax.ShapeDtypeStruct((SUBLANES, n), jnp.int32),
                   jax.ShapeDtypeStruct((n, LANES), F32),
                   jax.ShapeDtypeStruct((N_EXPERTS, LANES), jnp.int32)],
        scratch_shapes=[pltpu.VMEM((N_EXPERTS, LANES), F32), pltpu.VMEM((d, LANES), BF16)],
        compiler_params=_params(("arbitrary",)),
        name="norm_mod_route",
    )(x, g.reshape(1, d), sh, sc, rw_pad, rb_col)


def _half_swap_rope(x, c, s):
    return x * c + pltpu.roll(x, LANES // 2, 1) * s


def _row_sums(sq):
    return jnp.dot(sq.astype(BF16), jnp.ones((sq.shape[1], LANES), BF16), preferred_element_type=F32)


def _fold_rope(c_t, s_t, g, scale):
    return c_t * (g * scale)[None, :], s_t * (jnp.roll(g, LANES // 2) * scale)[None, :]


def _proj_kernel(*refs, mode):
    x_ref, w_ref = refs[0], refs[1]
    o_ref, wb_ref = refs[-2], refs[-1]

    @pl.when(pl.program_id(1) == 0)
    def _():
        wb_ref[...] = w_ref[...].astype(BF16)

    acc = jnp.dot(x_ref[...], wb_ref[...], preferred_element_type=F32)
    tn = acc.shape[1]
    if mode == "plain":
        o_ref[...] = acc.astype(o_ref.dtype)
    elif mode == "silu":
        o_ref[...] = (acc * jax.nn.sigmoid(acc)).astype(o_ref.dtype)
    elif mode == "norm_rope":
        gc, gs = refs[2][...], refs[3][...]
        for j in range(tn // LANES):
            blk = acc[:, j * LANES:(j + 1) * LANES]
            blk = blk * lax.rsqrt(_row_sums(blk * blk) / LANES + NORM_EPS)
            o_ref[:, j * LANES:(j + 1) * LANES] = _half_swap_rope(blk, gc, gs).astype(o_ref.dtype)
    elif mode == "rope256":
        c, s = refs[2][...], refs[3][...]
        for j in range(tn // (2 * LANES)):
            x1 = acc[:, (2 * j) * LANES:(2 * j + 1) * LANES]
            x2 = acc[:, (2 * j + 1) * LANES:(2 * j + 2) * LANES]
            o_ref[:, (2 * j) * LANES:(2 * j + 1) * LANES] = (x1 * c - x2 * s).astype(o_ref.dtype)
            o_ref[:, (2 * j + 1) * LANES:(2 * j + 2) * LANES] = (x2 * c + x1 * s).astype(o_ref.dtype)
    else:
        raise ValueError(mode)


def _col_tile(k, f32_tiles):
    tn = MM_COL_TILE if k <= MM_COL_TILE_MAX_K else MM_COL_TILE // 2
    return tn // 2 if f32_tiles else tn


def _proj(x, w, col0, n_cols, *, mode="plain", extras=(), out_dtype=BF16, tn=None):
    n, k = x.shape
    tm = MM_ROW_TILE
    tn = min(_col_tile(k, False) if tn is None else tn, n_cols)
    j0 = col0 // tn
    extra_specs = []
    for e in extras:
        if e.shape[0] == 1:
            extra_specs.append(pl.BlockSpec(e.shape, lambda j, i: (0, 0)))
        else:
            extra_specs.append(pl.BlockSpec((tm, e.shape[1]), lambda j, i: (i, 0)))
    return pl.pallas_call(
        functools.partial(_proj_kernel, mode=mode),
        grid=(n_cols // tn, n // tm),
        in_specs=[pl.BlockSpec((tm, k), lambda j, i: (i, 0)),
                  pl.BlockSpec((k, tn), lambda j, i: (0, j0 + j))] + extra_specs,
        out_specs=pl.BlockSpec((tm, tn), lambda j, i: (i, j)),
        out_shape=jax.ShapeDtypeStruct((n, n_cols), out_dtype),
        scratch_shapes=[pltpu.VMEM((k, tn), BF16)],
        compiler_params=_params(("arbitrary", "arbitrary")),
        name="proj_" + mode,
    )(x, w, *extras)


def _out_res_kernel(*refs, y_head_tiles, x_head_tiles):
    refs = list(refs)
    i = pl.program_id(1)
    y = refs.pop(0)[...] if y_head_tiles is None else jnp.where(i < y_head_tiles, refs.pop(0)[...], refs.pop(0)[...])
    w_ref = refs.pop(0)
    x = refs.pop(0)[...] if x_head_tiles is None else jnp.where(i < x_head_tiles, refs.pop(0)[...], refs.pop(0)[...])
    g_ref, o_ref, wb_ref = refs

    @pl.when(i == 0)
    def _():
        wb_ref[...] = w_ref[...].astype(BF16)

    o_ref[...] = x + g_ref[0] * jnp.dot(y, wb_ref[...], preferred_element_type=F32)


def _out_res(y, w, x, gate, rows_per_group, n_groups, rows, y_tail=None, x_tail=None):
    n, k = rows, y.shape[1]
    d = w.shape[1]
    tm = MM_ROW_TILE
    tn = _col_tile(k, True)
    y_specs = [pl.BlockSpec((tm, k), lambda j, i: (i, 0))] if y_tail is None else list(
        _split_rows_specs(y.shape[0], tm, k, 1))
    if x_tail is None:
        x_specs = [pl.BlockSpec((tm, tn), lambda j, i: (i, j))]
    else:
        xt = x.shape[0] // tm
        x_specs = [pl.BlockSpec((tm, tn), lambda j, i: (jnp.minimum(i, xt - 1), j)),
                   pl.BlockSpec((tm, tn), lambda j, i: (jnp.maximum(i - xt, 0), j))]
    args = [y] + ([] if y_tail is None else [y_tail]) + [w, x] + ([] if x_tail is None else [x_tail]) + [gate]
    return pl.pallas_call(
        functools.partial(_out_res_kernel, y_head_tiles=None if y_tail is None else y.shape[0] // tm,
                          x_head_tiles=None if x_tail is None else x.shape[0] // tm),
        grid=(d // tn, n // tm),
        in_specs=y_specs + [pl.BlockSpec((k, tn), lambda j, i: (0, j))] + x_specs
        + [pl.BlockSpec((1, 1, tn), lambda j, i: (jnp.minimum(i * tm // rows_per_group, n_groups - 1), 0, j))],
        out_specs=pl.BlockSpec((tm, tn), lambda j, i: (i, j)),
        out_shape=jax.ShapeDtypeStruct((n, d), F32),
        scratch_shapes=[pltpu.VMEM((k, tn), BF16)],
        compiler_params=_params(("arbitrary", "arbitrary")),
        name="out_res",
    )(*args)


def _mla_q_kernel(z_ref, ga_ref, w_ref, g_ref, gc_ref, gs_ref, o_ref):
    cq = z_ref[...]
    cn = (_rms(cq, cq.shape[-1]) * ga_ref[...]).astype(BF16)
    acc = jnp.dot(cn, w_ref[...], preferred_element_type=F32)
    g, gc, gs = g_ref[...], gc_ref[...], gs_ref[...]
    width = float(MLA_NOPE + MLA_ROPE)
    for j in range(acc.shape[1] // MLA_HEAD_PAD):
        qh = acc[:, j * MLA_HEAD_PAD:(j + 1) * MLA_HEAD_PAD]
        f = lax.rsqrt(_row_sums(qh * qh) / width + NORM_EPS)
        o_ref[:, j * MLA_HEAD_PAD:j * MLA_HEAD_PAD + LANES] = (qh[:, :LANES] * f * g).astype(o_ref.dtype)
        o_ref[:, j * MLA_HEAD_PAD + LANES:(j + 1) * MLA_HEAD_PAD] = _half_swap_rope(
            qh[:, LANES:] * f, gc, gs).astype(o_ref.dtype)


def _mla_q(z, q_a_g, w_q_b_pad, g_nope, gc, gs, rows, tn=MM_COL_TILE):
    n = rows
    tm = MM_ROW_TILE
    n_out = w_q_b_pad.shape[1]
    return pl.pallas_call(
        _mla_q_kernel,
        grid=(n // tm, n_out // tn),
        in_specs=[pl.BlockSpec((tm, MLA_Q_LORA), lambda i, j: (i, 0)),
                  pl.BlockSpec((1, MLA_Q_LORA), lambda i, j: (0, 0)),
                  pl.BlockSpec((MLA_Q_LORA, tn), lambda i, j: (0, j)),
                  pl.BlockSpec((1, LANES), lambda i, j: (0, 0)),
                  pl.BlockSpec((tm, LANES), lambda i, j: (i, 0)),
                  pl.BlockSpec((tm, LANES), lambda i, j: (i, 0))],
        out_specs=pl.BlockSpec((tm, tn), lambda i, j: (i, j)),
        out_shape=jax.ShapeDtypeStruct((n, n_out), BF16),
        compiler_params=_params(("parallel", "arbitrary")),
        name="mla_q",
    )(z, q_a_g.reshape(1, -1), w_q_b_pad, g_nope.reshape(1, LANES), gc, gs)


def _mla_kv_kernel(z_ref, kr_ref, ga_ref, w_ref, g_ref, gc_ref, gs_ref, k_ref, v_ref):
    ckv = z_ref[...]
    cn = (_rms(ckv, ckv.shape[-1]) * ga_ref[...]).astype(BF16)
    acc = jnp.dot(cn, w_ref[...], preferred_element_type=F32)
    kr = kr_ref[...]
    kr_ss = _row_sums(kr * kr)
    g, gc, gs = g_ref[...], gc_ref[...], gs_ref[...]
    width = float(MLA_NOPE + MLA_ROPE)
    for j in range(acc.shape[1] // MLA_HEAD_PAD):
        kn = acc[:, j * MLA_HEAD_PAD:j * MLA_HEAD_PAD + MLA_NOPE]
        vv = acc[:, j * MLA_HEAD_PAD + MLA_NOPE:(j + 1) * MLA_HEAD_PAD]
        f = lax.rsqrt((_row_sums(kn * kn) + kr_ss) / width + NORM_EPS)
        k_ref[:, j * MLA_HEAD_PAD:j * MLA_HEAD_PAD + LANES] = (kn * f * g).astype(k_ref.dtype)
        k_ref[:, j * MLA_HEAD_PAD + LANES:(j + 1) * MLA_HEAD_PAD] = _half_swap_rope(
            kr * f, gc, gs).astype(k_ref.dtype)
        v_ref[:, j * MLA_V:(j + 1) * MLA_V] = vv.astype(v_ref.dtype)


def _mla_kv(z, kv_a_g, w_kv_b, g_nope, gc, gs, tn=MM_COL_TILE):
    n = z.shape[0]
    tm = MM_ROW_TILE
    n_out = w_kv_b.shape[1]
    kr_block = (MLA_Q_LORA + MLA_KV_LORA) // LANES
    return pl.pallas_call(
        _mla_kv_kernel,
        grid=(n // tm, n_out // tn),
        in_specs=[pl.BlockSpec((tm, MLA_KV_LORA), lambda i, j: (i, 1)),
                  pl.BlockSpec((tm, LANES), lambda i, j: (i, kr_block)),
                  pl.BlockSpec((1, MLA_KV_LORA), lambda i, j: (0, 0)),
                  pl.BlockSpec((MLA_KV_LORA, tn), lambda i, j: (0, j)),
                  pl.BlockSpec((1, LANES), lambda i, j: (0, 0)),
                  pl.BlockSpec((tm, LANES), lambda i, j: (i, 0)),
                  pl.BlockSpec((tm, LANES), lambda i, j: (i, 0))],
        out_specs=[pl.BlockSpec((tm, tn), lambda i, j: (i, j)),
                   pl.BlockSpec((tm, tn // 2), lambda i, j: (i, j))],
        out_shape=[jax.ShapeDtypeStruct((n, n_out), BF16),
                   jax.ShapeDtypeStruct((n, n_out // 2), BF16)],
        compiler_params=_params(("parallel", "arbitrary")),
        name="mla_kv",
    )(z, z, kv_a_g.reshape(1, -1), w_kv_b, g_nope.reshape(1, LANES), gc, gs)


def _softmax_pv(q, k_refs, v_refs, c0, dq, *, den_from_v):
    chunks = []
    for k_ref, v_ref in zip(k_refs, v_refs):
        step = min(k_ref.shape[0], KEY_CHUNK)
        chunks += [(k_ref, v_ref, r0, step) for r0 in range(0, k_ref.shape[0], step)]
    m = acc = den = None
    for k_ref, v_ref, r0, step in chunks:
        s = lax.dot_general(q, k_ref[r0:r0 + step, c0:c0 + dq], (((1,), (1,)), ((), ())),
                            preferred_element_type=F32)
        mc = jnp.max(s, axis=-1, keepdims=True)
        m_new = mc if m is None else jnp.maximum(m, mc)
        p = jnp.exp2((s - m_new).astype(BF16)) if den_from_v else jnp.exp2(s - m_new)
        vblk = v_ref[r0:r0 + step, :]
        if den_from_v:
            vblk = jnp.concatenate([vblk, jnp.ones((step, LANES), BF16)], axis=1)
        pv = jnp.dot(p.astype(BF16), vblk, preferred_element_type=F32)
        if m is None:
            acc = pv
            if not den_from_v:
                den = jnp.sum(p, axis=-1, keepdims=True)
        else:
            alpha = jnp.exp2(m - m_new)
            acc = acc * alpha + pv
            if not den_from_v:
                den = den * alpha + jnp.sum(p, axis=-1, keepdims=True)
        m = m_new
    if den_from_v:
        dv = acc.shape[1] - LANES
        return acc[:, :dv] / acc[:, dv:dv + 1]
    return acc / den


def _attn_kernel(*refs, n_seg, dq):
    q_ref = refs[0]
    k_refs = refs[1:1 + n_seg]
    v_refs = refs[1 + n_seg:1 + 2 * n_seg]
    o_ref = refs[-1]
    o_ref[...] = _softmax_pv(q_ref[...], k_refs, v_refs, 0, dq, den_from_v=True).astype(o_ref.dtype)


def _diff_attn_kernel(*refs, n_seg, dq, lam_init):
    q_ref = refs[0]
    k_refs = refs[1:1 + n_seg]
    v_refs = refs[1 + n_seg:1 + 2 * n_seg]
    lam_ref, g_ref, o_ref = refs[-3], refs[-2], refs[-1]
    lf = lam_ref[...]
    lam = (jnp.exp(jnp.sum(lf[0:1] * lf[1:2], axis=-1, keepdims=True))
           - jnp.exp(jnp.sum(lf[2:3] * lf[3:4], axis=-1, keepdims=True)) + lam_init)
    q = q_ref[...]
    o1 = _softmax_pv(q[:, :dq], k_refs, v_refs, 0, dq, den_from_v=False)
    o2 = _softmax_pv(q[:, dq:], k_refs, v_refs, dq, dq, den_from_v=False)
    o = o1 - lam * o2
    o_ref[...] = (_rms(o, o.shape[-1]) * g_ref[...] * (1.0 - lam_init)).astype(o_ref.dtype)


def _attention(q, k, v, *, heads, q_width, v_width, dv, seq, n_ctx, batch, lat_queries, q_tile, diff=None):
    lat_rows = batch * seq
    ctx_blk0 = lat_rows // n_ctx
    if lat_queries:
        tq = min(q_tile, seq)
        nq = seq // tq
        q_map = lambda b, h, i: (b * nq + i, h)
        segs = [(seq, lambda b, h, i: (b, h)), (n_ctx, lambda b, h, i: (ctx_blk0 + b, h))]
        out_rows = lat_rows
    else:
        tq = n_ctx
        nq = 1
        q_map = lambda b, h, i: (ctx_blk0 + b, h)
        segs = [(n_ctx, lambda b, h, i: (ctx_blk0 + b, h))]
        out_rows = batch * n_ctx
    n_seg = len(segs)
    in_specs = [pl.BlockSpec((tq, q_width), q_map)]
    in_specs += [pl.BlockSpec((rows, q_width), m) for rows, m in segs]
    in_specs += [pl.BlockSpec((rows, v_width), m) for rows, m in segs]
    args = [q] + [k] * n_seg + [v] * n_seg
    if diff is None:
        kern = functools.partial(_attn_kernel, n_seg=n_seg, dq=q_width)
    else:
        lam, subln_g, lam_init = diff
        kern = functools.partial(_diff_attn_kernel, n_seg=n_seg, dq=q_width // 2, lam_init=lam_init)
        in_specs += [pl.BlockSpec(lam.shape, lambda b, h, i: (0, 0)), pl.BlockSpec((1, dv), lambda b, h, i: (0, 0))]
        args += [lam, subln_g.reshape(1, dv)]
    return pl.pallas_call(
        kern,
        grid=(batch, heads, nq),
        in_specs=in_specs,
        out_specs=pl.BlockSpec((tq, dv), lambda b, h, i: (b * nq + i, h)),
        out_shape=jax.ShapeDtypeStruct((out_rows, heads * dv), BF16),
        compiler_params=_params(("parallel", "parallel", "arbitrary")),
        name="attention" if diff is None else "diff_attention",
    )(*args)


def _ret_kernel(dec_ref, q_ref, k_ref, v_ref, *rest, backward, finish):
    if finish:
        other_ref, gate_ref, ng_ref, o_ref, state_ref, intra_ref, qd_ref, kd_ref = rest
    else:
        o_ref, state_ref, intra_ref, qd_ref, kd_ref = rest
    ch = q_ref.shape[0]
    heads = state_ref.shape[0]

    @pl.when(pl.program_id(2) == 0)
    def _():
        state_ref[...] = jnp.zeros_like(state_ref)
        ii = lax.broadcasted_iota(jnp.int32, (ch, ch), 0).astype(F32)
        jj = lax.broadcasted_iota(jnp.int32, (ch, ch), 1).astype(F32)
        pos = lax.broadcasted_iota(jnp.int32, (ch, LANES), 0).astype(F32)
        if backward:
            dist, valid = jnp.maximum(jj - ii, 0.0), jj > ii
            q_pow, k_pow = ch - pos, pos
        else:
            dist, valid = jnp.maximum(ii - jj, 0.0), ii >= jj
            q_pow, k_pow = pos + 1.0, ch - 1.0 - pos
        for h in range(heads):
            lg = jax.nn.log_sigmoid(dec_ref[h])[:, :1]
            intra_ref[h] = jnp.where(valid, jnp.exp(lg * dist), 0.0)
            qd_ref[h] = jnp.exp(lg * q_pow)
            kd_ref[h] = jnp.exp(lg * k_pow)

    wide = lambda t: jnp.concatenate([t] * (RET_DK // LANES), axis=1)
    for h in range(heads):
        lg = jax.nn.log_sigmoid(dec_ref[h])[:, :1]
        intra = intra_ref[h]
        q = q_ref[:, h * RET_DK:(h + 1) * RET_DK]
        k = k_ref[:, h * RET_DK:(h + 1) * RET_DK]
        v = v_ref[:, h * RET_DV:(h + 1) * RET_DV]
        scores = lax.dot_general(q, k, (((1,), (1,)), ((), ())), preferred_element_type=F32) * intra
        state = state_ref[h]
        o = jnp.dot(scores.astype(BF16), v, preferred_element_type=F32)
        o += jnp.dot((q.astype(F32) * wide(qd_ref[h])).astype(BF16), state.astype(BF16),
                     preferred_element_type=F32)
        cols = slice(h * RET_DV, (h + 1) * RET_DV)
        if finish:
            y = o + other_ref[:, cols].astype(F32)
            o = gate_ref[:, cols].astype(F32) * (_rms(y, RET_DV) * ng_ref[:, cols])
        o_ref[:, cols] = o.astype(o_ref.dtype)
        kd_t = (k.astype(F32) * wide(kd_ref[h])).T.astype(BF16)
        state_ref[h] = state * jnp.exp(lg * ch) + jnp.dot(kd_t, v, preferred_element_type=F32)


def _retention(dec, q, k, v, *, backward, seq, n_ctx, batch, finish=None):
    ch = RET_CHUNK
    n_c, n_s = n_ctx // ch, seq // ch
    ctx_blk0 = batch * seq // ch

    def row_block(b, t):
        if backward:
            return jnp.where(t < n_c, ctx_blk0 + b * n_c + (n_c - 1 - t), b * n_s + (n_s - 1 - (t - n_c)))
        return jnp.where(t < n_c, ctx_blk0 + b * n_c + t, b * n_s + (t - n_c))

    hb = RET_HEAD_BLOCK
    spec = lambda width: pl.BlockSpec((ch, hb * width), lambda b, h, t: (row_block(b, t), h))
    extra_specs, extra = [], ()
    if finish is not None:
        extra_specs = [spec(RET_DV), spec(RET_DV), pl.BlockSpec((1, hb * RET_DV), lambda b, h, t: (0, h))]
        extra = tuple(finish)
    return pl.pallas_call(
        functools.partial(_ret_kernel, backward=backward, finish=finish is not None),
        grid=(batch, RET_HEADS // hb, n_c + n_s),
        in_specs=[pl.BlockSpec((hb, 1, LANES), lambda b, h, t: (h, 0, 0)), spec(RET_DK), spec(RET_DK), spec(RET_DV)]
        + extra_specs,
        out_specs=spec(RET_DV),
        out_shape=jax.ShapeDtypeStruct((q.shape[0], RET_HEADS * RET_DV), BF16),
        scratch_shapes=[pltpu.VMEM((hb, RET_DK, RET_DV), F32), pltpu.VMEM((hb, ch, ch), F32),
                        pltpu.VMEM((hb, ch, LANES), F32), pltpu.VMEM((hb, ch, LANES), F32)],
        compiler_params=_params(("parallel", "parallel", "arbitrary")),
        name="retention_bwd" if backward else "retention_fwd",
    )(dec, q, k, v, *extra)


def _moe_kernel(te_ref, nu_ref, first_ref, nxt_ref, x_ref, wg_hbm, wu_hbm, wd_hbm, o_ref,
                stage_g, stage_u, stage_d, cur_g, cur_u, cur_d, sem, *, layer):
    t = pl.program_id(0)

    def weight_copies(e):
        return (pltpu.make_async_copy(wg_hbm.at[layer, e], stage_g, sem.at[0]),
                pltpu.make_async_copy(wu_hbm.at[layer, e], stage_u, sem.at[1]),
                pltpu.make_async_copy(wd_hbm.at[layer, e], stage_d, sem.at[2]))

    @pl.when(t == 0)
    def _():
        for cp in weight_copies(te_ref[0]):
            cp.start()

    @pl.when((t < nu_ref[0]) & (first_ref[t] == 1))
    def _():
        for cp in weight_copies(te_ref[t]):
            cp.wait()
        for stage, cur in ((stage_g, cur_g), (stage_u, cur_u), (stage_d, cur_d)):
            rows = stage.shape[0] // WEIGHT_CAST_CHUNKS

            def cast(c, carry, stage=stage, cur=cur, rows=rows):
                r0 = pl.multiple_of(c * rows, rows)
                cur[pl.ds(r0, rows), :] = stage[pl.ds(r0, rows), :].astype(BF16)
                return carry

            lax.fori_loop(0, WEIGHT_CAST_CHUNKS, cast, 0)

        @pl.when(nxt_ref[t] >= 0)
        def _():
            for cp in weight_copies(nxt_ref[t]):
                cp.start()

    @pl.when(t < nu_ref[0])
    def _():
        x = x_ref[...].astype(BF16)
        a = jnp.dot(x, cur_g[...], preferred_element_type=F32)
        u = jnp.dot(x, cur_u[...], preferred_element_type=F32)
        act = (a * jax.nn.sigmoid(a) * u).astype(BF16)
        o_ref[...] = jnp.dot(act, cur_d[...], preferred_element_type=F32).astype(o_ref.dtype)

    @pl.when(t >= nu_ref[0])
    def _():
        o_ref[...] = jnp.zeros_like(o_ref)


def _moe_ffn(tile_expert, n_used, first, nxt, x_sorted, wg, wu, wd, layer):
    r, d = x_sorted.shape
    f = wg.shape[3]
    tm = MOE_ROW_TILE
    any_spec = pl.BlockSpec(memory_space=pl.ANY)
    return pl.pallas_call(
        functools.partial(_moe_kernel, layer=layer),
        grid_spec=pltpu.PrefetchScalarGridSpec(
            num_scalar_prefetch=4,
            grid=(r // tm,),
            in_specs=[pl.BlockSpec((tm, d), lambda t, *_: (t, 0)), any_spec, any_spec, any_spec],
            out_specs=pl.BlockSpec((tm, d), lambda t, *_: (t, 0)),
            scratch_shapes=[pltpu.VMEM((d, f), F32), pltpu.VMEM((d, f), F32), pltpu.VMEM((f, d), F32),
                            pltpu.VMEM((d, f), BF16), pltpu.VMEM((d, f), BF16), pltpu.VMEM((f, d), BF16),
                            pltpu.SemaphoreType.DMA((3,))],
        ),
        out_shape=jax.ShapeDtypeStruct((r, d), F32),
        compiler_params=_params(("arbitrary",)),
        name="moe_ffn",
    )(tile_expert, n_used, first, nxt, x_sorted, wg, wu, wd)


def _combine_kernel(x_ref, a_ref, b_ref, w_ref, g_ref, *rest):
    w = w_ref[...]
    xn = x_ref[...] + g_ref[0] * (w[:, 0:1] * a_ref[...] + w[:, 1:2] * b_ref[...])
    rest[-1 if len(rest) == 1 else -2][...] = xn
    if len(rest) > 1:
        gn_ref, sh_ref, sc_ref, _, h_ref = rest
        h_ref[...] = (_rms(xn, xn.shape[-1]) * gn_ref[...] * (1.0 + sc_ref[0]) + sh_ref[0]).astype(h_ref.dtype)


def _combine(x, ab, w_col, gate, rows_per_group, n_groups, next_norm=None):
    n, d = x.shape
    tm = ROW_TILE
    row = pl.BlockSpec((tm, d), lambda i: (i, 0))
    gs = _group_spec(d, tm, rows_per_group, n_groups)
    in_specs = [row, row, pl.BlockSpec((tm, d), lambda i: (i + n // tm, 0)),
                pl.BlockSpec((tm, LANES), lambda i: (i, 0)), gs]
    args = [x, ab, ab, w_col, gate]
    out_specs, out_shape = row, jax.ShapeDtypeStruct((n, d), F32)
    if next_norm is not None:
        gn, sh, sc = next_norm
        in_specs += [pl.BlockSpec((1, d), lambda i: (0, 0)), gs, gs]
        args += [gn.reshape(1, d), sh, sc]
        out_specs, out_shape = [row, row], [out_shape, jax.ShapeDtypeStruct((n, d), BF16)]
    return pl.pallas_call(
        _combine_kernel,
        grid=(n // tm,),
        in_specs=in_specs,
        out_specs=out_specs,
        out_shape=out_shape,
        compiler_params=_params(("parallel",)),
        name="moe_combine",
    )(*args)


def _moe_layer(x, g, sh, sc, gate, rw_pad, rb_col, wg, wu, wd, layer, rows_per_group, n_groups, next_norm):
    n, d = x.shape
    tm = MOE_ROW_TILE
    h, idx8, w_col, cnt = _norm_mod_route(x, g, sh, sc, rw_pad, rb_col, rows_per_group, n_groups)
    experts = jnp.arange(N_EXPERTS, dtype=jnp.int32)
    padded = (cnt[:, 0] + tm - 1) // tm * tm
    ends = jnp.cumsum(padded)
    starts = ends - padded
    e2, rank2 = idx8[0:2], idx8[2:4]
    dest = rank2 + jnp.sum(jnp.where(e2[..., None] == experts, starts, 0), axis=-1)
    r = (2 * n + N_EXPERTS * (tm - 1)) // tm * tm
    tok = jnp.tile(jnp.arange(n, dtype=jnp.int32), 2)
    src = (jnp.arange(r, dtype=jnp.int32) % n).at[dest.reshape(-1)].set(tok, unique_indices=True,
                                                                         mode="promise_in_bounds")
    tile_start = jnp.arange(r // tm, dtype=jnp.int32) * tm
    tile_expert = jnp.minimum(jnp.sum((ends[None, :] <= tile_start[:, None]).astype(jnp.int32), axis=1),
                              N_EXPERTS - 1)
    n_used = ends[-1:] // tm
    first = jnp.concatenate([jnp.ones((1,), jnp.int32), (tile_expert[1:] != tile_expert[:-1]).astype(jnp.int32)])
    later = (padded[None, :] > 0) & (experts[None, :] > experts[:, None])
    next_of = jnp.min(jnp.where(later, experts[None, :], N_EXPERTS), axis=1)
    next_of = jnp.where(next_of < N_EXPERTS, next_of, -1)
    nxt = jnp.sum(jnp.where(tile_expert[:, None] == experts, next_of, 0), axis=1)
    x_sorted = h.at[src].get(mode="promise_in_bounds")
    y_sorted = _moe_ffn(tile_expert, n_used, first, nxt, x_sorted, wg, wu, wd, layer)
    ab = y_sorted.at[dest.reshape(-1)].get(mode="promise_in_bounds")
    return _combine(x, ab, w_col, gate, rows_per_group, n_groups, next_norm)


def _axial_tables(rows, rot_dim):
    n_freq = rot_dim // 4
    inv_freq = jnp.power(ROPE_BASE, -jnp.arange(n_freq, dtype=F32) / n_freq)
    row = jnp.repeat(jnp.arange(rows, dtype=F32), GRID_W)
    col = jnp.tile(jnp.arange(GRID_W, dtype=F32), rows)
    ang = jnp.concatenate([row[:, None] * inv_freq, col[:, None] * inv_freq], axis=-1)
    return jnp.cos(ang), jnp.sin(ang)


def _flat_tables(c_lat, s_lat, batch, n_ctx_rows):
    c = jnp.concatenate([jnp.tile(c_lat, (batch, 1)), jnp.broadcast_to(c_lat[:1], (n_ctx_rows, LANES))])
    s = jnp.concatenate([jnp.tile(s_lat, (batch, 1)), jnp.zeros((n_ctx_rows, LANES), F32)])
    return c, s


def _spread_pairs(a, axis):
    x1, x2 = jnp.split(a, 2, axis=axis)
    z = jnp.zeros_like(x1)
    return jnp.concatenate([x1, z, x2, z], axis=axis)


def kernel(x, c, ctx, c_ctx, ada_w, ada_b, norm_mix_g, norm_ffn_g, mla_w_in, mla_q_a_g, mla_w_q_b, mla_kv_a_g,
           mla_w_kv_b, mla_q_norm_g, mla_k_norm_g, mla_w_o, diff_w_in, diff_q_norm_g, diff_k_norm_g, diff_lambda,
           diff_subln_g, diff_w_o, ret_w_in, ret_decay_fwd, ret_decay_bwd, ret_norm_g, ret_w_o, router_w,
           router_bias, moe_w_gate, moe_w_up, moe_w_down):
    batch, seq, d = x.shape
    n_ctx = ctx.shape[1]
    depth = ada_w.shape[0]
    n_lat, n_cx = batch * seq, batch * n_ctx
    n_groups = batch + 1
    assert n_groups <= SUBLANES and seq % MM_ROW_TILE == 0 and n_cx % MM_ROW_TILE == 0
    assert seq % RET_CHUNK == 0 and n_ctx % RET_CHUNK == 0 and n_lat % n_ctx == 0 and n_ctx == ROW_TILE

    x_lat, x_ctx = x.reshape(n_lat, d), ctx.reshape(n_cx, d)
    xa = None
    n_all = n_lat + n_cx

    cond = jnp.concatenate([c, c_ctx[None, :], jnp.zeros((SUBLANES - n_groups, d), F32)], axis=0)
    mods = _ada_mods(cond.T, ada_w, ada_b, n_groups)

    rows = seq // GRID_W
    cos64, sin64 = _axial_tables(rows, MLA_ROPE)
    zeros32 = jnp.zeros_like(cos64)
    mla_c, mla_s = _flat_tables(jnp.concatenate([cos64, zeros32, cos64, zeros32], axis=1),
                                jnp.concatenate([-sin64, zeros32, sin64, zeros32], axis=1), batch, n_cx)
    cos128, sin128 = _axial_tables(rows, DIFF_HEAD_DIM)
    diff_c, diff_s = _flat_tables(jnp.concatenate([cos128, cos128], axis=1),
                                  jnp.concatenate([-sin128, sin128], axis=1), batch, n_cx)
    cos256, sin256 = _axial_tables(rows, RET_DK)
    ret_c, ret_s = _flat_tables(cos256, sin256, batch, n_cx)

    rw_pad = jnp.concatenate([router_w, router_w, jnp.zeros((d, LANES - 2 * N_EXPERTS), F32)], axis=1)
    rb_col = router_bias.reshape(N_EXPERTS, 1)

    for i in range(depth):
        kind, j, last = i % N_MIXERS, i // N_MIXERS, i == depth - 1
        m = mods[i].reshape(SUBLANES, 6, 1, d)
        sh_m, sc_m, g_m, sh_f, sc_f, g_f = (m[:, t] for t in range(6))
        if i == 0:
            h = _norm_mod(x_lat, x_ctx, norm_mix_g[i], sh_m, sc_m, seq, n_groups)
        q_rows = n_lat if last else n_all

        if kind == 0:
            w_in = jnp.concatenate([mla_w_in[j][:, :MLA_Q_LORA + MLA_KV_LORA],
                                    _spread_pairs(mla_w_in[j][:, MLA_Q_LORA + MLA_KV_LORA:], 1)], axis=1)
            z = _proj(h, w_in, 0, w_in.shape[1], out_dtype=F32, tn=w_in.shape[1])
            wq = mla_w_q_b[j].reshape(MLA_Q_LORA, MLA_HEADS, MLA_NOPE + MLA_ROPE)
            wq = jnp.concatenate([wq[..., :MLA_NOPE], _spread_pairs(wq[..., MLA_NOPE:], 2)], axis=2)
            wq = wq.reshape(MLA_Q_LORA, MLA_HEADS * MLA_HEAD_PAD).astype(BF16)
            scale = float(MLA_NOPE + MLA_ROPE) ** -0.5 * LOG2E
            qg, kg = mla_q_norm_g[j], mla_k_norm_g[j]
            q = _mla_q(z, mla_q_a_g[j], wq, qg[:MLA_NOPE] * scale,
                       *_fold_rope(mla_c, mla_s, _spread_pairs(qg[MLA_NOPE:], 0), scale), q_rows)
            k, v = _mla_kv(z, mla_kv_a_g[j], mla_w_kv_b[j].astype(BF16), kg[:MLA_NOPE],
                           *_fold_rope(mla_c, mla_s, _spread_pairs(kg[MLA_NOPE:], 0), 1.0))
            attn = functools.partial(_attention, q, k, v, heads=MLA_HEADS, q_width=MLA_HEAD_PAD,
                                     v_width=MLA_V, dv=MLA_V, seq=seq, n_ctx=n_ctx, batch=batch,
                                     q_tile=MLA_Q_TILE)
            w_o = mla_w_o[j]
        elif kind == 1:
            w_in = diff_w_in[j]
            hd = DIFF_HEADS * 2 * DIFF_HEAD_DIM
            scale = float(DIFF_HEAD_DIM) ** -0.5 * LOG2E
            q = _proj(h[:q_rows], w_in, 0, hd, mode="norm_rope",
                      extras=_fold_rope(diff_c[:q_rows], diff_s[:q_rows], diff_q_norm_g[j], scale))
            k = _proj(h, w_in, hd, hd, mode="norm_rope", extras=_fold_rope(diff_c, diff_s, diff_k_norm_g[j], 1.0))
            v = _proj(h, w_in, 2 * hd, hd)
            lam_init = 0.8 - 0.6 * math.exp(-0.3 * i)
            attn = functools.partial(_attention, q, k, v, heads=DIFF_HEADS, q_width=2 * DIFF_HEAD_DIM,
                                     v_width=2 * DIFF_HEAD_DIM, dv=2 * DIFF_HEAD_DIM, seq=seq, n_ctx=n_ctx,
                                     batch=batch, q_tile=ATTN_Q_TILE,
                                     diff=(diff_lambda[j], diff_subln_g[j], lam_init))
            w_o = diff_w_o[j]
        else:
            w_in = ret_w_in[j]
            nq, nv = RET_HEADS * RET_DK, RET_HEADS * RET_DV
            q = _proj(h, w_in, 0, nq, mode="rope256", extras=(ret_c, ret_s))
            k_scale = float(RET_DK) ** -0.5
            k = _proj(h, w_in, nq, nq, mode="rope256", extras=(ret_c * k_scale, ret_s * k_scale))
            v = _proj(h, w_in, 2 * nq, nv)
            gate = _proj(h, w_in, 2 * nq + nv, nv, mode="silu")
            dec = lambda p: jnp.broadcast_to(p.astype(F32)[:, None, None], (RET_HEADS, 1, LANES))
            o_b = _retention(dec(ret_decay_bwd[j]), q, k, v, backward=True, seq=seq, n_ctx=n_ctx, batch=batch)
            y = _retention(dec(ret_decay_fwd[j]), q, k, v, backward=False, seq=seq, n_ctx=n_ctx, batch=batch,
                           finish=(o_b, gate, ret_norm_g[j].reshape(1, -1)))
            attn = None
            w_o = ret_w_o[j]

        y_tail = None
        if attn is not None:
            y = attn(lat_queries=True)
            if not last:
                y_tail = attn(lat_queries=False)
        res_in, res_tail = (x_lat, x_ctx) if xa is None else (xa, None)
        xa = _out_res(y, w_o, res_in, g_m, seq, n_groups, n_lat if last else n_all, y_tail, res_tail)
        next_norm = None
        if not last:
            m_next = mods[i + 1].reshape(SUBLANES, 6, 1, d)
            next_norm = (norm_mix_g[i + 1], m_next[:, 0], m_next[:, 1])
        res = _moe_layer(xa, norm_ffn_g[i], sh_f, sc_f, g_f, rw_pad, rb_col, moe_w_gate, moe_w_up, moe_w_down, i,
                         seq, n_groups, next_norm)
        xa, h = (res, None) if last else res
    return xa[:n_lat].reshape(batch, seq, d)
```

```python
import functools
import math

import jax
import jax.numpy as jnp
from jax import lax
from jax.experimental import pallas as pl
from jax.experimental.pallas import tpu as pltpu

F32 = jnp.float32
BF16 = jnp.bfloat16

GRID_W = 64
ROPE_BASE = 10000.0
NORM_EPS = 1e-6
N_MIXERS = 3

MLA_HEADS = 16
MLA_Q_LORA = 512
MLA_KV_LORA = 512
MLA_NOPE = 128
MLA_ROPE = 64
MLA_V = 128
MLA_HEAD_PAD = 256

DIFF_HEADS = 8
DIFF_HEAD_DIM = 128

RET_HEADS = 8
RET_DK = 256
RET_DV = 512
RET_CHUNK = 256

N_EXPERTS = 16
N_GROUPS = 4
EXPERTS_PER_GROUP = 4
D_EXPERT = 1024

LANES = 128
SUBLANES = 8
VMEM_LIMIT = 56 * 1024 * 1024

ROW_TILE = 256
MM_ROW_TILE = 512
MM_COL_TILE = 2048
MM_COL_TILE_MAX_K = 2048
MLA_COL_TILE = MLA_HEADS * MLA_HEAD_PAD
RET_HEAD_BLOCK = 4
WEIGHT_CAST_CHUNKS = 8
MOE_ROW_TILE = 256
KEY_CHUNK = 512
ATTN_Q_TILE = 512
MLA_Q_TILE = 1024
LOG2E = 1.4426950408889634
ADA_COL_TILE = 1024


def _params(sem):
    return pltpu.CompilerParams(dimension_semantics=sem, vmem_limit_bytes=VMEM_LIMIT)


def _rms(x, width):
    return x * lax.rsqrt(jnp.sum(x * x, axis=-1, keepdims=True) / width + NORM_EPS)


def _ada_kernel(ct_ref, w_ref, b_ref, o_ref, sb_ref, *, n_cond):
    d = ct_ref.shape[0]
    tn = o_ref.shape[-1]

    @pl.when((pl.program_id(0) == 0) & (pl.program_id(1) == 0))
    def _():
        ct = ct_ref[...]
        s = ct * jax.nn.sigmoid(ct)
        for r in range(n_cond):
            sb_ref[r] = jnp.broadcast_to(s[:, r:r + 1], (d, LANES))

    def body(kc, accs):
        k0 = pl.multiple_of(kc * SUBLANES, SUBLANES)
        w8 = w_ref[0, pl.ds(k0, SUBLANES), :]
        out = []
        for r in range(n_cond):
            s8 = sb_ref[r, pl.ds(k0, SUBLANES), :]
            out.append(accs[r] + w8 * jnp.concatenate([s8] * (tn // LANES), axis=1))
        return tuple(out)

    accs = lax.fori_loop(0, d // SUBLANES, body,
                         tuple(jnp.zeros((SUBLANES, tn), F32) for _ in range(n_cond)), unroll=4)
    rows = [jnp.sum(a, axis=0, keepdims=True) + b_ref[0] for a in accs]
    rows.append(jnp.zeros((SUBLANES - n_cond, tn), F32))
    o_ref[0] = jnp.concatenate(rows, axis=0)


def _ada_mods(cond_t, ada_w, ada_b, n_cond):
    depth, d, n6 = ada_w.shape
    tn = ADA_COL_TILE
    return pl.pallas_call(
        functools.partial(_ada_kernel, n_cond=n_cond),
        grid=(depth, n6 // tn),
        in_specs=[pl.BlockSpec((d, SUBLANES), lambda l, j: (0, 0)),
                  pl.BlockSpec((1, d, tn), lambda l, j: (l, 0, j)),
                  pl.BlockSpec((1, 1, tn), lambda l, j: (l, 0, j))],
        out_specs=pl.BlockSpec((1, SUBLANES, tn), lambda l, j: (l, 0, j)),
        out_shape=jax.ShapeDtypeStruct((depth, SUBLANES, n6), F32),
        scratch_shapes=[pltpu.VMEM((n_cond, d, LANES), F32)],
        compiler_params=_params(("arbitrary", "arbitrary")),
        name="ada_mods",
    )(cond_t, ada_w, ada_b.reshape(depth, 1, n6))


def _group_spec(d, tm, rows_per_group, n_groups):
    return pl.BlockSpec((1, 1, d), lambda i, *_: (jnp.minimum(i * tm // rows_per_group, n_groups - 1), 0, 0))


def _split_rows_specs(head_rows, tm, width, row_axis):
    head_tiles = head_rows // tm
    pick = lambda idx: idx[row_axis]
    head = pl.BlockSpec((tm, width), lambda *idx: (jnp.minimum(pick(idx), head_tiles - 1), 0))
    tail = pl.BlockSpec((tm, width), lambda *idx: (jnp.maximum(pick(idx) - head_tiles, 0), 0))
    return head, tail


def _norm_mod_kernel(x_ref, x2_ref, g_ref, sh_ref, sc_ref, o_ref, *, head_tiles):
    x = jnp.where(pl.program_id(0) < head_tiles, x_ref[...], x2_ref[...])
    h = _rms(x, x.shape[-1]) * g_ref[...] * (1.0 + sc_ref[0]) + sh_ref[0]
    o_ref[...] = h.astype(o_ref.dtype)


def _norm_mod(x, x2, g, sh, sc, rows_per_group, n_groups):
    n, d = x.shape[0] + x2.shape[0], x.shape[1]
    tm = ROW_TILE
    gs = _group_spec(d, tm, rows_per_group, n_groups)
    return pl.pallas_call(
        functools.partial(_norm_mod_kernel, head_tiles=x.shape[0] // tm),
        grid=(n // tm,),
        in_specs=[*_split_rows_specs(x.shape[0], tm, d, 0), pl.BlockSpec((1, d), lambda i: (0, 0)), gs, gs],
        out_specs=pl.BlockSpec((tm, d), lambda i: (i, 0)),
        out_shape=jax.ShapeDtypeStruct((n, d), BF16),
        compiler_params=_params(("parallel",)),
        name="norm_mod",
    )(x, x2, g.reshape(1, d), sh, sc)


def _route_kernel(x_ref, g_ref, sh_ref, sc_ref, rw_ref, rb_ref, h_ref, idx_ref, wt_ref, cnt_ref, carry_ref,
                  rwp_ref):
    @pl.when(pl.program_id(0) == 0)
    def _():
        carry_ref[...] = jnp.zeros_like(carry_ref)
        w = rw_ref[...]
        hi = w.astype(BF16)
        lo = (w - hi.astype(F32)).astype(BF16)
        rwp_ref[...] = jnp.where(lax.broadcasted_iota(jnp.int32, w.shape, 1) < N_EXPERTS, hi, lo)

    x = x_ref[...]
    tm = x.shape[0]
    h = _rms(x, x.shape[-1]) * g_ref[...] * (1.0 + sc_ref[0]) + sh_ref[0]
    h_ref[...] = h.astype(h_ref.dtype)
    h_hi = h.astype(BF16)
    h_lo = (h - h_hi.astype(F32)).astype(BF16)
    parts = (jnp.dot(h_hi, rwp_ref[...], preferred_element_type=F32)
             + jnp.dot(h_lo, rwp_ref[...], preferred_element_type=F32)).T
    logits = parts[:N_EXPERTS] + parts[N_EXPERTS:2 * N_EXPERTS]
    scores = jax.nn.sigmoid(logits)
    sel = scores + rb_ref[...]
    sel_r = [sel[e:e + 1, :] for e in range(N_EXPERTS)]
    sc_r = [scores[e:e + 1, :] for e in range(N_EXPERTS)]

    gscore = []
    for g in range(N_GROUPS):
        a, b, c, d = sel_r[4 * g:4 * g + 4]
        hi1, lo1, hi2, lo2 = jnp.maximum(a, b), jnp.minimum(a, b), jnp.maximum(c, d), jnp.minimum(c, d)
        gscore.append(jnp.maximum(hi1, hi2) + jnp.maximum(jnp.minimum(hi1, hi2), jnp.maximum(lo1, lo2)))
    grp = jnp.zeros((1, tm), jnp.int32)
    best = gscore[0]
    for g in range(1, N_GROUPS):
        better = gscore[g] > best
        grp = jnp.where(better, g, grp)
        best = jnp.where(better, gscore[g], best)

    def pick(rows, l):
        out = rows[l]
        for g in range(1, N_GROUPS):
            out = jnp.where(grp == g, rows[4 * g + l], out)
        return out

    v = [pick(sel_r, l) for l in range(EXPERTS_PER_GROUP)]
    s = [pick(sc_r, l) for l in range(EXPERTS_PER_GROUP)]

    def first_max(vals):
        m = jnp.maximum(jnp.maximum(vals[0], vals[1]), jnp.maximum(vals[2], vals[3]))
        l = jnp.where(vals[0] == m, 0, jnp.where(vals[1] == m, 1, jnp.where(vals[2] == m, 2, 3)))
        return l

    def at(vals, l):
        return jnp.where(l == 0, vals[0], jnp.where(l == 1, vals[1], jnp.where(l == 2, vals[2], vals[3])))

    l1 = first_max(v)
    l2 = first_max([jnp.where(l1 == l, -jnp.inf, v[l]) for l in range(EXPERTS_PER_GROUP)])
    s1, s2 = at(s, l1), at(s, l2)
    tot = s1 + s2
    e1, e2 = grp * EXPERTS_PER_GROUP + l1, grp * EXPERTS_PER_GROUP + l2

    eids = lax.broadcasted_iota(jnp.int32, (N_EXPERTS, tm), 0)
    oh1, oh2 = (eids == e1).astype(F32), (eids == e2).astype(F32)
    before = (lax.broadcasted_iota(jnp.int32, (tm, tm), 0) < lax.broadcasted_iota(jnp.int32, (tm, tm), 1)).astype(BF16)
    carry = carry_ref[:, :1]
    tot1 = jnp.sum(oh1, axis=1, keepdims=True)
    pre1 = carry + jnp.dot(oh1.astype(BF16), before, preferred_element_type=F32)
    pre2 = carry + tot1 + jnp.dot(oh2.astype(BF16), before, preferred_element_type=F32)
    r1 = jnp.sum(oh1 * pre1, axis=0, keepdims=True).astype(jnp.int32)
    r2 = jnp.sum(oh2 * pre2, axis=0, keepdims=True).astype(jnp.int32)
    carry_new = carry + tot1 + jnp.sum(oh2, axis=1, keepdims=True)
    carry_ref[...] = jnp.broadcast_to(carry_new, carry_ref.shape)
    cnt_ref[...] = jnp.broadcast_to(carry_new, cnt_ref.shape).astype(jnp.int32)

    idx_ref[...] = jnp.concatenate([e1, e2, r1, r2, jnp.zeros((SUBLANES - 4, tm), jnp.int32)], axis=0)
    wt_ref[...] = jnp.concatenate([s1 / tot, s2 / tot, jnp.zeros((LANES - 2, tm), F32)], axis=0).T


def _norm_mod_route(x, g, sh, sc, rw_pad, rb_col, rows_per_group, n_groups):
    n, d = x.shape
    tm = ROW_TILE
    gs = _group_spec(d, tm, rows_per_group, n_groups)
    return pl.pallas_call(
        _route_kernel,
        grid=(n // tm,),
        in_specs=[pl.BlockSpec((tm, d), lambda i: (i, 0)), pl.BlockSpec((1, d), lambda i: (0, 0)), gs, gs,
                  pl.BlockSpec((d, LANES), lambda i: (0, 0)), pl.BlockSpec((N_EXPERTS, 1), lambda i: (0, 0))],
        out_specs=[pl.BlockSpec((tm, d), lambda i: (i, 0)),
                   pl.BlockSpec((SUBLANES, tm), lambda i: (0, i)),
                   pl.BlockSpec((tm, LANES), lambda i: (i, 0)),
                   pl.BlockSpec((N_EXPERTS, LANES), lambda i: (0, 0))],
        out_shape=[jax.ShapeDtypeStruct((n, d), F32),
                   jax.ShapeDtypeStruct((SUBLANES, n), jnp.int32),
                   jax.ShapeDtypeStruct((n, LANES), F32),
                   jax.ShapeDtypeStruct((N_EXPERTS, LANES), jnp.int32)],
        scratch_shapes=[pltpu.VMEM((N_EXPERTS, LANES), F32), pltpu.VMEM((d, LANES), BF16)],
        compiler_params=_params(("arbitrary",)),
        name="norm_mod_route",
    )(x, g.reshape(1, d), sh, sc, rw_pad, rb_col)


def _half_swap_rope(x, c, s):
    return x * c + pltpu.roll(x, LANES // 2, 1) * s


def _row_sums(sq):
    return jnp.dot(sq.astype(BF16), jnp.ones((sq.shape[1], LANES), BF16), preferred_element_type=F32)


def _fold_rope(c_t, s_t, g, scale):
    return c_t * (g * scale)[None, :], s_t * (jnp.roll(g, LANES // 2) * scale)[None, :]


def _proj_kernel(*refs, mode):
    x_ref, w_ref = refs[0], refs[1]
    o_ref, wb_ref = refs[-2], refs[-1]

    @pl.when(pl.program_id(1) == 0)
    def _():
        wb_ref[...] = w_ref[...].astype(BF16)

    acc = jnp.dot(x_ref[...], wb_ref[...], preferred_element_type=F32)
    tn = acc.shape[1]
    if mode == "plain":
        o_ref[...] = acc.astype(o_ref.dtype)
    elif mode == "silu":
        o_ref[...] = (acc * jax.nn.sigmoid(acc)).astype(o_ref.dtype)
    elif mode == "norm_rope":
        gc, gs = refs[2][...], refs[3][...]
        for j in range(tn // LANES):
            blk = acc[:, j * LANES:(j + 1) * LANES]
            blk = blk * lax.rsqrt(_row_sums(blk * blk) / LANES + NORM_EPS)
            o_ref[:, j * LANES:(j + 1) * LANES] = _half_swap_rope(blk, gc, gs).astype(o_ref.dtype)
    elif mode == "rope256":
        c, s = refs[2][...], refs[3][...]
        for j in range(tn // (2 * LANES)):
            x1 = acc[:, (2 * j) * LANES:(2 * j + 1) * LANES]
            x2 = acc[:, (2 * j + 1) * LANES:(2 * j + 2) * LANES]
            o_ref[:, (2 * j) * LANES:(2 * j + 1) * LANES] = (x1 * c - x2 * s).astype(o_ref.dtype)
            o_ref[:, (2 * j + 1) * LANES:(2 * j + 2) * LANES] = (x2 * c + x1 * s).astype(o_ref.dtype)
    else:
        raise ValueError(mode)


def _col_tile(k, f32_tiles):
    tn = MM_COL_TILE if k <= MM_COL_TILE_MAX_K else MM_COL_TILE // 2
    return tn // 2 if f32_tiles else tn


def _proj(x, w, col0, n_cols, *, mode="plain", extras=(), out_dtype=BF16, tn=None):
    n, k = x.shape
    tm = MM_ROW_TILE
    tn = min(_col_tile(k, False) if tn is None else tn, n_cols)
    j0 = col0 // tn
    extra_specs = []
    for e in extras:
        if e.shape[0] == 1:
            extra_specs.append(pl.BlockSpec(e.shape, lambda j, i: (0, 0)))
        else:
            extra_specs.append(pl.BlockSpec((tm, e.shape[1]), lambda j, i: (i, 0)))
    return pl.pallas_call(
        functools.partial(_proj_kernel, mode=mode),
        grid=(n_cols // tn, n // tm),
        in_specs=[pl.BlockSpec((tm, k), lambda j, i: (i, 0)),
                  pl.BlockSpec((k, tn), lambda j, i: (0, j0 + j))] + extra_specs,
        out_specs=pl.BlockSpec((tm, tn), lambda j, i: (i, j)),
        out_shape=jax.ShapeDtypeStruct((n, n_cols), out_dtype),
        scratch_shapes=[pltpu.VMEM((k, tn), BF16)],
        compiler_params=_params(("arbitrary", "arbitrary")),
        name="proj_" + mode,
    )(x, w, *extras)


def _out_res_kernel(*refs, y_head_tiles, x_head_tiles):
    refs = list(refs)
    i = pl.program_id(1)
    y = refs.pop(0)[...] if y_head_tiles is None else jnp.where(i < y_head_tiles, refs.pop(0)[...], refs.pop(0)[...])
    w_ref = refs.pop(0)
    x = refs.pop(0)[...] if x_head_tiles is None else jnp.where(i < x_head_tiles, refs.pop(0)[...], refs.pop(0)[...])
    g_ref, o_ref, wb_ref = refs

    @pl.when(i == 0)
    def _():
        wb_ref[...] = w_ref[...].astype(BF16)

    o_ref[...] = x + g_ref[0] * jnp.dot(y, wb_ref[...], preferred_element_type=F32)


def _out_res(y, w, x, gate, rows_per_group, n_groups, rows, y_tail=None, x_tail=None):
    n, k = rows, y.shape[1]
    d = w.shape[1]
    tm = MM_ROW_TILE
    tn = _col_tile(k, True)
    y_specs = [pl.BlockSpec((tm, k), lambda j, i: (i, 0))] if y_tail is None else list(
        _split_rows_specs(y.shape[0], tm, k, 1))
    if x_tail is None:
        x_specs = [pl.BlockSpec((tm, tn), lambda j, i: (i, j))]
    else:
        xt = x.shape[0] // tm
        x_specs = [pl.BlockSpec((tm, tn), lambda j, i: (jnp.minimum(i, xt - 1), j)),
                   pl.BlockSpec((tm, tn), lambda j, i: (jnp.maximum(i - xt, 0), j))]
    args = [y] + ([] if y_tail is None else [y_tail]) + [w, x] + ([] if x_tail is None else [x_tail]) + [gate]
    return pl.pallas_call(
        functools.partial(_out_res_kernel, y_head_tiles=None if y_tail is None else y.shape[0] // tm,
                          x_head_tiles=None if x_tail is None else x.shape[0] // tm),
        grid=(d // tn, n // tm),
        in_specs=y_specs + [pl.BlockSpec((k, tn), lambda j, i: (0, j))] + x_specs
        + [pl.BlockSpec((1, 1, tn), lambda j, i: (jnp.minimum(i * tm // rows_per_group, n_groups - 1), 0, j))],
        out_specs=pl.BlockSpec((tm, tn), lambda j, i: (i, j)),
        out_shape=jax.ShapeDtypeStruct((n, d), F32),
        scratch_shapes=[pltpu.VMEM((k, tn), BF16)],
        compiler_params=_params(("arbitrary", "arbitrary")),
        name="out_res",
    )(*args)


def _mla_q_kernel(z_ref, ga_ref, w_ref, g_ref, gc_ref, gs_ref, o_ref):
    cq = z_ref[...]
    cn = (_rms(cq, cq.shape[-1]) * ga_ref[...]).astype(BF16)
    acc = jnp.dot(cn, w_ref[...], preferred_element_type=F32)
    g, gc, gs = g_ref[...], gc_ref[...], gs_ref[...]
    width = float(MLA_NOPE + MLA_ROPE)
    for j in range(acc.shape[1] // MLA_HEAD_PAD):
        qh = acc[:, j * MLA_HEAD_PAD:(j + 1) * MLA_HEAD_PAD]
        f = lax.rsqrt(_row_sums(qh * qh) / width + NORM_EPS)
        o_ref[:, j * MLA_HEAD_PAD:j * MLA_HEAD_PAD + LANES] = (qh[:, :LANES] * f * g).astype(o_ref.dtype)
        o_ref[:, j * MLA_HEAD_PAD + LANES:(j + 1) * MLA_HEAD_PAD] = _half_swap_rope(
            qh[:, LANES:] * f, gc, gs).astype(o_ref.dtype)


def _mla_q(z, q_a_g, w_q_b_pad, g_nope, gc, gs, rows, tn=MLA_COL_TILE):
    n = rows
    tm = MM_ROW_TILE
    n_out = w_q_b_pad.shape[1]
    return pl.pallas_call(
        _mla_q_kernel,
        grid=(n // tm, n_out // tn),
        in_specs=[pl.BlockSpec((tm, MLA_Q_LORA), lambda i, j: (i, 0)),
                  pl.BlockSpec((1, MLA_Q_LORA), lambda i, j: (0, 0)),
                  pl.BlockSpec((MLA_Q_LORA, tn), lambda i, j: (0, j)),
                  pl.BlockSpec((1, LANES), lambda i, j: (0, 0)),
                  pl.BlockSpec((tm, LANES), lambda i, j: (i, 0)),
                  pl.BlockSpec((tm, LANES), lambda i, j: (i, 0))],
        out_specs=pl.BlockSpec((tm, tn), lambda i, j: (i, j)),
        out_shape=jax.ShapeDtypeStruct((n, n_out), BF16),
        compiler_params=_params(("parallel", "arbitrary")),
        name="mla_q",
    )(z, q_a_g.reshape(1, -1), w_q_b_pad, g_nope.reshape(1, LANES), gc, gs)


def _mla_kv_kernel(z_ref, kr_ref, ga_ref, w_ref, g_ref, gc_ref, gs_ref, k_ref, v_ref):
    ckv = z_ref[...]
    cn = (_rms(ckv, ckv.shape[-1]) * ga_ref[...]).astype(BF16)
    acc = jnp.dot(cn, w_ref[...], preferred_element_type=F32)
    kr = kr_ref[...]
    kr_ss = _row_sums(kr * kr)
    g, gc, gs = g_ref[...], gc_ref[...], gs_ref[...]
    width = float(MLA_NOPE + MLA_ROPE)
    for j in range(acc.shape[1] // MLA_HEAD_PAD):
        kn = acc[:, j * MLA_HEAD_PAD:j * MLA_HEAD_PAD + MLA_NOPE]
        vv = acc[:, j * MLA_HEAD_PAD + MLA_NOPE:(j + 1) * MLA_HEAD_PAD]
        f = lax.rsqrt((_row_sums(kn * kn) + kr_ss) / width + NORM_EPS)
        k_ref[:, j * MLA_HEAD_PAD:j * MLA_HEAD_PAD + LANES] = (kn * f * g).astype(k_ref.dtype)
        k_ref[:, j * MLA_HEAD_PAD + LANES:(j + 1) * MLA_HEAD_PAD] = _half_swap_rope(
            kr * f, gc, gs).astype(k_ref.dtype)
        v_ref[:, j * MLA_V:(j + 1) * MLA_V] = vv.astype(v_ref.dtype)


def _mla_kv(z, kv_a_g, w_kv_b, g_nope, gc, gs, tn=MLA_COL_TILE):
    n = z.shape[0]
    tm = MM_ROW_TILE
    n_out = w_kv_b.shape[1]
    kr_block = (MLA_Q_LORA + MLA_KV_LORA) // LANES
    return pl.pallas_call(
        _mla_kv_kernel,
        grid=(n // tm, n_out // tn),
        in_specs=[pl.BlockSpec((tm, MLA_KV_LORA), lambda i, j: (i, 1)),
                  pl.BlockSpec((tm, LANES), lambda i, j: (i, kr_block)),
                  pl.BlockSpec((1, MLA_KV_LORA), lambda i, j: (0, 0)),
                  pl.BlockSpec((MLA_KV_LORA, tn), lambda i, j: (0, j)),
                  pl.BlockSpec((1, LANES), lambda i, j: (0, 0)),
                  pl.BlockSpec((tm, LANES), lambda i, j: (i, 0)),
                  pl.BlockSpec((tm, LANES), lambda i, j: (i, 0))],
        out_specs=[pl.BlockSpec((tm, tn), lambda i, j: (i, j)),
                   pl.BlockSpec((tm, tn // 2), lambda i, j: (i, j))],
        out_shape=[jax.ShapeDtypeStruct((n, n_out), BF16),
                   jax.ShapeDtypeStruct((n, n_out // 2), BF16)],
        compiler_params=_params(("parallel", "arbitrary")),
        name="mla_kv",
    )(z, z, kv_a_g.reshape(1, -1), w_kv_b, g_nope.reshape(1, LANES), gc, gs)


def _softmax_pv(q, k_refs, v_refs, c0, dq, *, den_from_v):
    chunks = []
    for k_ref, v_ref in zip(k_refs, v_refs):
        step = min(k_ref.shape[0], KEY_CHUNK)
        chunks += [(k_ref, v_ref, r0, step) for r0 in range(0, k_ref.shape[0], step)]
    m = acc = den = None
    for k_ref, v_ref, r0, step in chunks:
        s = lax.dot_general(q, k_ref[r0:r0 + step, c0:c0 + dq], (((1,), (1,)), ((), ())),
                            preferred_element_type=F32)
        mc = jnp.max(s, axis=-1, keepdims=True)
        m_new = mc if m is None else jnp.maximum(m, mc)
        p = jnp.exp2((s - m_new).astype(BF16)) if den_from_v else jnp.exp2(s - m_new)
        vblk = v_ref[r0:r0 + step, :]
        if den_from_v:
            vblk = jnp.concatenate([vblk, jnp.ones((step, LANES), BF16)], axis=1)
        pv = jnp.dot(p.astype(BF16), vblk, preferred_element_type=F32)
        if m is None:
            acc = pv
            if not den_from_v:
                den = jnp.sum(p, axis=-1, keepdims=True)
        else:
            alpha = jnp.exp2(m - m_new)
            acc = acc * alpha + pv
            if not den_from_v:
                den = den * alpha + jnp.sum(p, axis=-1, keepdims=True)
        m = m_new
    if den_from_v:
        dv = acc.shape[1] - LANES
        return acc[:, :dv] / acc[:, dv:dv + 1]
    return acc / den


def _attn_kernel(*refs, n_seg, dq):
    q_ref = refs[0]
    k_refs = refs[1:1 + n_seg]
    v_refs = refs[1 + n_seg:1 + 2 * n_seg]
    o_ref = refs[-1]
    o_ref[...] = _softmax_pv(q_ref[...], k_refs, v_refs, 0, dq, den_from_v=True).astype(o_ref.dtype)


def _diff_attn_kernel(*refs, n_seg, dq, lam_init):
    q_ref = refs[0]
    k_refs = refs[1:1 + n_seg]
    v_refs = refs[1 + n_seg:1 + 2 * n_seg]
    lam_ref, g_ref, o_ref = refs[-3], refs[-2], refs[-1]
    lf = lam_ref[...]
    lam = (jnp.exp(jnp.sum(lf[0:1] * lf[1:2], axis=-1, keepdims=True))
           - jnp.exp(jnp.sum(lf[2:3] * lf[3:4], axis=-1, keepdims=True)) + lam_init)
    q = q_ref[...]
    o1 = _softmax_pv(q[:, :dq], k_refs, v_refs, 0, dq, den_from_v=False)
    o2 = _softmax_pv(q[:, dq:], k_refs, v_refs, dq, dq, den_from_v=False)
    o = o1 - lam * o2
    o_ref[...] = (_rms(o, o.shape[-1]) * g_ref[...] * (1.0 - lam_init)).astype(o_ref.dtype)


def _attention(q, k, v, *, heads, q_width, v_width, dv, seq, n_ctx, batch, lat_queries, q_tile, diff=None):
    lat_rows = batch * seq
    ctx_blk0 = lat_rows // n_ctx
    if lat_queries:
        tq = min(q_tile, seq)
        nq = seq // tq
        q_map = lambda b, h, i: (b * nq + i, h)
        segs = [(seq, lambda b, h, i: (b, h)), (n_ctx, lambda b, h, i: (ctx_blk0 + b, h))]
        out_rows = lat_rows
    else:
        tq = n_ctx
        nq = 1
        q_map = lambda b, h, i: (ctx_blk0 + b, h)
        segs = [(n_ctx, lambda b, h, i: (ctx_blk0 + b, h))]
        out_rows = batch * n_ctx
    n_seg = len(segs)
    in_specs = [pl.BlockSpec((tq, q_width), q_map)]
    in_specs += [pl.BlockSpec((rows, q_width), m) for rows, m in segs]
    in_specs += [pl.BlockSpec((rows, v_width), m) for rows, m in segs]
    args = [q] + [k] * n_seg + [v] * n_seg
    if diff is None:
        kern = functools.partial(_attn_kernel, n_seg=n_seg, dq=q_width)
    else:
        lam, subln_g, lam_init = diff
        kern = functools.partial(_diff_attn_kernel, n_seg=n_seg, dq=q_width // 2, lam_init=lam_init)
        in_specs += [pl.BlockSpec(lam.shape, lambda b, h, i: (0, 0)), pl.BlockSpec((1, dv), lambda b, h, i: (0, 0))]
        args += [lam, subln_g.reshape(1, dv)]
    return pl.pallas_call(
        kern,
        grid=(batch, heads, nq),
        in_specs=in_specs,
        out_specs=pl.BlockSpec((tq, dv), lambda b, h, i: (b * nq + i, h)),
        out_shape=jax.ShapeDtypeStruct((out_rows, heads * dv), BF16),
        compiler_params=_params(("parallel", "parallel", "arbitrary")),
        name="attention" if diff is None else "diff_attention",
    )(*args)


def _ret_kernel(dec_ref, q_ref, k_ref, v_ref, *rest, backward, finish):
    if finish:
        other_ref, gate_ref, ng_ref, o_ref, state_ref, intra_ref, qd_ref, kd_ref = rest
    else:
        o_ref, state_ref, intra_ref, qd_ref, kd_ref = rest
    ch = q_ref.shape[0]
    heads = state_ref.shape[0]

    @pl.when(pl.program_id(2) == 0)
    def _():
        state_ref[...] = jnp.zeros_like(state_ref)
        ii = lax.broadcasted_iota(jnp.int32, (ch, ch), 0).astype(F32)
        jj = lax.broadcasted_iota(jnp.int32, (ch, ch), 1).astype(F32)
        pos = lax.broadcasted_iota(jnp.int32, (ch, LANES), 0).astype(F32)
        if backward:
            dist, valid = jnp.maximum(jj - ii, 0.0), jj > ii
            q_pow, k_pow = ch - pos, pos
        else:
            dist, valid = jnp.maximum(ii - jj, 0.0), ii >= jj
            q_pow, k_pow = pos + 1.0, ch - 1.0 - pos
        for h in range(heads):
            lg = jax.nn.log_sigmoid(dec_ref[h])[:, :1]
            intra_ref[h] = jnp.where(valid, jnp.exp(lg * dist), 0.0)
            qd_ref[h] = jnp.exp(lg * q_pow)
            kd_ref[h] = jnp.exp(lg * k_pow)

    wide = lambda t: jnp.concatenate([t] * (RET_DK // LANES), axis=1)
    for h in range(heads):
        lg = jax.nn.log_sigmoid(dec_ref[h])[:, :1]
        intra = intra_ref[h]
        q = q_ref[:, h * RET_DK:(h + 1) * RET_DK]
        k = k_ref[:, h * RET_DK:(h + 1) * RET_DK]
        v = v_ref[:, h * RET_DV:(h + 1) * RET_DV]
        scores = lax.dot_general(q, k, (((1,), (1,)), ((), ())), preferred_element_type=F32) * intra
        state = state_ref[h]
        o = jnp.dot(scores.astype(BF16), v, preferred_element_type=F32)
        o += jnp.dot((q.astype(F32) * wide(qd_ref[h])).astype(BF16), state.astype(BF16),
                     preferred_element_type=F32)
        cols = slice(h * RET_DV, (h + 1) * RET_DV)
        if finish:
            y = o + other_ref[:, cols].astype(F32)
            o = gate_ref[:, cols].astype(F32) * (_rms(y, RET_DV) * ng_ref[:, cols])
        o_ref[:, cols] = o.astype(o_ref.dtype)
        kd_t = (k.astype(F32) * wide(kd_ref[h])).T.astype(BF16)
        state_ref[h] = state * jnp.exp(lg * ch) + jnp.dot(kd_t, v, preferred_element_type=F32)


def _retention(dec, q, k, v, *, backward, seq, n_ctx, batch, finish=None):
    ch = RET_CHUNK
    n_c, n_s = n_ctx // ch, seq // ch
    ctx_blk0 = batch * seq // ch

    def row_block(b, t):
        if backward:
            return jnp.where(t < n_c, ctx_blk0 + b * n_c + (n_c - 1 - t), b * n_s + (n_s - 1 - (t - n_c)))
        return jnp.where(t < n_c, ctx_blk0 + b * n_c + t, b * n_s + (t - n_c))

    hb = RET_HEAD_BLOCK
    spec = lambda width: pl.BlockSpec((ch, hb * width), lambda b, h, t: (row_block(b, t), h))
    extra_specs, extra = [], ()
    if finish is not None:
        extra_specs = [spec(RET_DV), spec(RET_DV), pl.BlockSpec((1, hb * RET_DV), lambda b, h, t: (0, h))]
        extra = tuple(finish)
    return pl.pallas_call(
        functools.partial(_ret_kernel, backward=backward, finish=finish is not None),
        grid=(batch, RET_HEADS // hb, n_c + n_s),
        in_specs=[pl.BlockSpec((hb, 1, LANES), lambda b, h, t: (h, 0, 0)), spec(RET_DK), spec(RET_DK), spec(RET_DV)]
        + extra_specs,
        out_specs=spec(RET_DV),
        out_shape=jax.ShapeDtypeStruct((q.shape[0], RET_HEADS * RET_DV), BF16),
        scratch_shapes=[pltpu.VMEM((hb, RET_DK, RET_DV), F32), pltpu.VMEM((hb, ch, ch), F32),
                        pltpu.VMEM((hb, ch, LANES), F32), pltpu.VMEM((hb, ch, LANES), F32)],
        compiler_params=_params(("parallel", "parallel", "arbitrary")),
        name="retention_bwd" if backward else "retention_fwd",
    )(dec, q, k, v, *extra)


def _moe_kernel(te_ref, nu_ref, first_ref, nxt_ref, x_ref, wg_hbm, wu_hbm, wd_hbm, o_ref,
                stage_g, stage_u, stage_d, cur_g, cur_u, cur_d, sem, *, layer):
    t = pl.program_id(0)

    def weight_copies(e):
        return (pltpu.make_async_copy(wg_hbm.at[layer, e], stage_g, sem.at[0]),
                pltpu.make_async_copy(wu_hbm.at[layer, e], stage_u, sem.at[1]),
                pltpu.make_async_copy(wd_hbm.at[layer, e], stage_d, sem.at[2]))

    @pl.when(t == 0)
    def _():
        for cp in weight_copies(te_ref[0]):
            cp.start()

    @pl.when((t < nu_ref[0]) & (first_ref[t] == 1))
    def _():
        for cp in weight_copies(te_ref[t]):
            cp.wait()
        for stage, cur in ((stage_g, cur_g), (stage_u, cur_u), (stage_d, cur_d)):
            rows = stage.shape[0] // WEIGHT_CAST_CHUNKS

            def cast(c, carry, stage=stage, cur=cur, rows=rows):
                r0 = pl.multiple_of(c * rows, rows)
                cur[pl.ds(r0, rows), :] = stage[pl.ds(r0, rows), :].astype(BF16)
                return carry

            lax.fori_loop(0, WEIGHT_CAST_CHUNKS, cast, 0)

        @pl.when(nxt_ref[t] >= 0)
        def _():
            for cp in weight_copies(nxt_ref[t]):
                cp.start()

    @pl.when(t < nu_ref[0])
    def _():
        x = x_ref[...].astype(BF16)
        a = jnp.dot(x, cur_g[...], preferred_element_type=F32)
        u = jnp.dot(x, cur_u[...], preferred_element_type=F32)
        act = (a * jax.nn.sigmoid(a) * u).astype(BF16)
        o_ref[...] = jnp.dot(act, cur_d[...], preferred_element_type=F32).astype(o_ref.dtype)

    @pl.when(t >= nu_ref[0])
    def _():
        o_ref[...] = jnp.zeros_like(o_ref)


def _moe_ffn(tile_expert, n_used, first, nxt, x_sorted, wg, wu, wd, layer):
    r, d = x_sorted.shape
    f = wg.shape[3]
    tm = MOE_ROW_TILE
    any_spec = pl.BlockSpec(memory_space=pl.ANY)
    return pl.pallas_call(
        functools.partial(_moe_kernel, layer=layer),
        grid_spec=pltpu.PrefetchScalarGridSpec(
            num_scalar_prefetch=4,
            grid=(r // tm,),
            in_specs=[pl.BlockSpec((tm, d), lambda t, *_: (t, 0)), any_spec, any_spec, any_spec],
            out_specs=pl.BlockSpec((tm, d), lambda t, *_: (t, 0)),
            scratch_shapes=[pltpu.VMEM((d, f), F32), pltpu.VMEM((d, f), F32), pltpu.VMEM((f, d), F32),
                            pltpu.VMEM((d, f), BF16), pltpu.VMEM((d, f), BF16), pltpu.VMEM((f, d), BF16),
                            pltpu.SemaphoreType.DMA((3,))],
        ),
        out_shape=jax.ShapeDtypeStruct((r, d), F32),
        compiler_params=_params(("arbitrary",)),
        name="moe_ffn",
    )(tile_expert, n_used, first, nxt, x_sorted, wg, wu, wd)


def _combine_kernel(x_ref, a_ref, b_ref, w_ref, g_ref, *rest):
    w = w_ref[...]
    xn = x_ref[...] + g_ref[0] * (w[:, 0:1] * a_ref[...] + w[:, 1:2] * b_ref[...])
    rest[-1 if len(rest) == 1 else -2][...] = xn
    if len(rest) > 1:
        gn_ref, sh_ref, sc_ref, _, h_ref = rest
        h_ref[...] = (_rms(xn, xn.shape[-1]) * gn_ref[...] * (1.0 + sc_ref[0]) + sh_ref[0]).astype(h_ref.dtype)


def _combine(x, ab, w_col, gate, rows_per_group, n_groups, next_norm=None):
    n, d = x.shape
    tm = ROW_TILE
    row = pl.BlockSpec((tm, d), lambda i: (i, 0))
    gs = _group_spec(d, tm, rows_per_group, n_groups)
    in_specs = [row, row, pl.BlockSpec((tm, d), lambda i: (i + n // tm, 0)),
                pl.BlockSpec((tm, LANES), lambda i: (i, 0)), gs]
    args = [x, ab, ab, w_col, gate]
    out_specs, out_shape = row, jax.ShapeDtypeStruct((n, d), F32)
    if next_norm is not None:
        gn, sh, sc = next_norm
        in_specs += [pl.BlockSpec((1, d), lambda i: (0, 0)), gs, gs]
        args += [gn.reshape(1, d), sh, sc]
        out_specs, out_shape = [row, row], [out_shape, jax.ShapeDtypeStruct((n, d), BF16)]
    return pl.pallas_call(
        _combine_kernel,
        grid=(n // tm,),
        in_specs=in_specs,
        out_specs=out_specs,
        out_shape=out_shape,
        compiler_params=_params(("parallel",)),
        name="moe_combine",
    )(*args)


def _moe_layer(x, g, sh, sc, gate, rw_pad, rb_col, wg, wu, wd, layer, rows_per_group, n_groups, next_norm):
    n, d = x.shape
    tm = MOE_ROW_TILE
    h, idx8, w_col, cnt = _norm_mod_route(x, g, sh, sc, rw_pad, rb_col, rows_per_group, n_groups)
    experts = jnp.arange(N_EXPERTS, dtype=jnp.int32)
    padded = (cnt[:, 0] + tm - 1) // tm * tm
    ends = jnp.cumsum(padded)
    starts = ends - padded
    e2, rank2 = idx8[0:2], idx8[2:4]
    dest = rank2 + jnp.sum(jnp.where(e2[..., None] == experts, starts, 0), axis=-1)
    r = (2 * n + N_EXPERTS * (tm - 1)) // tm * tm
    tok = jnp.tile(jnp.arange(n, dtype=jnp.int32), 2)
    src = (jnp.arange(r, dtype=jnp.int32) % n).at[dest.reshape(-1)].set(tok, unique_indices=True,
                                                                         mode="promise_in_bounds")
    tile_start = jnp.arange(r // tm, dtype=jnp.int32) * tm
    tile_expert = jnp.minimum(jnp.sum((ends[None, :] <= tile_start[:, None]).astype(jnp.int32), axis=1),
                              N_EXPERTS - 1)
    n_used = ends[-1:] // tm
    first = jnp.concatenate([jnp.ones((1,), jnp.int32), (tile_expert[1:] != tile_expert[:-1]).astype(jnp.int32)])
    later = (padded[None, :] > 0) & (experts[None, :] > experts[:, None])
    next_of = jnp.min(jnp.where(later, experts[None, :], N_EXPERTS), axis=1)
    next_of = jnp.where(next_of < N_EXPERTS, next_of, -1)
    nxt = jnp.sum(jnp.where(tile_expert[:, None] == experts, next_of, 0), axis=1)
    x_sorted = h.at[src].get(mode="promise_in_bounds")
    y_sorted = _moe_ffn(tile_expert, n_used, first, nxt, x_sorted, wg, wu, wd, layer)
    ab = y_sorted.at[dest.reshape(-1)].get(mode="promise_in_bounds")
    return _combine(x, ab, w_col, gate, rows_per_group, n_groups, next_norm)


def _axial_tables(rows, rot_dim):
    n_freq = rot_dim // 4
    inv_freq = jnp.power(ROPE_BASE, -jnp.arange(n_freq, dtype=F32) / n_freq)
    row = jnp.repeat(jnp.arange(rows, dtype=F32), GRID_W)
    col = jnp.tile(jnp.arange(GRID_W, dtype=F32), rows)
    ang = jnp.concatenate([row[:, None] * inv_freq, col[:, None] * inv_freq], axis=-1)
    return jnp.cos(ang), jnp.sin(ang)


def _flat_tables(c_lat, s_lat, batch, n_ctx_rows):
    c = jnp.concatenate([jnp.tile(c_lat, (batch, 1)), jnp.broadcast_to(c_lat[:1], (n_ctx_rows, LANES))])
    s = jnp.concatenate([jnp.tile(s_lat, (batch, 1)), jnp.zeros((n_ctx_rows, LANES), F32)])
    return c, s


def _spread_pairs(a, axis):
    x1, x2 = jnp.split(a, 2, axis=axis)
    z = jnp.zeros_like(x1)
    return jnp.concatenate([x1, z, x2, z], axis=axis)


def kernel(x, c, ctx, c_ctx, ada_w, ada_b, norm_mix_g, norm_ffn_g, mla_w_in, mla_q_a_g, mla_w_q_b, mla_kv_a_g,
           mla_w_kv_b, mla_q_norm_g, mla_k_norm_g, mla_w_o, diff_w_in, diff_q_norm_g, diff_k_norm_g, diff_lambda,
           diff_subln_g, diff_w_o, ret_w_in, ret_decay_fwd, ret_decay_bwd, ret_norm_g, ret_w_o, router_w,
           router_bias, moe_w_gate, moe_w_up, moe_w_down):
    batch, seq, d = x.shape
    n_ctx = ctx.shape[1]
    depth = ada_w.shape[0]
    n_lat, n_cx = batch * seq, batch * n_ctx
    n_groups = batch + 1
    assert n_groups <= SUBLANES and seq % MM_ROW_TILE == 0 and n_cx % MM_ROW_TILE == 0
    assert seq % RET_CHUNK == 0 and n_ctx % RET_CHUNK == 0 and n_lat % n_ctx == 0 and n_ctx == ROW_TILE

    x_lat, x_ctx = x.reshape(n_lat, d), ctx.reshape(n_cx, d)
    xa = None
    n_all = n_lat + n_cx

    cond = jnp.concatenate([c, c_ctx[None, :], jnp.zeros((SUBLANES - n_groups, d), F32)], axis=0)
    mods = _ada_mods(cond.T, ada_w, ada_b, n_groups)

    rows = seq // GRID_W
    cos64, sin64 = _axial_tables(rows, MLA_ROPE)
    zeros32 = jnp.zeros_like(cos64)
    mla_c, mla_s = _flat_tables(jnp.concatenate([cos64, zeros32, cos64, zeros32], axis=1),
                                jnp.concatenate([-sin64, zeros32, sin64, zeros32], axis=1), batch, n_cx)
    cos128, sin128 = _axial_tables(rows, DIFF_HEAD_DIM)
    diff_c, diff_s = _flat_tables(jnp.concatenate([cos128, cos128], axis=1),
                                  jnp.concatenate([-sin128, sin128], axis=1), batch, n_cx)
    cos256, sin256 = _axial_tables(rows, RET_DK)
    ret_c, ret_s = _flat_tables(cos256, sin256, batch, n_cx)

    rw_pad = jnp.concatenate([router_w, router_w, jnp.zeros((d, LANES - 2 * N_EXPERTS), F32)], axis=1)
    rb_col = router_bias.reshape(N_EXPERTS, 1)

    for i in range(depth):
        kind, j, last = i % N_MIXERS, i // N_MIXERS, i == depth - 1
        m = mods[i].reshape(SUBLANES, 6, 1, d)
        sh_m, sc_m, g_m, sh_f, sc_f, g_f = (m[:, t] for t in range(6))
        if i == 0:
            h = _norm_mod(x_lat, x_ctx, norm_mix_g[i], sh_m, sc_m, seq, n_groups)
        q_rows = n_lat if last else n_all

        if kind == 0:
            w_in = jnp.concatenate([mla_w_in[j][:, :MLA_Q_LORA + MLA_KV_LORA],
                                    _spread_pairs(mla_w_in[j][:, MLA_Q_LORA + MLA_KV_LORA:], 1)], axis=1)
            z = _proj(h, w_in, 0, w_in.shape[1], out_dtype=F32, tn=w_in.shape[1])
            wq = mla_w_q_b[j].reshape(MLA_Q_LORA, MLA_HEADS, MLA_NOPE + MLA_ROPE)
            wq = jnp.concatenate([wq[..., :MLA_NOPE], _spread_pairs(wq[..., MLA_NOPE:], 2)], axis=2)
            wq = wq.reshape(MLA_Q_LORA, MLA_HEADS * MLA_HEAD_PAD).astype(BF16)
            scale = float(MLA_NOPE + MLA_ROPE) ** -0.5 * LOG2E
            qg, kg = mla_q_norm_g[j], mla_k_norm_g[j]
            q = _mla_q(z, mla_q_a_g[j], wq, qg[:MLA_NOPE] * scale,
                       *_fold_rope(mla_c, mla_s, _spread_pairs(qg[MLA_NOPE:], 0), scale), q_rows)
            k, v = _mla_kv(z, mla_kv_a_g[j], mla_w_kv_b[j].astype(BF16), kg[:MLA_NOPE],
                           *_fold_rope(mla_c, mla_s, _spread_pairs(kg[MLA_NOPE:], 0), 1.0))
            attn = functools.partial(_attention, q, k, v, heads=MLA_HEADS, q_width=MLA_HEAD_PAD,
                                     v_width=MLA_V, dv=MLA_V, seq=seq, n_ctx=n_ctx, batch=batch,
                                     q_tile=MLA_Q_TILE)
            w_o = mla_w_o[j]
        elif kind == 1:
            w_in = diff_w_in[j]
            hd = DIFF_HEADS * 2 * DIFF_HEAD_DIM
            scale = float(DIFF_HEAD_DIM) ** -0.5 * LOG2E
            q = _proj(h[:q_rows], w_in, 0, hd, mode="norm_rope",
                      extras=_fold_rope(diff_c[:q_rows], diff_s[:q_rows], diff_q_norm_g[j], scale))
            k = _proj(h, w_in, hd, hd, mode="norm_rope", extras=_fold_rope(diff_c, diff_s, diff_k_norm_g[j], 1.0))
            v = _proj(h, w_in, 2 * hd, hd)
            lam_init = 0.8 - 0.6 * math.exp(-0.3 * i)
            attn = functools.partial(_attention, q, k, v, heads=DIFF_HEADS, q_width=2 * DIFF_HEAD_DIM,
                                     v_width=2 * DIFF_HEAD_DIM, dv=2 * DIFF_HEAD_DIM, seq=seq, n_ctx=n_ctx,
                                     batch=batch, q_tile=ATTN_Q_TILE,
                                     diff=(diff_lambda[j], diff_subln_g[j], lam_init))
            w_o = diff_w_o[j]
        else:
            w_in = ret_w_in[j]
            nq, nv = RET_HEADS * RET_DK, RET_HEADS * RET_DV
            q = _proj(h, w_in, 0, nq, mode="rope256", extras=(ret_c, ret_s))
            k_scale = float(RET_DK) ** -0.5
            k = _proj(h, w_in, nq, nq, mode="rope256", extras=(ret_c * k_scale, ret_s * k_scale))
            v = _proj(h, w_in, 2 * nq, nv)
            gate = _proj(h, w_in, 2 * nq + nv, nv, mode="silu")
            dec = lambda p: jnp.broadcast_to(p.astype(F32)[:, None, None], (RET_HEADS, 1, LANES))
            o_b = _retention(dec(ret_decay_bwd[j]), q, k, v, backward=True, seq=seq, n_ctx=n_ctx, batch=batch)
            y = _retention(dec(ret_decay_fwd[j]), q, k, v, backward=False, seq=seq, n_ctx=n_ctx, batch=batch,
                           finish=(o_b, gate, ret_norm_g[j].reshape(1, -1)))
            attn = None
            w_o = ret_w_o[j]

        y_tail = None
        if attn is not None:
            y = attn(lat_queries=True)
            if not last:
                y_tail = attn(lat_queries=False)
        res_in, res_tail = (x_lat, x_ctx) if xa is None else (xa, None)
        xa = _out_res(y, w_o, res_in, g_m, seq, n_groups, n_lat if last else n_all, y_tail, res_tail)
        next_norm = None
        if not last:
            m_next = mods[i + 1].reshape(SUBLANES, 6, 1, d)
            next_norm = (norm_mix_g[i + 1], m_next[:, 0], m_next[:, 1])
        res = _moe_layer(xa, norm_ffn_g[i], sh_f, sc_f, g_f, rw_pad, rb_col, moe_w_gate, moe_w_up, moe_w_down, i,
                         seq, n_groups, next_norm)
        xa, h = (res, None) if last else res
    return xa[:n_lat].reshape(batch, seq, d)
```

```python
import functools
import math

import jax
import jax.numpy as jnp
from jax import lax
from jax.experimental import pallas as pl
from jax.experimental.pallas import tpu as pltpu

F32 = jnp.float32
BF16 = jnp.bfloat16

GRID_W = 64
ROPE_BASE = 10000.0
NORM_EPS = 1e-6
N_MIXERS = 3

MLA_HEADS = 16
MLA_Q_LORA = 512
MLA_KV_LORA = 512
MLA_NOPE = 128
MLA_ROPE = 64
MLA_V = 128
MLA_HEAD_PAD = 256

DIFF_HEADS = 8
DIFF_HEAD_DIM = 128

RET_HEADS = 8
RET_DK = 256
RET_DV = 512
RET_CHUNK = 256

N_EXPERTS = 16
N_GROUPS = 4
EXPERTS_PER_GROUP = 4
D_EXPERT = 1024

LANES = 128
SUBLANES = 8
VMEM_LIMIT = 56 * 1024 * 1024

ROW_TILE = 256
MM_ROW_TILE = 512
MM_COL_TILE = 2048
MM_COL_TILE_MAX_K = 2048
MLA_COL_TILE = MLA_HEADS * MLA_HEAD_PAD
RET_HEAD_BLOCK = 4
WEIGHT_CAST_CHUNKS = 8
MOE_ROW_TILE = 256
KEY_CHUNK = 512
ATTN_Q_TILE = 1024
MLA_Q_TILE = 1024
LOG2E = 1.4426950408889634
ADA_COL_TILE = 1024


def _params(sem):
    return pltpu.CompilerParams(dimension_semantics=sem, vmem_limit_bytes=VMEM_LIMIT)


def _rms(x, width):
    return x * lax.rsqrt(jnp.sum(x * x, axis=-1, keepdims=True) / width + NORM_EPS)


def _ada_kernel(ct_ref, w_ref, b_ref, o_ref, sb_ref, *, n_cond):
    d = ct_ref.shape[0]
    tn = o_ref.shape[-1]

    @pl.when((pl.program_id(0) == 0) & (pl.program_id(1) == 0))
    def _():
        ct = ct_ref[...]
        s = ct * jax.nn.sigmoid(ct)
        for r in range(n_cond):
            sb_ref[r] = jnp.broadcast_to(s[:, r:r + 1], (d, LANES))

    def body(kc, accs):
        k0 = pl.multiple_of(kc * SUBLANES, SUBLANES)
        w8 = w_ref[0, pl.ds(k0, SUBLANES), :]
        out = []
        for r in range(n_cond):
            s8 = sb_ref[r, pl.ds(k0, SUBLANES), :]
            out.append(accs[r] + w8 * jnp.concatenate([s8] * (tn // LANES), axis=1))
        return tuple(out)

    accs = lax.fori_loop(0, d // SUBLANES, body,
                         tuple(jnp.zeros((SUBLANES, tn), F32) for _ in range(n_cond)), unroll=4)
    rows = [jnp.sum(a, axis=0, keepdims=True) + b_ref[0] for a in accs]
    rows.append(jnp.zeros((SUBLANES - n_cond, tn), F32))
    o_ref[0] = jnp.concatenate(rows, axis=0)


def _ada_mods(cond_t, ada_w, ada_b, n_cond):
    depth, d, n6 = ada_w.shape
    tn = ADA_COL_TILE
    return pl.pallas_call(
        functools.partial(_ada_kernel, n_cond=n_cond),
        grid=(depth, n6 // tn),
        in_specs=[pl.BlockSpec((d, SUBLANES), lambda l, j: (0, 0)),
                  pl.BlockSpec((1, d, tn), lambda l, j: (l, 0, j)),
                  pl.BlockSpec((1, 1, tn), lambda l, j: (l, 0, j))],
        out_specs=pl.BlockSpec((1, SUBLANES, tn), lambda l, j: (l, 0, j)),
        out_shape=jax.ShapeDtypeStruct((depth, SUBLANES, n6), F32),
        scratch_shapes=[pltpu.VMEM((n_cond, d, LANES), F32)],
        compiler_params=_params(("arbitrary", "arbitrary")),
        name="ada_mods",
    )(cond_t, ada_w, ada_b.reshape(depth, 1, n6))


def _group_spec(d, tm, rows_per_group, n_groups):
    return pl.BlockSpec((1, 1, d), lambda i, *_: (jnp.minimum(i * tm // rows_per_group, n_groups - 1), 0, 0))


def _split_rows_specs(head_rows, tm, width, row_axis):
    head_tiles = head_rows // tm
    pick = lambda idx: idx[row_axis]
    head = pl.BlockSpec((tm, width), lambda *idx: (jnp.minimum(pick(idx), head_tiles - 1), 0))
    tail = pl.BlockSpec((tm, width), lambda *idx: (jnp.maximum(pick(idx) - head_tiles, 0), 0))
    return head, tail


def _norm_mod_kernel(x_ref, x2_ref, g_ref, sh_ref, sc_ref, o_ref, *, head_tiles):
    x = jnp.where(pl.program_id(0) < head_tiles, x_ref[...], x2_ref[...])
    h = _rms(x, x.shape[-1]) * g_ref[...] * (1.0 + sc_ref[0]) + sh_ref[0]
    o_ref[...] = h.astype(o_ref.dtype)


def _norm_mod(x, x2, g, sh, sc, rows_per_group, n_groups):
    n, d = x.shape[0] + x2.shape[0], x.shape[1]
    tm = ROW_TILE
    gs = _group_spec(d, tm, rows_per_group, n_groups)
    return pl.pallas_call(
        functools.partial(_norm_mod_kernel, head_tiles=x.shape[0] // tm),
        grid=(n // tm,),
        in_specs=[*_split_rows_specs(x.shape[0], tm, d, 0), pl.BlockSpec((1, d), lambda i: (0, 0)), gs, gs],
        out_specs=pl.BlockSpec((tm, d), lambda i: (i, 0)),
        out_shape=jax.ShapeDtypeStruct((n, d), BF16),
        compiler_params=_params(("parallel",)),
        name="norm_mod",
    )(x, x2, g.reshape(1, d), sh, sc)


def _route_kernel(x_ref, g_ref, sh_ref, sc_ref, rw_ref, rb_ref, h_ref, idx_ref, wt_ref, cnt_ref, carry_ref,
                  rwp_ref):
    @pl.when(pl.program_id(0) == 0)
    def _():
        carry_ref[...] = jnp.zeros_like(carry_ref)
        w = rw_ref[...]
        hi = w.astype(BF16)
        lo = (w - hi.astype(F32)).astype(BF16)
        rwp_ref[...] = jnp.where(lax.broadcasted_iota(jnp.int32, w.shape, 1) < N_EXPERTS, hi, lo)

    x = x_ref[...]
    tm = x.shape[0]
    h = _rms(x, x.shape[-1]) * g_ref[...] * (1.0 + sc_ref[0]) + sh_ref[0]
    h_ref[...] = h.astype(h_ref.dtype)
    h_hi = h.astype(BF16)
    h_lo = (h - h_hi.astype(F32)).astype(BF16)
    parts = (jnp.dot(h_hi, rwp_ref[...], preferred_element_type=F32)
             + jnp.dot(h_lo, rwp_ref[...], preferred_element_type=F32)).T
    logits = parts[:N_EXPERTS] + parts[N_EXPERTS:2 * N_EXPERTS]
    scores = jax.nn.sigmoid(logits)
    sel = scores + rb_ref[...]
    sel_r = [sel[e:e + 1, :] for e in range(N_EXPERTS)]
    sc_r = [scores[e:e + 1, :] for e in range(N_EXPERTS)]

    gscore = []
    for g in range(N_GROUPS):
        a, b, c, d = sel_r[4 * g:4 * g + 4]
        hi1, lo1, hi2, lo2 = jnp.maximum(a, b), jnp.minimum(a, b), jnp.maximum(c, d), jnp.minimum(c, d)
        gscore.append(jnp.maximum(hi1, hi2) + jnp.maximum(jnp.minimum(hi1, hi2), jnp.maximum(lo1, lo2)))
    grp = jnp.zeros((1, tm), jnp.int32)
    best = gscore[0]
    for g in range(1, N_GROUPS):
        better = gscore[g] > best
        grp = jnp.where(better, g, grp)
        best = jnp.where(better, gscore[g], best)

    def pick(rows, l):
        out = rows[l]
        for g in range(1, N_GROUPS):
            out = jnp.where(grp == g, rows[4 * g + l], out)
        return out

    v = [pick(sel_r, l) for l in range(EXPERTS_PER_GROUP)]
    s = [pick(sc_r, l) for l in range(EXPERTS_PER_GROUP)]

    def first_max(vals):
        m = jnp.maximum(jnp.maximum(vals[0], vals[1]), jnp.maximum(vals[2], vals[3]))
        l = jnp.where(vals[0] == m, 0, jnp.where(vals[1] == m, 1, jnp.where(vals[2] == m, 2, 3)))
        return l

    def at(vals, l):
        return jnp.where(l == 0, vals[0], jnp.where(l == 1, vals[1], jnp.where(l == 2, vals[2], vals[3])))

    l1 = first_max(v)
    l2 = first_max([jnp.where(l1 == l, -jnp.inf, v[l]) for l in range(EXPERTS_PER_GROUP)])
    s1, s2 = at(s, l1), at(s, l2)
    tot = s1 + s2
    e1, e2 = grp * EXPERTS_PER_GROUP + l1, grp * EXPERTS_PER_GROUP + l2

    eids = lax.broadcasted_iota(jnp.int32, (N_EXPERTS, tm), 0)
    oh1, oh2 = (eids == e1).astype(F32), (eids == e2).astype(F32)
    before = (lax.broadcasted_iota(jnp.int32, (tm, tm), 0) < lax.broadcasted_iota(jnp.int32, (tm, tm), 1)).astype(BF16)
    carry = carry_ref[:, :1]
    tot1 = jnp.sum(oh1, axis=1, keepdims=True)
    pre1 = carry + jnp.dot(oh1.astype(BF16), before, preferred_element_type=F32)
    pre2 = carry + tot1 + jnp.dot(oh2.astype(BF16), before, preferred_element_type=F32)
    r1 = jnp.sum(oh1 * pre1, axis=0, keepdims=True).astype(jnp.int32)
    r2 = jnp.sum(oh2 * pre2, axis=0, keepdims=True).astype(jnp.int32)
    carry_new = carry + tot1 + jnp.sum(oh2, axis=1, keepdims=True)
    carry_ref[...] = jnp.broadcast_to(carry_new, carry_ref.shape)
    cnt_ref[...] = jnp.broadcast_to(carry_new, cnt_ref.shape).astype(jnp.int32)

    idx_ref[...] = jnp.concatenate([e1, e2, r1, r2, jnp.zeros((SUBLANES - 4, tm), jnp.int32)], axis=0)
    wt_ref[...] = jnp.concatenate([s1 / tot, s2 / tot, jnp.zeros((LANES - 2, tm), F32)], axis=0).T


def _norm_mod_route(x, g, sh, sc, rw_pad, rb_col, rows_per_group, n_groups):
    n, d = x.shape
    tm = ROW_TILE
    gs = _group_spec(d, tm, rows_per_group, n_groups)
    return pl.pallas_call(
        _route_kernel,
        grid=(n // tm,),
        in_specs=[pl.BlockSpec((tm, d), lambda i: (i, 0)), pl.BlockSpec((1, d), lambda i: (0, 0)), gs, gs,
                  pl.BlockSpec((d, LANES), lambda i: (0, 0)), pl.BlockSpec((N_EXPERTS, 1), lambda i: (0, 0))],
        out_specs=[pl.BlockSpec((tm, d), lambda i: (i, 0)),
                   pl.BlockSpec((SUBLANES, tm), lambda i: (0, i)),
                   pl.BlockSpec((tm, LANES), lambda i: (i, 0)),
                   pl.BlockSpec((N_EXPERTS, LANES), lambda i: (0, 0))],
        out_shape=[jax.ShapeDtypeStruct((n, d), F32),
                   jax.ShapeDtypeStruct((SUBLANES, n), jnp.int32),
                   jax.ShapeDtypeStruct((n, LANES), F32),
                   jax.ShapeDtypeStruct((N_EXPERTS, LANES), jnp.int32)],
        scratch_shapes=[pltpu.VMEM((N_EXPERTS, LANES), F32), pltpu.VMEM((d, LANES), BF16)],
        compiler_params=_params(("arbitrary",)),
        name="norm_mod_route",
    )(x, g.reshape(1, d), sh, sc, rw_pad, rb_col)


def _half_swap_rope(x, c, s):
    return x * c + pltpu.roll(x, LANES // 2, 1) * s


def _row_sums(sq):
    return jnp.dot(sq.astype(BF16), jnp.ones((sq.shape[1], LANES), BF16), preferred_element_type=F32)


def _fold_rope(c_t, s_t, g, scale):
    return c_t * (g * scale)[None, :], s_t * (jnp.roll(g, LANES // 2) * scale)[None, :]


def _proj_kernel(*refs, mode):
    x_ref, w_ref = refs[0], refs[1]
    o_ref, wb_ref = refs[-2], refs[-1]

    @pl.when(pl.program_id(1) == 0)
    def _():
        wb_ref[...] = w_ref[...].astype(BF16)

    acc = jnp.dot(x_ref[...], wb_ref[...], preferred_element_type=F32)
    tn = acc.shape[1]
    if mode == "plain":
        o_ref[...] = acc.astype(o_ref.dtype)
    elif mode == "silu":
        o_ref[...] = (acc * jax.nn.sigmoid(acc)).astype(o_ref.dtype)
    elif mode == "norm_rope":
        gc, gs = refs[2][...], refs[3][...]
        for j in range(tn // LANES):
            blk = acc[:, j * LANES:(j + 1) * LANES]
            blk = blk * lax.rsqrt(_row_sums(blk * blk) / LANES + NORM_EPS)
            o_ref[:, j * LANES:(j + 1) * LANES] = _half_swap_rope(blk, gc, gs).astype(o_ref.dtype)
    elif mode == "rope256":
        c, s = refs[2][...], refs[3][...]
        for j in range(tn // (2 * LANES)):
            x1 = acc[:, (2 * j) * LANES:(2 * j + 1) * LANES]
            x2 = acc[:, (2 * j + 1) * LANES:(2 * j + 2) * LANES]
            o_ref[:, (2 * j) * LANES:(2 * j + 1) * LANES] = (x1 * c - x2 * s).astype(o_ref.dtype)
            o_ref[:, (2 * j + 1) * LANES:(2 * j + 2) * LANES] = (x2 * c + x1 * s).astype(o_ref.dtype)
    else:
        raise ValueError(mode)


def _col_tile(k, f32_tiles):
    tn = MM_COL_TILE if k <= MM_COL_TILE_MAX_K else MM_COL_TILE // 2
    return tn // 2 if f32_tiles else tn


def _proj(x, w, col0, n_cols, *, mode="plain", extras=(), out_dtype=BF16, tn=None):
    n, k = x.shape
    tm = MM_ROW_TILE
    tn = min(_col_tile(k, False) if tn is None else tn, n_cols)
    j0 = col0 // tn
    extra_specs = []
    for e in extras:
        if e.shape[0] == 1:
            extra_specs.append(pl.BlockSpec(e.shape, lambda j, i: (0, 0)))
        else:
            extra_specs.append(pl.BlockSpec((tm, e.shape[1]), lambda j, i: (i, 0)))
    return pl.pallas_call(
        functools.partial(_proj_kernel, mode=mode),
        grid=(n_cols // tn, n // tm),
        in_specs=[pl.BlockSpec((tm, k), lambda j, i: (i, 0)),
                  pl.BlockSpec((k, tn), lambda j, i: (0, j0 + j))] + extra_specs,
        out_specs=pl.BlockSpec((tm, tn), lambda j, i: (i, j)),
        out_shape=jax.ShapeDtypeStruct((n, n_cols), out_dtype),
        scratch_shapes=[pltpu.VMEM((k, tn), BF16)],
        compiler_params=_params(("arbitrary", "arbitrary")),
        name="proj_" + mode,
    )(x, w, *extras)


def _out_res_kernel(*refs, y_head_tiles, x_head_tiles):
    refs = list(refs)
    i = pl.program_id(1)
    y = refs.pop(0)[...] if y_head_tiles is None else jnp.where(i < y_head_tiles, refs.pop(0)[...], refs.pop(0)[...])
    w_ref = refs.pop(0)
    x = refs.pop(0)[...] if x_head_tiles is None else jnp.where(i < x_head_tiles, refs.pop(0)[...], refs.pop(0)[...])
    g_ref, o_ref, wb_ref = refs

    @pl.when(i == 0)
    def _():
        wb_ref[...] = w_ref[...].astype(BF16)

    o_ref[...] = x + g_ref[0] * jnp.dot(y, wb_ref[...], preferred_element_type=F32)


def _out_res(y, w, x, gate, rows_per_group, n_groups, rows, y_tail=None, x_tail=None):
    n, k = rows, y.shape[1]
    d = w.shape[1]
    tm = MM_ROW_TILE
    tn = _col_tile(k, True)
    y_specs = [pl.BlockSpec((tm, k), lambda j, i: (i, 0))] if y_tail is None else list(
        _split_rows_specs(y.shape[0], tm, k, 1))
    if x_tail is None:
        x_specs = [pl.BlockSpec((tm, tn), lambda j, i: (i, j))]
    else:
        xt = x.shape[0] // tm
        x_specs = [pl.BlockSpec((tm, tn), lambda j, i: (jnp.minimum(i, xt - 1), j)),
                   pl.BlockSpec((tm, tn), lambda j, i: (jnp.maximum(i - xt, 0), j))]
    args = [y] + ([] if y_tail is None else [y_tail]) + [w, x] + ([] if x_tail is None else [x_tail]) + [gate]
    return pl.pallas_call(
        functools.partial(_out_res_kernel, y_head_tiles=None if y_tail is None else y.shape[0] // tm,
                          x_head_tiles=None if x_tail is None else x.shape[0] // tm),
        grid=(d // tn, n // tm),
        in_specs=y_specs + [pl.BlockSpec((k, tn), lambda j, i: (0, j))] + x_specs
        + [pl.BlockSpec((1, 1, tn), lambda j, i: (jnp.minimum(i * tm // rows_per_group, n_groups - 1), 0, j))],
        out_specs=pl.BlockSpec((tm, tn), lambda j, i: (i, j)),
        out_shape=jax.ShapeDtypeStruct((n, d), F32),
        scratch_shapes=[pltpu.VMEM((k, tn), BF16)],
        compiler_params=_params(("arbitrary", "arbitrary")),
        name="out_res",
    )(*args)


def _mla_q_kernel(z_ref, ga_ref, w_ref, g_ref, gc_ref, gs_ref, o_ref):
    cq = z_ref[...]
    cn = (_rms(cq, cq.shape[-1]) * ga_ref[...]).astype(BF16)
    acc = jnp.dot(cn, w_ref[...], preferred_element_type=F32)
    g, gc, gs = g_ref[...], gc_ref[...], gs_ref[...]
    width = float(MLA_NOPE + MLA_ROPE)
    for j in range(acc.shape[1] // MLA_HEAD_PAD):
        qh = acc[:, j * MLA_HEAD_PAD:(j + 1) * MLA_HEAD_PAD]
        f = lax.rsqrt(_row_sums(qh * qh) / width + NORM_EPS)
        o_ref[:, j * MLA_HEAD_PAD:j * MLA_HEAD_PAD + LANES] = (qh[:, :LANES] * f * g).astype(o_ref.dtype)
        o_ref[:, j * MLA_HEAD_PAD + LANES:(j + 1) * MLA_HEAD_PAD] = _half_swap_rope(
            qh[:, LANES:] * f, gc, gs).astype(o_ref.dtype)


def _mla_q(z, q_a_g, w_q_b_pad, g_nope, gc, gs, rows, tn=MLA_COL_TILE):
    n = rows
    tm = MM_ROW_TILE
    n_out = w_q_b_pad.shape[1]
    return pl.pallas_call(
        _mla_q_kernel,
        grid=(n // tm, n_out // tn),
        in_specs=[pl.BlockSpec((tm, MLA_Q_LORA), lambda i, j: (i, 0)),
                  pl.BlockSpec((1, MLA_Q_LORA), lambda i, j: (0, 0)),
                  pl.BlockSpec((MLA_Q_LORA, tn), lambda i, j: (0, j)),
                  pl.BlockSpec((1, LANES), lambda i, j: (0, 0)),
                  pl.BlockSpec((tm, LANES), lambda i, j: (i, 0)),
                  pl.BlockSpec((tm, LANES), lambda i, j: (i, 0))],
        out_specs=pl.BlockSpec((tm, tn), lambda i, j: (i, j)),
        out_shape=jax.ShapeDtypeStruct((n, n_out), BF16),
        compiler_params=_params(("parallel", "arbitrary")),
        name="mla_q",
    )(z, q_a_g.reshape(1, -1), w_q_b_pad, g_nope.reshape(1, LANES), gc, gs)


def _mla_kv_kernel(z_ref, kr_ref, ga_ref, w_ref, g_ref, gc_ref, gs_ref, k_ref, v_ref):
    ckv = z_ref[...]
    cn = (_rms(ckv, ckv.shape[-1]) * ga_ref[...]).astype(BF16)
    acc = jnp.dot(cn, w_ref[...], preferred_element_type=F32)
    kr = kr_ref[...]
    kr_ss = _row_sums(kr * kr)
    g, gc, gs = g_ref[...], gc_ref[...], gs_ref[...]
    width = float(MLA_NOPE + MLA_ROPE)
    for j in range(acc.shape[1] // MLA_HEAD_PAD):
        kn = acc[:, j * MLA_HEAD_PAD:j * MLA_HEAD_PAD + MLA_NOPE]
        vv = acc[:, j * MLA_HEAD_PAD + MLA_NOPE:(j + 1) * MLA_HEAD_PAD]
        f = lax.rsqrt((_row_sums(kn * kn) + kr_ss) / width + NORM_EPS)
        k_ref[:, j * MLA_HEAD_PAD:j * MLA_HEAD_PAD + LANES] = (kn * f * g).astype(k_ref.dtype)
        k_ref[:, j * MLA_HEAD_PAD + LANES:(j + 1) * MLA_HEAD_PAD] = _half_swap_rope(
            kr * f, gc, gs).astype(k_ref.dtype)
        v_ref[:, j * MLA_V:(j + 1) * MLA_V] = vv.astype(v_ref.dtype)


def _mla_kv(z, kv_a_g, w_kv_b, g_nope, gc, gs, tn=MLA_COL_TILE):
    n = z.shape[0]
    tm = MM_ROW_TILE
    n_out = w_kv_b.shape[1]
    kr_block = (MLA_Q_LORA + MLA_KV_LORA) // LANES
    return pl.pallas_call(
        _mla_kv_kernel,
        grid=(n // tm, n_out // tn),
        in_specs=[pl.BlockSpec((tm, MLA_KV_LORA), lambda i, j: (i, 1)),
                  pl.BlockSpec((tm, LANES), lambda i, j: (i, kr_block)),
                  pl.BlockSpec((1, MLA_KV_LORA), lambda i, j: (0, 0)),
                  pl.BlockSpec((MLA_KV_LORA, tn), lambda i, j: (0, j)),
                  pl.BlockSpec((1, LANES), lambda i, j: (0, 0)),
                  pl.BlockSpec((tm, LANES), lambda i, j: (i, 0)),
                  pl.BlockSpec((tm, LANES), lambda i, j: (i, 0))],
        out_specs=[pl.BlockSpec((tm, tn), lambda i, j: (i, j)),
                   pl.BlockSpec((tm, tn // 2), lambda i, j: (i, j))],
        out_shape=[jax.ShapeDtypeStruct((n, n_out), BF16),
                   jax.ShapeDtypeStruct((n, n_out // 2), BF16)],
        compiler_params=_params(("parallel", "arbitrary")),
        name="mla_kv",
    )(z, z, kv_a_g.reshape(1, -1), w_kv_b, g_nope.reshape(1, LANES), gc, gs)


def _softmax_pv(q, k_refs, v_refs, c0, dq, *, den_from_v):
    chunks = []
    for k_ref, v_ref in zip(k_refs, v_refs):
        step = min(k_ref.shape[0], KEY_CHUNK)
        chunks += [(k_ref, v_ref, r0, step) for r0 in range(0, k_ref.shape[0], step)]
    m = acc = den = None
    for k_ref, v_ref, r0, step in chunks:
        s = lax.dot_general(q, k_ref[r0:r0 + step, c0:c0 + dq], (((1,), (1,)), ((), ())),
                            preferred_element_type=F32)
        mc = jnp.max(s, axis=-1, keepdims=True)
        m_new = mc if m is None else jnp.maximum(m, mc)
        p = jnp.exp2((s - m_new).astype(BF16)) if den_from_v else jnp.exp2(s - m_new)
        vblk = v_ref[r0:r0 + step, :]
        if den_from_v:
            vblk = jnp.concatenate([vblk, jnp.ones((step, LANES), BF16)], axis=1)
        pv = jnp.dot(p.astype(BF16), vblk, preferred_element_type=F32)
        if m is None:
            acc = pv
            if not den_from_v:
                den = jnp.sum(p, axis=-1, keepdims=True)
        else:
            alpha = jnp.exp2(m - m_new)
            acc = acc * alpha + pv
            if not den_from_v:
                den = den * alpha + jnp.sum(p, axis=-1, keepdims=True)
        m = m_new
    if den_from_v:
        dv = acc.shape[1] - LANES
        return acc[:, :dv] / acc[:, dv:dv + 1]
    return acc / den


def _attn_kernel(*refs, n_seg, dq):
    q_ref = refs[0]
    k_refs = refs[1:1 + n_seg]
    v_refs = refs[1 + n_seg:1 + 2 * n_seg]
    o_ref = refs[-1]
    o_ref[...] = _softmax_pv(q_ref[...], k_refs, v_refs, 0, dq, den_from_v=True).astype(o_ref.dtype)


def _diff_attn_kernel(*refs, n_seg, dq, lam_init):
    q_ref = refs[0]
    k_refs = refs[1:1 + n_seg]
    v_refs = refs[1 + n_seg:1 + 2 * n_seg]
    lam_ref, g_ref, o_ref = refs[-3], refs[-2], refs[-1]
    lf = lam_ref[...]
    lam = (jnp.exp(jnp.sum(lf[0:1] * lf[1:2], axis=-1, keepdims=True))
           - jnp.exp(jnp.sum(lf[2:3] * lf[3:4], axis=-1, keepdims=True)) + lam_init)
    q = q_ref[...]
    o1 = _softmax_pv(q[:, :dq], k_refs, v_refs, 0, dq, den_from_v=False)
    o2 = _softmax_pv(q[:, dq:], k_refs, v_refs, dq, dq, den_from_v=False)
    o = o1 - lam * o2
    o_ref[...] = (_rms(o, o.shape[-1]) * g_ref[...] * (1.0 - lam_init)).astype(o_ref.dtype)


def _attention(q, k, v, *, heads, q_width, v_width, dv, seq, n_ctx, batch, lat_queries, q_tile, diff=None):
    lat_rows = batch * seq
    ctx_blk0 = lat_rows // n_ctx
    if lat_queries:
        tq = min(q_tile, seq)
        nq = seq // tq
        q_map = lambda b, h, i: (b * nq + i, h)
        segs = [(seq, lambda b, h, i: (b, h)), (n_ctx, lambda b, h, i: (ctx_blk0 + b, h))]
        out_rows = lat_rows
    else:
        tq = n_ctx
        nq = 1
        q_map = lambda b, h, i: (ctx_blk0 + b, h)
        segs = [(n_ctx, lambda b, h, i: (ctx_blk0 + b, h))]
        out_rows = batch * n_ctx
    n_seg = len(segs)
    in_specs = [pl.BlockSpec((tq, q_width), q_map)]
    in_specs += [pl.BlockSpec((rows, q_width), m) for rows, m in segs]
    in_specs += [pl.BlockSpec((rows, v_width), m) for rows, m in segs]
    args = [q] + [k] * n_seg + [v] * n_seg
    if diff is None:
        kern = functools.partial(_attn_kernel, n_seg=n_seg, dq=q_width)
    else:
        lam, subln_g, lam_init = diff
        kern = functools.partial(_diff_attn_kernel, n_seg=n_seg, dq=q_width // 2, lam_init=lam_init)
        in_specs += [pl.BlockSpec(lam.shape, lambda b, h, i: (0, 0)), pl.BlockSpec((1, dv), lambda b, h, i: (0, 0))]
        args += [lam, subln_g.reshape(1, dv)]
    return pl.pallas_call(
        kern,
        grid=(batch, heads, nq),
        in_specs=in_specs,
        out_specs=pl.BlockSpec((tq, dv), lambda b, h, i: (b * nq + i, h)),
        out_shape=jax.ShapeDtypeStruct((out_rows, heads * dv), BF16),
        compiler_params=_params(("parallel", "parallel", "arbitrary")),
        name="attention" if diff is None else "diff_attention",
    )(*args)


def _ret_kernel(dec_ref, q_ref, k_ref, v_ref, *rest, backward, finish):
    if finish:
        other_ref, gate_ref, ng_ref, o_ref, state_ref, intra_ref, qd_ref, kd_ref = rest
    else:
        o_ref, state_ref, intra_ref, qd_ref, kd_ref = rest
    ch = q_ref.shape[0]
    heads = state_ref.shape[0]

    @pl.when(pl.program_id(2) == 0)
    def _():
        state_ref[...] = jnp.zeros_like(state_ref)
        ii = lax.broadcasted_iota(jnp.int32, (ch, ch), 0).astype(F32)
        jj = lax.broadcasted_iota(jnp.int32, (ch, ch), 1).astype(F32)
        pos = lax.broadcasted_iota(jnp.int32, (ch, LANES), 0).astype(F32)
        if backward:
            dist, valid = jnp.maximum(jj - ii, 0.0), jj > ii
            q_pow, k_pow = ch - pos, pos
        else:
            dist, valid = jnp.maximum(ii - jj, 0.0), ii >= jj
            q_pow, k_pow = pos + 1.0, ch - 1.0 - pos
        for h in range(heads):
            lg = jax.nn.log_sigmoid(dec_ref[h])[:, :1]
            intra_ref[h] = jnp.where(valid, jnp.exp(lg * dist), 0.0)
            qd_ref[h] = jnp.exp(lg * q_pow)
            kd_ref[h] = jnp.exp(lg * k_pow)

    wide = lambda t: jnp.concatenate([t] * (RET_DK // LANES), axis=1)
    for h in range(heads):
        lg = jax.nn.log_sigmoid(dec_ref[h])[:, :1]
        intra = intra_ref[h]
        q = q_ref[:, h * RET_DK:(h + 1) * RET_DK]
        k = k_ref[:, h * RET_DK:(h + 1) * RET_DK]
        v = v_ref[:, h * RET_DV:(h + 1) * RET_DV]
        scores = lax.dot_general(q, k, (((1,), (1,)), ((), ())), preferred_element_type=F32) * intra
        state = state_ref[h]
        o = jnp.dot(scores.astype(BF16), v, preferred_element_type=F32)
        o += jnp.dot((q.astype(F32) * wide(qd_ref[h])).astype(BF16), state.astype(BF16),
                     preferred_element_type=F32)
        cols = slice(h * RET_DV, (h + 1) * RET_DV)
        if finish:
            y = o + other_ref[:, cols].astype(F32)
            o = gate_ref[:, cols].astype(F32) * (_rms(y, RET_DV) * ng_ref[:, cols])
        o_ref[:, cols] = o.astype(o_ref.dtype)
        kd_t = (k.astype(F32) * wide(kd_ref[h])).T.astype(BF16)
        state_ref[h] = state * jnp.exp(lg * ch) + jnp.dot(kd_t, v, preferred_element_type=F32)


def _retention(dec, q, k, v, *, backward, seq, n_ctx, batch, finish=None):
    ch = RET_CHUNK
    n_c, n_s = n_ctx // ch, seq // ch
    ctx_blk0 = batch * seq // ch

    def row_block(b, t):
        if backward:
            return jnp.where(t < n_c, ctx_blk0 + b * n_c + (n_c - 1 - t), b * n_s + (n_s - 1 - (t - n_c)))
        return jnp.where(t < n_c, ctx_blk0 + b * n_c + t, b * n_s + (t - n_c))

    hb = RET_HEAD_BLOCK
    spec = lambda width: pl.BlockSpec((ch, hb * width), lambda b, h, t: (row_block(b, t), h))
    extra_specs, extra = [], ()
    if finish is not None:
        extra_specs = [spec(RET_DV), spec(RET_DV), pl.BlockSpec((1, hb * RET_DV), lambda b, h, t: (0, h))]
        extra = tuple(finish)
    return pl.pallas_call(
        functools.partial(_ret_kernel, backward=backward, finish=finish is not None),
        grid=(batch, RET_HEADS // hb, n_c + n_s),
        in_specs=[pl.BlockSpec((hb, 1, LANES), lambda b, h, t: (h, 0, 0)), spec(RET_DK), spec(RET_DK), spec(RET_DV)]
        + extra_specs,
        out_specs=spec(RET_DV),
        out_shape=jax.ShapeDtypeStruct((q.shape[0], RET_HEADS * RET_DV), BF16),
        scratch_shapes=[pltpu.VMEM((hb, RET_DK, RET_DV), F32), pltpu.VMEM((hb, ch, ch), F32),
                        pltpu.VMEM((hb, ch, LANES), F32), pltpu.VMEM((hb, ch, LANES), F32)],
        compiler_params=_params(("parallel", "parallel", "arbitrary")),
        name="retention_bwd" if backward else "retention_fwd",
    )(dec, q, k, v, *extra)


def _moe_kernel(te_ref, nu_ref, first_ref, nxt_ref, x_ref, wg_hbm, wu_hbm, wd_hbm, o_ref,
                stage_g, stage_u, stage_d, cur_g, cur_u, cur_d, sem, *, layer):
    t = pl.program_id(0)

    def weight_copies(e):
        return (pltpu.make_async_copy(wg_hbm.at[layer, e], stage_g, sem.at[0]),
                pltpu.make_async_copy(wu_hbm.at[layer, e], stage_u, sem.at[1]),
                pltpu.make_async_copy(wd_hbm.at[layer, e], stage_d, sem.at[2]))

    @pl.when(t == 0)
    def _():
        for cp in weight_copies(te_ref[0]):
            cp.start()

    @pl.when((t < nu_ref[0]) & (first_ref[t] == 1))
    def _():
        for cp in weight_copies(te_ref[t]):
            cp.wait()
        for stage, cur in ((stage_g, cur_g), (stage_u, cur_u), (stage_d, cur_d)):
            rows = stage.shape[0] // WEIGHT_CAST_CHUNKS

            def cast(c, carry, stage=stage, cur=cur, rows=rows):
                r0 = pl.multiple_of(c * rows, rows)
                cur[pl.ds(r0, rows), :] = stage[pl.ds(r0, rows), :].astype(BF16)
                return carry

            lax.fori_loop(0, WEIGHT_CAST_CHUNKS, cast, 0)

        @pl.when(nxt_ref[t] >= 0)
        def _():
            for cp in weight_copies(nxt_ref[t]):
                cp.start()

    @pl.when(t < nu_ref[0])
    def _():
        x = x_ref[...].astype(BF16)
        a = jnp.dot(x, cur_g[...], preferred_element_type=F32)
        u = jnp.dot(x, cur_u[...], preferred_element_type=F32)
        act = (a * jax.nn.sigmoid(a) * u).astype(BF16)
        o_ref[...] = jnp.dot(act, cur_d[...], preferred_element_type=F32).astype(o_ref.dtype)

    @pl.when(t >= nu_ref[0])
    def _():
        o_ref[...] = jnp.zeros_like(o_ref)


def _moe_ffn(tile_expert, n_used, first, nxt, x_sorted, wg, wu, wd, layer):
    r, d = x_sorted.shape
    f = wg.shape[3]
    tm = MOE_ROW_TILE
    any_spec = pl.BlockSpec(memory_space=pl.ANY)
    return pl.pallas_call(
        functools.partial(_moe_kernel, layer=layer),
        grid_spec=pltpu.PrefetchScalarGridSpec(
            num_scalar_prefetch=4,
            grid=(r // tm,),
            in_specs=[pl.BlockSpec((tm, d), lambda t, *_: (t, 0)), any_spec, any_spec, any_spec],
            out_specs=pl.BlockSpec((tm, d), lambda t, *_: (t, 0)),
            scratch_shapes=[pltpu.VMEM((d, f), F32), pltpu.VMEM((d, f), F32), pltpu.VMEM((f, d), F32),
                            pltpu.VMEM((d, f), BF16), pltpu.VMEM((d, f), BF16), pltpu.VMEM((f, d), BF16),
                            pltpu.SemaphoreType.DMA((3,))],
        ),
        out_shape=jax.ShapeDtypeStruct((r, d), F32),
        compiler_params=_params(("arbitrary",)),
        name="moe_ffn",
    )(tile_expert, n_used, first, nxt, x_sorted, wg, wu, wd)


def _combine_kernel(x_ref, a_ref, b_ref, w_ref, g_ref, *rest):
    w = w_ref[...]
    xn = x_ref[...] + g_ref[0] * (w[:, 0:1] * a_ref[...] + w[:, 1:2] * b_ref[...])
    rest[-1 if len(rest) == 1 else -2][...] = xn
    if len(rest) > 1:
        gn_ref, sh_ref, sc_ref, _, h_ref = rest
        h_ref[...] = (_rms(xn, xn.shape[-1]) * gn_ref[...] * (1.0 + sc_ref[0]) + sh_ref[0]).astype(h_ref.dtype)


def _combine(x, ab, w_col, gate, rows_per_group, n_groups, next_norm=None):
    n, d = x.shape
    tm = ROW_TILE
    row = pl.BlockSpec((tm, d), lambda i: (i, 0))
    gs = _group_spec(d, tm, rows_per_group, n_groups)
    in_specs = [row, row, pl.BlockSpec((tm, d), lambda i: (i + n // tm, 0)),
                pl.BlockSpec((tm, LANES), lambda i: (i, 0)), gs]
    args = [x, ab, ab, w_col, gate]
    out_specs, out_shape = row, jax.ShapeDtypeStruct((n, d), F32)
    if next_norm is not None:
        gn, sh, sc = next_norm
        in_specs += [pl.BlockSpec((1, d), lambda i: (0, 0)), gs, gs]
        args += [gn.reshape(1, d), sh, sc]
        out_specs, out_shape = [row, row], [out_shape, jax.ShapeDtypeStruct((n, d), BF16)]
    return pl.pallas_call(
        _combine_kernel,
        grid=(n // tm,),
        in_specs=in_specs,
        out_specs=out_specs,
        out_shape=out_shape,
        compiler_params=_params(("parallel",)),
        name="moe_combine",
    )(*args)


def _moe_layer(x, g, sh, sc, gate, rw_pad, rb_col, wg, wu, wd, layer, rows_per_group, n_groups, next_norm):
    n, d = x.shape
    tm = MOE_ROW_TILE
    h, idx8, w_col, cnt = _norm_mod_route(x, g, sh, sc, rw_pad, rb_col, rows_per_group, n_groups)
    experts = jnp.arange(N_EXPERTS, dtype=jnp.int32)
    padded = (cnt[:, 0] + tm - 1) // tm * tm
    ends = jnp.cumsum(padded)
    starts = ends - padded
    e2, rank2 = idx8[0:2], idx8[2:4]
    dest = rank2 + jnp.sum(jnp.where(e2[..., None] == experts, starts, 0), axis=-1)
    r = (2 * n + N_EXPERTS * (tm - 1)) // tm * tm
    tok = jnp.tile(jnp.arange(n, dtype=jnp.int32), 2)
    src = (jnp.arange(r, dtype=jnp.int32) % n).at[dest.reshape(-1)].set(tok, unique_indices=True,
                                                                         mode="promise_in_bounds")
    tile_start = jnp.arange(r // tm, dtype=jnp.int32) * tm
    tile_expert = jnp.minimum(jnp.sum((ends[None, :] <= tile_start[:, None]).astype(jnp.int32), axis=1),
                              N_EXPERTS - 1)
    n_used = ends[-1:] // tm
    first = jnp.concatenate([jnp.ones((1,), jnp.int32), (tile_expert[1:] != tile_expert[:-1]).astype(jnp.int32)])
    later = (padded[None, :] > 0) & (experts[None, :] > experts[:, None])
    next_of = jnp.min(jnp.where(later, experts[None, :], N_EXPERTS), axis=1)
    next_of = jnp.where(next_of < N_EXPERTS, next_of, -1)
    nxt = jnp.sum(jnp.where(tile_expert[:, None] == experts, next_of, 0), axis=1)
    x_sorted = h.at[src].get(mode="promise_in_bounds")
    y_sorted = _moe_ffn(tile_expert, n_used, first, nxt, x_sorted, wg, wu, wd, layer)
    ab = y_sorted.at[dest.reshape(-1)].get(mode="promise_in_bounds")
    return _combine(x, ab, w_col, gate, rows_per_group, n_groups, next_norm)


def _axial_tables(rows, rot_dim):
    n_freq = rot_dim // 4
    inv_freq = jnp.power(ROPE_BASE, -jnp.arange(n_freq, dtype=F32) / n_freq)
    row = jnp.repeat(jnp.arange(rows, dtype=F32), GRID_W)
    col = jnp.tile(jnp.arange(GRID_W, dtype=F32), rows)
    ang = jnp.concatenate([row[:, None] * inv_freq, col[:, None] * inv_freq], axis=-1)
    return jnp.cos(ang), jnp.sin(ang)


def _flat_tables(c_lat, s_lat, batch, n_ctx_rows):
    c = jnp.concatenate([jnp.tile(c_lat, (batch, 1)), jnp.broadcast_to(c_lat[:1], (n_ctx_rows, LANES))])
    s = jnp.concatenate([jnp.tile(s_lat, (batch, 1)), jnp.zeros((n_ctx_rows, LANES), F32)])
    return c, s


def _spread_pairs(a, axis):
    x1, x2 = jnp.split(a, 2, axis=axis)
    z = jnp.zeros_like(x1)
    return jnp.concatenate([x1, z, x2, z], axis=axis)


def kernel(x, c, ctx, c_ctx, ada_w, ada_b, norm_mix_g, norm_ffn_g, mla_w_in, mla_q_a_g, mla_w_q_b, mla_kv_a_g,
           mla_w_kv_b, mla_q_norm_g, mla_k_norm_g, mla_w_o, diff_w_in, diff_q_norm_g, diff_k_norm_g, diff_lambda,
           diff_subln_g, diff_w_o, ret_w_in, ret_decay_fwd, ret_decay_bwd, ret_norm_g, ret_w_o, router_w,
           router_bias, moe_w_gate, moe_w_up, moe_w_down):
    batch, seq, d = x.shape
    n_ctx = ctx.shape[1]
    depth = ada_w.shape[0]
    n_lat, n_cx = batch * seq, batch * n_ctx
    n_groups = batch + 1
    assert n_groups <= SUBLANES and seq % MM_ROW_TILE == 0 and n_cx % MM_ROW_TILE == 0
    assert seq % RET_CHUNK == 0 and n_ctx % RET_CHUNK == 0 and n_lat % n_ctx == 0 and n_ctx == ROW_TILE

    x_lat, x_ctx = x.reshape(n_lat, d), ctx.reshape(n_cx, d)
    xa = None
    n_all = n_lat + n_cx

    cond = jnp.concatenate([c, c_ctx[None, :], jnp.zeros((SUBLANES - n_groups, d), F32)], axis=0)
    mods = _ada_mods(cond.T, ada_w, ada_b, n_groups)

    rows = seq // GRID_W
    cos64, sin64 = _axial_tables(rows, MLA_ROPE)
    zeros32 = jnp.zeros_like(cos64)
    mla_c, mla_s = _flat_tables(jnp.concatenate([cos64, zeros32, cos64, zeros32], axis=1),
                                jnp.concatenate([-sin64, zeros32, sin64, zeros32], axis=1), batch, n_cx)
    cos128, sin128 = _axial_tables(rows, DIFF_HEAD_DIM)
    diff_c, diff_s = _flat_tables(jnp.concatenate([cos128, cos128], axis=1),
                                  jnp.concatenate([-sin128, sin128], axis=1), batch, n_cx)
    cos256, sin256 = _axial_tables(rows, RET_DK)
    ret_c, ret_s = _flat_tables(cos256, sin256, batch, n_cx)

    rw_pad = jnp.concatenate([router_w, router_w, jnp.zeros((d, LANES - 2 * N_EXPERTS), F32)], axis=1)
    rb_col = router_bias.reshape(N_EXPERTS, 1)

    for i in range(depth):
        kind, j, last = i % N_MIXERS, i // N_MIXERS, i == depth - 1
        m = mods[i].reshape(SUBLANES, 6, 1, d)
        sh_m, sc_m, g_m, sh_f, sc_f, g_f = (m[:, t] for t in range(6))
        if i == 0:
            h = _norm_mod(x_lat, x_ctx, norm_mix_g[i], sh_m, sc_m, seq, n_groups)
        q_rows = n_lat if last else n_all

        if kind == 0:
            w_in = jnp.concatenate([mla_w_in[j][:, :MLA_Q_LORA + MLA_KV_LORA],
                                    _spread_pairs(mla_w_in[j][:, MLA_Q_LORA + MLA_KV_LORA:], 1)], axis=1)
            z = _proj(h, w_in, 0, w_in.shape[1], out_dtype=F32, tn=w_in.shape[1])
            wq = mla_w_q_b[j].reshape(MLA_Q_LORA, MLA_HEADS, MLA_NOPE + MLA_ROPE)
            wq = jnp.concatenate([wq[..., :MLA_NOPE], _spread_pairs(wq[..., MLA_NOPE:], 2)], axis=2)
            wq = wq.reshape(MLA_Q_LORA, MLA_HEADS * MLA_HEAD_PAD).astype(BF16)
            scale = float(MLA_NOPE + MLA_ROPE) ** -0.5 * LOG2E
            qg, kg = mla_q_norm_g[j], mla_k_norm_g[j]
            q = _mla_q(z, mla_q_a_g[j], wq, qg[:MLA_NOPE] * scale,
                       *_fold_rope(mla_c, mla_s, _spread_pairs(qg[MLA_NOPE:], 0), scale), q_rows)
            k, v = _mla_kv(z, mla_kv_a_g[j], mla_w_kv_b[j].astype(BF16), kg[:MLA_NOPE],
                           *_fold_rope(mla_c, mla_s, _spread_pairs(kg[MLA_NOPE:], 0), 1.0))
            attn = functools.partial(_attention, q, k, v, heads=MLA_HEADS, q_width=MLA_HEAD_PAD,
                                     v_width=MLA_V, dv=MLA_V, seq=seq, n_ctx=n_ctx, batch=batch,
                                     q_tile=MLA_Q_TILE)
            w_o = mla_w_o[j]
        elif kind == 1:
            w_in = diff_w_in[j]
            hd = DIFF_HEADS * 2 * DIFF_HEAD_DIM
            scale = float(DIFF_HEAD_DIM) ** -0.5 * LOG2E
            q = _proj(h[:q_rows], w_in, 0, hd, mode="norm_rope",
                      extras=_fold_rope(diff_c[:q_rows], diff_s[:q_rows], diff_q_norm_g[j], scale))
            k = _proj(h, w_in, hd, hd, mode="norm_rope", extras=_fold_rope(diff_c, diff_s, diff_k_norm_g[j], 1.0))
            v = _proj(h, w_in, 2 * hd, hd)
            lam_init = 0.8 - 0.6 * math.exp(-0.3 * i)
            attn = functools.partial(_attention, q, k, v, heads=DIFF_HEADS, q_width=2 * DIFF_HEAD_DIM,
                                     v_width=2 * DIFF_HEAD_DIM, dv=2 * DIFF_HEAD_DIM, seq=seq, n_ctx=n_ctx,
                                     batch=batch, q_tile=ATTN_Q_TILE,
                                     diff=(diff_lambda[j], diff_subln_g[j], lam_init))
            w_o = diff_w_o[j]
        else:
            w_in = ret_w_in[j]
            nq, nv = RET_HEADS * RET_DK, RET_HEADS * RET_DV
            q = _proj(h, w_in, 0, nq, mode="rope256", extras=(ret_c, ret_s))
            k_scale = float(RET_DK) ** -0.5
            k = _proj(h, w_in, nq, nq, mode="rope256", extras=(ret_c * k_scale, ret_s * k_scale))
            v = _proj(h, w_in, 2 * nq, nv)
            gate = _proj(h, w_in, 2 * nq + nv, nv, mode="silu")
            dec = lambda p: jnp.broadcast_to(p.astype(F32)[:, None, None], (RET_HEADS, 1, LANES))
            o_b = _retention(dec(ret_decay_bwd[j]), q, k, v, backward=True, seq=seq, n_ctx=n_ctx, batch=batch)
            y = _retention(dec(ret_decay_fwd[j]), q, k, v, backward=False, seq=seq, n_ctx=n_ctx, batch=batch,
                           finish=(o_b, gate, ret_norm_g[j].reshape(1, -1)))
            attn = None
            w_o = ret_w_o[j]

        y_tail = None
        if attn is not None:
            y = attn(lat_queries=True)
            if not last:
                y_tail = attn(lat_queries=False)
        res_in, res_tail = (x_lat, x_ctx) if xa is None else (xa, None)
        xa = _out_res(y, w_o, res_in, g_m, seq, n_groups, n_lat if last else n_all, y_tail, res_tail)
        next_norm = None
        if not last:
            m_next = mods[i + 1].reshape(SUBLANES, 6, 1, d)
            next_norm = (norm_mix_g[i + 1], m_next[:, 0], m_next[:, 1])
        res = _moe_layer(xa, norm_ffn_g[i], sh_f, sc_f, g_f, rw_pad, rb_col, moe_w_gate, moe_w_up, moe_w_down, i,
                         seq, n_groups, next_norm)
        xa, h = (res, None) if last else res
    return xa[:n_lat].reshape(batch, seq, d)
```
